```python
import jax
import jax.numpy as jnp
from jax import lax
import numpy as np

D_MODEL = 2048
BATCH = 16
SEQ = 256
DEPTH = 2
DEC_BATCH = 4
DEC_SEQ = 4096
PAST_LEN = 256

GRID_W = 64
N_BRANCH = 4
BRANCH_W = D_MODEL // N_BRANCH
FNET_GROUPS = 4
FNET_GW = BRANCH_W // FNET_GROUPS
POOL_WINDOWS = (2, 4, 8, 16)
POOL_GROUPS = len(POOL_WINDOWS)
POOL_GW = BRANCH_W // POOL_GROUPS
NA_HEADS = 4
NA_HEAD_DIM = BRANCH_W // NA_HEADS
NA_WIN_ROWS = 8
NA_WIN_COLS = 16
CONV_K = 31
CTX_Q_BLOCK = 128
D_FF = ((8 * D_MODEL // 3 + 255) // 256) * 256
N_EXPERTS = 8
TOP_K = 2
D_FF_EXPERT = 7 * D_MODEL // 2
N_DENSE = (DEPTH + 1) // 2
N_MOE = DEPTH // 2
IN_F = BRANCH_W
IN_P = BRANCH_W
IN_QKV = 3 * BRANCH_W
IN_C = 2 * BRANCH_W
IN_G = N_BRANCH * D_MODEL
IN_W = IN_F + IN_P + IN_QKV + IN_C + IN_G
IN_OFFSETS = [IN_F, IN_F + IN_P, IN_F + IN_P + IN_QKV, IN_F + IN_P + IN_QKV + IN_C]
RMS_EPS = 1e-6
LN_EPS = 1e-5

kernel_name = 'hybrid_dit_gated_mixers_step'


def rms_norm(x, g):
    xf = x.astype(jnp.float32)
    y = xf * lax.rsqrt(jnp.mean(xf * xf, axis=-1, keepdims=True) + RMS_EPS)
    return (y * g.astype(jnp.float32)).astype(x.dtype)


def layer_norm(x, g, b):
    xf = x.astype(jnp.float32)
    mu = jnp.mean(xf, axis=-1, keepdims=True)
    var = jnp.mean(jnp.square(xf - mu), axis=-1, keepdims=True)
    y = (xf - mu) * lax.rsqrt(var + LN_EPS)
    return (y * g.astype(jnp.float32) + b.astype(jnp.float32)).astype(x.dtype)


def adaln(cvec, w, b):
    m = jax.nn.silu(cvec) @ w + b
    return jnp.split(m[..., None, :], 6, axis=-1)


def fourier_mix(u):
    b, l, _ = u.shape
    ug = u.astype(jnp.float32).reshape(b, l, FNET_GROUPS, FNET_GW)
    y = jnp.fft.fft2(ug, axes=(1, 3), norm='ortho').real
    return y.reshape(b, l, BRANCH_W).astype(u.dtype)


def pool_mix(u, pool_w, pool_scale):
    b, l, _ = u.shape
    uf = u.astype(jnp.float32).reshape(b, l, POOL_GROUPS, POOL_GW)
    cs = jnp.concatenate([jnp.zeros_like(uf[:, :1]), jnp.cumsum(uf, axis=1)], axis=1)
    t = jnp.arange(l)
    means = []
    for gi, win in enumerate(POOL_WINDOWS):
        lo = win // 2
        hi = win - lo
        start = jnp.clip(t - lo, 0, l)
        end = jnp.clip(t + hi, 0, l)
        csg = cs[:, :, gi]
        s = jnp.take(csg, end, axis=1) - jnp.take(csg, start, axis=1)
        means.append(s / (end - start).astype(jnp.float32)[None, :, None])
    d = (jnp.stack(means, axis=2) - uf).astype(u.dtype)
    y = jnp.einsum('blgc,gcd->blgd', d, pool_w).reshape(b, l, BRANCH_W)
    return y * pool_scale


def conv_module(u2, conv_w, conv_b, ln_g, ln_b):
    a, gt = jnp.split(u2, 2, axis=-1)
    v = a * jax.nn.sigmoid(gt)
    y = lax.conv_general_dilated(
        v, conv_w[:, None, :], window_strides=(1,),
        padding=[(CONV_K // 2, CONV_K // 2)],
        dimension_numbers=('NWC', 'WIO', 'NWC'),
        feature_group_count=BRANCH_W) + conv_b
    return jax.nn.silu(layer_norm(y, ln_g, ln_b))


def split_heads(t):
    b, l, _ = t.shape
    return t.reshape(b, l, NA_HEADS, NA_HEAD_DIM).transpose(0, 2, 1, 3)


def context_attention(q, k, v):
    b, h, lc, d = q.shape
    scale = d ** -0.5
    nb = lc // CTX_Q_BLOCK
    qb = jnp.moveaxis(q.reshape(b, h, nb, CTX_Q_BLOCK, d), 2, 0)

    def block(qi):
        s = jnp.einsum('bhqd,bhkd->bhqk', qi, k).astype(jnp.float32) * scale
        p = jax.nn.softmax(s, axis=-1).astype(v.dtype)
        return jnp.einsum('bhqk,bhkd->bhqd', p, v)

    o = lax.map(block, qb)
    return jnp.moveaxis(o, 0, 2).reshape(b, h, lc, d)


def neighbourhood_attention(q, k, v, k_ctx, v_ctx, rpb):
    b, h, l, d = q.shape
    rows = l // GRID_W
    wr = min(NA_WIN_ROWS, rows)
    n_loc = wr * NA_WIN_COLS
    scale = d ** -0.5
    qg = q.reshape(b, h, rows, GRID_W, d)
    kg = k.reshape(b, h, rows, GRID_W, d)
    vg = v.reshape(b, h, rows, GRID_W, d)
    cols = jnp.arange(GRID_W)
    col_start = jnp.clip(cols - NA_WIN_COLS // 2, 0, GRID_W - NA_WIN_COLS)
    col_idx = col_start[:, None] + jnp.arange(NA_WIN_COLS)
    rel_col = col_idx - cols[:, None] + NA_WIN_COLS - 1
    rpb_c = jnp.take(rpb, rel_col, axis=2)

    def row_block(r):
        rs = jnp.clip(r - wr // 2, 0, rows - wr)
        qr = lax.dynamic_index_in_dim(qg, r, axis=2, keepdims=False)
        kr = lax.dynamic_slice_in_dim(kg, rs, wr, axis=2)
        vr = lax.dynamic_slice_in_dim(vg, rs, wr, axis=2)
        kwin = kr[:, :, :, col_idx]
        vwin = vr[:, :, :, col_idx]
        rel_row = rs + jnp.arange(wr) - r + NA_WIN_ROWS - 1
        bias = jnp.take(rpb_c, rel_row, axis=1).transpose(0, 2, 1, 3)
        s_loc = (jnp.einsum('bhqd,bhiqjd->bhqij', qr, kwin).astype(jnp.float32) * scale
                 + bias.astype(jnp.float32))
        s_ctx = jnp.einsum('bhqd,bhkd->bhqk', qr, k_ctx).astype(jnp.float32) * scale
        s = jnp.concatenate([s_loc.reshape(b, h, GRID_W, n_loc), s_ctx], axis=-1)
        p = jax.nn.softmax(s, axis=-1).astype(v.dtype)
        p_loc = p[..., :n_loc].reshape(b, h, GRID_W, wr, NA_WIN_COLS)
        return (jnp.einsum('bhqij,bhiqjd->bhqd', p_loc, vwin)
                + jnp.einsum('bhqk,bhkd->bhqd', p[..., n_loc:], v_ctx))

    o = lax.map(row_block, jnp.arange(rows))
    return jnp.transpose(o, (1, 2, 0, 3, 4)).reshape(b, h, l, d)


def swiglu(h, wg, wu, wd):
    return (jax.nn.silu(h @ wg) * (h @ wu)) @ wd


def moe_swiglu(h, w_router, b_router, wg, wu, wd):
    logits = (h @ w_router).astype(jnp.float32) + b_router.astype(jnp.float32)
    top_v, top_i = lax.top_k(logits, TOP_K)
    wts = jax.nn.softmax(top_v, axis=-1)
    gate = jnp.einsum('...k,...ke->...e', wts, jax.nn.one_hot(top_i, N_EXPERTS, dtype=jnp.float32))
    gate = gate.astype(h.dtype)
    y = jnp.zeros(h.shape[:-1] + (wd.shape[-1],), h.dtype)
    for e in range(N_EXPERTS):
        y = y + gate[..., e:e + 1] * swiglu(h, wg[e], wu[e], wd[e])
    return y


def trunk_layer(x, mod, norm_g, w_in, b_in, pool_w, pool_scale, rpb, conv_w, conv_b, conv_ln_g,
                conv_ln_b, w_branch, w_out, ffn_w, is_moe, kv_ctx):
    shift1, scale1, gate1, shift2, scale2, gate2 = mod
    b, l, _ = x.shape
    hmod = rms_norm(x, norm_g[0]) * (1 + scale1) + shift1
    proj = hmod @ w_in + b_in
    u_f, u_p, u_qkv, u_c, u_g = jnp.split(proj, IN_OFFSETS, axis=-1)
    o_f = fourier_mix(u_f)
    o_p = pool_mix(u_p, pool_w, pool_scale)
    q, k, v = [split_heads(t) for t in jnp.split(u_qkv, 3, axis=-1)]
    if kv_ctx is None:
        o_a = context_attention(q, k, v)
        kv_out = (k, v)
    else:
        o_a = neighbourhood_attention(q, k, v, kv_ctx[0], kv_ctx[1], rpb)
        kv_out = None
    o_a = o_a.transpose(0, 2, 1, 3).reshape(b, l, BRANCH_W)
    o_c = conv_module(u_c, conv_w, conv_b, conv_ln_g, conv_ln_b)
    branches = jnp.stack([o_f, o_p, o_a, o_c], axis=2)
    gates = jax.nn.sigmoid(u_g.reshape(b, l, N_BRANCH, D_MODEL))
    merged = jnp.einsum('blnd,blnd->bld', gates,
                        jnp.einsum('blnc,ncd->blnd', branches, w_branch))
    x = x + gate1 * rms_norm(merged @ w_out, norm_g[1])
    h2 = rms_norm(x, norm_g[2]) * (1 + scale2) + shift2
    f = moe_swiglu(h2, *ffn_w) if is_moe else swiglu(h2, *ffn_w)
    x = x + gate2 * rms_norm(f, norm_g[3])
    return x, kv_out


def setup_inputs(seed: int = 0) -> dict:
    key = jax.random.key(seed)
    ks = jax.random.split(key, 28)
    D = D_MODEL

    def nrm(k, shape, s):
        return jax.random.normal(k, shape, jnp.float32) * s

    return {
        'x_prompt': nrm(ks[0], (BATCH, SEQ, D), 1.0),
        'x_sample': nrm(ks[1], (DEC_BATCH, DEC_SEQ, D), 1.0),
        'cache_k': nrm(ks[2], (DEC_BATCH, DEPTH, NA_HEADS, PAST_LEN, NA_HEAD_DIM), 1.0),
        'cache_v': nrm(ks[3], (DEC_BATCH, DEPTH, NA_HEADS, PAST_LEN, NA_HEAD_DIM), 1.0),
        'c': nrm(ks[4], (DEC_BATCH, D), 1.0),
        'c_ctx': nrm(ks[5], (D,), 1.0),
        'w_ada': nrm(ks[6], (DEPTH, D, 6 * D), D ** -0.5),
        'b_ada': nrm(ks[7], (DEPTH, 6 * D), 0.01),
        'norm_g': 1.0 + nrm(ks[8], (DEPTH, 4, D), 0.02),
        'w_in': nrm(ks[9], (DEPTH, D, IN_W), D ** -0.5),
        'b_in': nrm(ks[10], (DEPTH, IN_W), 0.01),
        'pool_w': nrm(ks[11], (DEPTH, POOL_GROUPS, POOL_GW, POOL_GW), POOL_GW ** -0.5),
        'pool_scale': 1.0 + nrm(ks[12], (DEPTH, BRANCH_W), 0.02),
        'rpb': nrm(ks[13], (DEPTH, NA_HEADS, 2 * NA_WIN_ROWS - 1, 2 * NA_WIN_COLS - 1), 0.02),
        'conv_w': nrm(ks[14], (DEPTH, CONV_K, BRANCH_W), CONV_K ** -0.5),
        'conv_b': nrm(ks[15], (DEPTH, BRANCH_W), 0.01),
        'conv_ln_g': 1.0 + nrm(ks[16], (DEPTH, BRANCH_W), 0.02),
        'conv_ln_b': nrm(ks[17], (DEPTH, BRANCH_W), 0.01),
        'w_branch': nrm(ks[18], (DEPTH, N_BRANCH, BRANCH_W, D), BRANCH_W ** -0.5),
        'w_out': nrm(ks[19], (DEPTH, D, D), D ** -0.5),
        'w_gate_d': nrm(ks[20], (N_DENSE, D, D_FF), D ** -0.5),
        'w_up_d': nrm(ks[21], (N_DENSE, D, D_FF), D ** -0.5),
        'w_down_d': nrm(ks[22], (N_DENSE, D_FF, D), D_FF ** -0.5),
        'w_router': nrm(ks[23], (N_MOE, D, N_EXPERTS), D ** -0.5),
        'b_router': nrm(ks[24], (N_MOE, N_EXPERTS), 0.01),
        'w_gate_e': nrm(ks[25], (N_MOE, N_EXPERTS, D, D_FF_EXPERT), D ** -0.5),
        'w_up_e': nrm(ks[26], (N_MOE, N_EXPERTS, D, D_FF_EXPERT), D ** -0.5),
        'w_down_e': nrm(ks[27], (N_MOE, N_EXPERTS, D_FF_EXPERT, D), D_FF_EXPERT ** -0.5),
    }


def reference(x_prompt, x_sample, cache_k, cache_v, c, c_ctx, w_ada, b_ada, norm_g, w_in, b_in,
              pool_w, pool_scale, rpb, conv_w, conv_b, conv_ln_g, conv_ln_b, w_branch, w_out,
              w_gate_d, w_up_d, w_down_d, w_router, b_router, w_gate_e, w_up_e, w_down_e):
    y_p = x_prompt
    y_s = x_sample
    new_k = []
    new_v = []
    for l in range(DEPTH):
        i = l // 2
        is_moe = (l % 2 == 1)
        if is_moe:
            ffn_w = (w_router[i], b_router[i], w_gate_e[i], w_up_e[i], w_down_e[i])
        else:
            ffn_w = (w_gate_d[i], w_up_d[i], w_down_d[i])
        mod_ctx = adaln(c_ctx, w_ada[l], b_ada[l])
        mod_lat = adaln(c, w_ada[l], b_ada[l])
        y_p, kv = trunk_layer(y_p, mod_ctx, norm_g[l], w_in[l], b_in[l], pool_w[l], pool_scale[l],
                              rpb[l], conv_w[l], conv_b[l], conv_ln_g[l], conv_ln_b[l], w_branch[l],
                              w_out[l], ffn_w, is_moe, None)
        new_k.append(kv[0])
        new_v.append(kv[1])
        y_s, _ = trunk_layer(y_s, mod_lat, norm_g[l], w_in[l], b_in[l], pool_w[l], pool_scale[l],
                             rpb[l], conv_w[l], conv_b[l], conv_ln_g[l], conv_ln_b[l], w_branch[l],
                             w_out[l], ffn_w, is_moe, (cache_k[:, l], cache_v[:, l]))
    new_cache_k = jnp.stack(new_k, axis=1)
    new_cache_v = jnp.stack(new_v, axis=1)
    return (y_p, y_s, new_cache_k, new_cache_v)
```

```python
import functools

import numpy as np
import jax
import jax.numpy as jnp
from jax import lax
from jax.experimental import pallas as pl
from jax.experimental.pallas import tpu as pltpu

F32 = jnp.float32
BF16 = jnp.bfloat16

N_BRANCH = 4
FNET_GROUPS = 4
POOL_WINDOWS = (2, 4, 8, 16)
NA_HEADS = 4
NA_WIN_ROWS = 8
NA_WIN_COLS = 16
GRID_W = 64
CONV_K = 31
TOP_K = 2
RMS_EPS = 1e-6
LN_EPS = 1e-5
MASK_VALUE = -1e30
MOD_ROWS = 8
POOL_HALO = 8
CONV_HALO = 16
NA_Q_ROWS = 8
NA_K_ROWS = 16
VMEM_LIMIT_BYTES = 48 * 1024 * 1024


def _cparams(*sem):
    return pltpu.CompilerParams(dimension_semantics=sem, vmem_limit_bytes=VMEM_LIMIT_BYTES)


def _silu(x):
    return x * jax.nn.sigmoid(x)


def _rms(x, g):
    return x * lax.rsqrt(jnp.mean(x * x, axis=-1, keepdims=True) + RMS_EPS) * g


def _nt_dot(a, b):
    return lax.dot_general(a, b, (((1,), (1,)), ((), ())), preferred_element_type=F32)


def _ada_kernel(c_ref, w_ref, b_ref, o_ref):
    s = _silu(c_ref[...]).astype(BF16)
    o_ref[...] = jnp.dot(s, w_ref[...].astype(BF16), preferred_element_type=F32) + b_ref[...]


def _ada_call(cvec, w_ada, b_ada):
    depth, d, n = w_ada.shape
    tn = 512
    return pl.pallas_call(
        _ada_kernel,
        grid=(depth, n // tn),
        in_specs=[pl.BlockSpec((MOD_ROWS, d), lambda l, j: (0, 0)),
                  pl.BlockSpec((None, d, tn), lambda l, j: (l, 0, j)),
                  pl.BlockSpec((None, 1, tn), lambda l, j: (l, 0, j))],
        out_specs=pl.BlockSpec((None, MOD_ROWS, tn), lambda l, j: (l, 0, j)),
        out_shape=jax.ShapeDtypeStruct((depth, MOD_ROWS, n), F32),
        compiler_params=_cparams("parallel", "parallel"),
        name="adaln",
    )(cvec, w_ada, b_ada.reshape(depth, 1, n))


def _in_kernel(x_ref, mod_ref, g_ref, w_ref, b_ref, o_ref, h_ref):
    @pl.when(pl.program_id(1) == 0)
    def _():
        h = _rms(x_ref[...], g_ref[...]) * (1.0 + mod_ref[1:2, :]) + mod_ref[0:1, :]
        h_ref[...] = h.astype(BF16)

    o_ref[...] = jnp.dot(h_ref[...], w_ref[...], preferred_element_type=F32) + b_ref[...]


def _in_call(x, mod, g, w, b, mod_row, tm, tn):
    t, d = x.shape
    n = w.shape[1]
    return pl.pallas_call(
        _in_kernel,
        grid=(t // tm, n // tn),
        in_specs=[pl.BlockSpec((tm, d), lambda i, j: (i, 0)),
                  pl.BlockSpec((None, 6, d), lambda i, j: (mod_row(i * tm), 0, 0)),
                  pl.BlockSpec((1, d), lambda i, j: (0, 0)),
                  pl.BlockSpec((d, tn), lambda i, j: (0, j)),
                  pl.BlockSpec((1, tn), lambda i, j: (0, j))],
        out_specs=pl.BlockSpec((tm, tn), lambda i, j: (i, j)),
        out_shape=jax.ShapeDtypeStruct((t, n), F32),
        scratch_shapes=[pltpu.VMEM((tm, d), BF16)],
        compiler_params=_cparams("parallel", "arbitrary"),
        name="in_proj",
    )(x, mod, g, w, b)


def _dft_mats(n):
    scale = 1.0 / np.sqrt(n)
    j = jnp.arange(n, dtype=jnp.int32)
    if n <= 1024:
        ang = ((j[:, None] * j[None, :]) % n).astype(F32) * (2.0 * np.pi / n)
        return (jnp.cos(ang) * scale).astype(BF16), (jnp.sin(ang) * scale).astype(BF16)
    base = 64
    hi = n // base
    k1 = jnp.arange(hi, dtype=jnp.int32)
    k0 = jnp.arange(base, dtype=jnp.int32)
    a = ((j[:, None] * k1[None, :]) % hi).astype(F32) * (2.0 * np.pi / hi)
    b = ((j[:, None] * k0[None, :]) % n).astype(F32) * (2.0 * np.pi / n)
    ca, sa = jnp.cos(a)[:, :, None], jnp.sin(a)[:, :, None]
    cb, sb = jnp.cos(b)[:, None, :] * scale, jnp.sin(b)[:, None, :] * scale
    c = (ca * cb - sa * sb).reshape(n, n)
    s = (sa * cb + ca * sb).reshape(n, n)
    return c.astype(BF16), s.astype(BF16)


def _channel_dft_mats(width, groups):
    gw = width // groups
    k = np.arange(gw)
    ang = 2.0 * np.pi * ((k[:, None] * k[None, :]) % gw) / gw
    c = np.zeros((width, width), np.float32)
    s = np.zeros((width, width), np.float32)
    for g in range(groups):
        sl = slice(g * gw, (g + 1) * gw)
        c[sl, sl] = np.cos(ang) / np.sqrt(gw)
        s[sl, sl] = np.sin(ang) / np.sqrt(gw)
    return jnp.asarray(c, BF16), jnp.asarray(s, BF16)


def _fnet1_kernel(u_ref, cc_ref, sc_ref, vc_ref, vs_ref):
    u = u_ref[...].astype(BF16)
    vc_ref[...] = jnp.dot(u, cc_ref[...], preferred_element_type=F32).astype(BF16)
    vs_ref[...] = jnp.dot(u, sc_ref[...], preferred_element_type=F32).astype(BF16)


def _fnet2_kernel(c_ref, s_ref, vc_ref, vs_ref, o_ref, acc_ref):
    k = pl.program_id(2)

    @pl.when(k == 0)
    def _():
        acc_ref[...] = jnp.zeros_like(acc_ref)

    acc_ref[...] += (jnp.dot(c_ref[...], vc_ref[...], preferred_element_type=F32)
                     - jnp.dot(s_ref[...], vs_ref[...], preferred_element_type=F32))

    @pl.when(k == pl.num_programs(2) - 1)
    def _():
        o_ref[...] = acc_ref[...].astype(o_ref.dtype)


def _fourier_call(proj, row_base, nb, seq, bw, cc, sc):
    c_l, s_l = _dft_mats(seq)
    tm1 = min(seq, 512)
    nt1 = seq // tm1
    base1 = row_base // tm1
    vc, vs = pl.pallas_call(
        _fnet1_kernel,
        grid=(nb, nt1),
        in_specs=[pl.BlockSpec((tm1, bw), lambda b, t: (base1 + b * nt1 + t, 0)),
                  pl.BlockSpec((bw, bw), lambda b, t: (0, 0)),
                  pl.BlockSpec((bw, bw), lambda b, t: (0, 0))],
        out_specs=[pl.BlockSpec((tm1, bw), lambda b, t: (t, b)),
                   pl.BlockSpec((tm1, bw), lambda b, t: (t, b))],
        out_shape=[jax.ShapeDtypeStruct((seq, nb * bw), BF16)] * 2,
        compiler_params=_cparams("parallel", "parallel"),
        name="fnet_channels",
    )(proj, cc, sc)
    tm2 = min(seq, 1024)
    tk = min(seq, 512)
    nt2 = seq // tm2
    return pl.pallas_call(
        _fnet2_kernel,
        grid=(nt2, nb, seq // tk),
        in_specs=[pl.BlockSpec((tm2, tk), lambda i, j, k: (i, k)),
                  pl.BlockSpec((tm2, tk), lambda i, j, k: (i, k)),
                  pl.BlockSpec((tk, bw), lambda i, j, k: (k, j)),
                  pl.BlockSpec((tk, bw), lambda i, j, k: (k, j))],
        out_specs=pl.BlockSpec((tm2, bw), lambda i, j, k: (j * nt2 + i, 0)),
        out_shape=jax.ShapeDtypeStruct((nb * seq, bw), BF16),
        scratch_shapes=[pltpu.VMEM((tm2, bw), F32)],
        compiler_params=_cparams("parallel", "parallel", "arbitrary"),
        name="fnet_positions",
    )(c_l, s_l, vc, vs)


def _seq_position(row0, ns_rows, dec_seq, seq):
    is_lat = row0 < ns_rows
    seq_len = jnp.where(is_lat, dec_seq, seq)
    pos0 = jnp.where(is_lat, row0 & (dec_seq - 1), row0 & (seq - 1))
    return seq_len, pos0


def _pool_kernel(prev_ref, cur_ref, nxt_ref, w_ref, sc_ref, o_ref, ext_ref, *,
                 tm, ns_rows, dec_seq, seq):
    seq_len, pos0 = _seq_position(pl.program_id(0) * tm, ns_rows, dec_seq, seq)
    h = POOL_HALO
    ext_ref[0:h, :] = jnp.where(pos0 == 0, 0.0, prev_ref[...])
    ext_ref[h:h + tm, :] = cur_ref[...]
    ext_ref[h + tm:2 * h + tm, :] = jnp.where(pos0 + tm == seq_len, 0.0, nxt_ref[...])
    t = pos0 + lax.broadcasted_iota(jnp.int32, (tm, 1), 0)
    gw = cur_ref.shape[1] // len(POOL_WINDOWS)
    for gi, win in enumerate(POOL_WINDOWS):
        lo = win // 2
        hi = win - lo
        cols = slice(gi * gw, (gi + 1) * gw)
        s = ext_ref[h - lo:h - lo + tm, cols]
        for j in range(1 - lo, hi):
            s = s + ext_ref[h + j:h + j + tm, cols]
        cnt = (jnp.minimum(t + hi, seq_len) - jnp.maximum(t - lo, 0)).astype(F32)
        dlt = s / cnt - cur_ref[:, cols]
        y = jnp.dot(dlt.astype(BF16), w_ref[gi], preferred_element_type=F32) * sc_ref[:, cols]
        o_ref[:, cols] = y.astype(o_ref.dtype)


def _pool_call(proj, pool_w, pool_scale, col_blk, bw, tm, ns_rows, dec_seq, seq):
    t = proj.shape[0]
    h = POOL_HALO
    r = tm // h
    last = t // h - 1
    gw = bw // len(POOL_WINDOWS)
    return pl.pallas_call(
        functools.partial(_pool_kernel, tm=tm, ns_rows=ns_rows, dec_seq=dec_seq, seq=seq),
        grid=(t // tm,),
        in_specs=[pl.BlockSpec((h, bw), lambda i: (jnp.maximum(i * r - 1, 0), col_blk)),
                  pl.BlockSpec((tm, bw), lambda i: (i, col_blk)),
                  pl.BlockSpec((h, bw), lambda i: (jnp.minimum((i + 1) * r, last), col_blk)),
                  pl.BlockSpec((len(POOL_WINDOWS), gw, gw), lambda i: (0, 0, 0)),
                  pl.BlockSpec((1, bw), lambda i: (0, 0))],
        out_specs=pl.BlockSpec((tm, bw), lambda i: (i, 0)),
        out_shape=jax.ShapeDtypeStruct((t, bw), BF16),
        scratch_shapes=[pltpu.VMEM((tm + 2 * h, bw), F32)],
        compiler_params=_cparams("parallel"),
        name="pool_mix",
    )(proj, proj, proj, pool_w, pool_scale)


def _conv_kernel(ap_ref, ac_ref, an_ref, gp_ref, gc_ref, gn_ref, w_ref, b_ref, lg_ref, lb_ref,
                 o_ref, ext_ref, y_ref, *, tm, ns_rows, dec_seq, seq):
    seq_len, pos0 = _seq_position(pl.program_id(0) * tm, ns_rows, dec_seq, seq)
    h = CONV_HALO
    ext_ref[0:h, :] = jnp.where(pos0 == 0, 0.0, ap_ref[...] * jax.nn.sigmoid(gp_ref[...]))
    ext_ref[h:h + tm, :] = ac_ref[...] * jax.nn.sigmoid(gc_ref[...])
    ext_ref[h + tm:2 * h + tm, :] = jnp.where(pos0 + tm == seq_len, 0.0,
                                              an_ref[...] * jax.nn.sigmoid(gn_ref[...]))
    bw = ac_ref.shape[1]
    lanes = 128
    for c in range(bw // lanes):
        cols = slice(c * lanes, (c + 1) * lanes)
        acc = jnp.zeros((tm, lanes), F32)
        for k in range(CONV_K):
            off = h + k - CONV_K // 2
            acc = acc + ext_ref[off:off + tm, cols] * w_ref[k:k + 1, cols]
        y_ref[:, cols] = acc + b_ref[:, cols]
    y = y_ref[...]
    mu = jnp.mean(y, axis=-1, keepdims=True)
    var = jnp.mean(jnp.square(y - mu), axis=-1, keepdims=True)
    z = (y - mu) * lax.rsqrt(var + LN_EPS) * lg_ref[...] + lb_ref[...]
    o_ref[...] = _silu(z).astype(o_ref.dtype)


def _conv_call(proj, conv_w, conv_b, ln_g, ln_b, col_blk, bw, tm, ns_rows, dec_seq, seq):
    t = proj.shape[0]
    h = CONV_HALO
    r = tm // h
    last = t // h - 1
    prev = lambda i: jnp.maximum(i * r - 1, 0)
    nxt = lambda i: jnp.minimum((i + 1) * r, last)
    vec = pl.BlockSpec((1, bw), lambda i: (0, 0))
    return pl.pallas_call(
        functools.partial(_conv_kernel, tm=tm, ns_rows=ns_rows, dec_seq=dec_seq, seq=seq),
        grid=(t // tm,),
        in_specs=[pl.BlockSpec((h, bw), lambda i: (prev(i), col_blk)),
                  pl.BlockSpec((tm, bw), lambda i: (i, col_blk)),
                  pl.BlockSpec((h, bw), lambda i: (nxt(i), col_blk)),
                  pl.BlockSpec((h, bw), lambda i: (prev(i), col_blk + 1)),
                  pl.BlockSpec((tm, bw), lambda i: (i, col_blk + 1)),
                  pl.BlockSpec((h, bw), lambda i: (nxt(i), col_blk + 1)),
                  pl.BlockSpec((CONV_K, bw), lambda i: (0, 0)),
                  vec, vec, vec],
        out_specs=pl.BlockSpec((tm, bw), lambda i: (i, 0)),
        out_shape=jax.ShapeDtypeStruct((t, bw), BF16),
        scratch_shapes=[pltpu.VMEM((tm + 2 * h, bw), F32), pltpu.VMEM((tm, bw), F32)],
        compiler_params=_cparams("parallel"),
        name="conv_module",
    )(proj, proj, proj, proj, proj, proj, conv_w, conv_b, ln_g, ln_b)


def _ctx_attn_kernel(q_ref, k_ref, v_ref, o_ref, ko_ref, vo_ref, *, scale):
    k = k_ref[...]
    v = v_ref[...]
    s = _nt_dot(q_ref[...].astype(BF16), k.astype(BF16)) * scale
    p = jnp.exp(s - jnp.max(s, axis=-1, keepdims=True))
    l = jnp.sum(p, axis=-1, keepdims=True)
    o = jnp.dot(p.astype(BF16), v.astype(BF16), preferred_element_type=F32)
    o_ref[...] = (o / l).astype(o_ref.dtype)
    ko_ref[...] = k
    vo_ref[...] = v


def _ctx_attn_call(proj, row_base, nb, seq, q_col, hd):
    base = row_base // seq
    nh = NA_HEADS
    kv_spec = pl.BlockSpec((None, None, seq, hd), lambda b, h: (b, h, 0, 0))
    return pl.pallas_call(
        functools.partial(_ctx_attn_kernel, scale=hd ** -0.5),
        grid=(nb, nh),
        in_specs=[pl.BlockSpec((seq, hd), lambda b, h: (base + b, q_col + h)),
                  pl.BlockSpec((seq, hd), lambda b, h: (base + b, q_col + nh + h)),
                  pl.BlockSpec((seq, hd), lambda b, h: (base + b, q_col + 2 * nh + h))],
        out_specs=[pl.BlockSpec((seq, hd), lambda b, h: (b, h)), kv_spec, kv_spec],
        out_shape=[jax.ShapeDtypeStruct((nb * seq, nh * hd), BF16),
                   jax.ShapeDtypeStruct((nb, nh, seq, hd), F32),
                   jax.ShapeDtypeStruct((nb, nh, seq, hd), F32)],
        compiler_params=_cparams("parallel", "parallel"),
        name="context_attention",
    )(proj, proj, proj)


def _na_bias_tables(rpb, rows):
    nh = rpb.shape[0]
    w = GRID_W
    c = np.arange(w)
    cs = np.clip(c - NA_WIN_COLS // 2, 0, w - NA_WIN_COLS)
    col_ok = (c[None, :] >= cs[:, None]) & (c[None, :] < cs[:, None] + NA_WIN_COLS)
    rel_col = np.clip(c[None, :] - c[:, None] + NA_WIN_COLS - 1, 0, 2 * NA_WIN_COLS - 2)
    planes = jnp.where(col_ok[None, None], rpb[:, :, rel_col], MASK_VALUE)
    planes = jnp.concatenate([planes, jnp.full((nh, 1, w, w), MASK_VALUE, rpb.dtype)], axis=1)
    masked_plane = 2 * NA_WIN_ROWS - 1
    variants, var_ids = [], []
    for kb in range(rows // NA_Q_ROWS):
        r = kb * NA_Q_ROWS + np.arange(NA_Q_ROWS)
        rs = np.clip(r - NA_WIN_ROWS // 2, 0, rows - NA_WIN_ROWS)
        kr = _na_key_row0(kb, rows) + np.arange(NA_K_ROWS)
        ok = (kr[None, :] >= rs[:, None]) & (kr[None, :] < rs[:, None] + NA_WIN_ROWS)
        assert ok.sum() == NA_Q_ROWS * NA_WIN_ROWS
        plane = np.where(ok, kr[None, :] - r[:, None] + NA_WIN_ROWS - 1, masked_plane)
        for vi, known in enumerate(variants):
            if np.array_equal(known, plane):
                var_ids.append(vi)
                break
        else:
            var_ids.append(len(variants))
            variants.append(plane)
    idx = np.stack(variants)
    tbl = planes[:, idx]
    tbl = tbl.transpose(0, 1, 2, 4, 3, 5).reshape(nh, len(variants), NA_Q_ROWS * w, NA_K_ROWS * w)
    return tbl, jnp.asarray(np.array(var_ids, np.int32))


def _na_key_row0(kb, rows):
    lo = kb * NA_Q_ROWS - NA_WIN_ROWS // 2
    if isinstance(kb, (int, np.integer)):
        return int(np.clip(lo, 0, rows - NA_K_ROWS))
    return jnp.clip(lo, 0, rows - NA_K_ROWS)


def _na_attn_kernel(var_ref, q_ref, k_ref, v_ref, kc_ref, vc_ref, bias_ref, o_ref, *, rows, scale):
    del var_ref
    nk = NA_K_ROWS * GRID_W
    start = pl.multiple_of(_na_key_row0(pl.program_id(2), rows) * GRID_W, 256)
    kw = k_ref[pl.ds(start, nk), :].astype(BF16)
    vw = v_ref[pl.ds(start, nk), :].astype(BF16)
    q = q_ref[...].astype(BF16)
    s_loc = _nt_dot(q, kw) * scale + bias_ref[...]
    s_ctx = _nt_dot(q, kc_ref[...].astype(BF16)) * scale
    m = jnp.maximum(jnp.max(s_loc, axis=-1, keepdims=True), jnp.max(s_ctx, axis=-1, keepdims=True))
    p_loc = jnp.exp(s_loc - m)
    p_ctx = jnp.exp(s_ctx - m)
    l = jnp.sum(p_loc, axis=-1, keepdims=True) + jnp.sum(p_ctx, axis=-1, keepdims=True)
    o = (jnp.dot(p_loc.astype(BF16), vw, preferred_element_type=F32)
         + jnp.dot(p_ctx.astype(BF16), vc_ref[...].astype(BF16), preferred_element_type=F32))
    o_ref[...] = (o / l).astype(o_ref.dtype)


def _na_attn_call(proj, cache_k, cache_v, layer, rpb, nb, dec_seq, q_col, hd):
    rows = dec_seq // GRID_W
    assert rows >= NA_K_ROWS and rows % NA_Q_ROWS == 0
    nh = NA_HEADS
    nq = NA_Q_ROWS * GRID_W
    nkb = rows // NA_Q_ROWS
    past = cache_k.shape[3]
    tbl, var_ids = _na_bias_tables(rpb, rows)
    ctx_spec = pl.BlockSpec((None, None, None, past, hd), lambda b, h, k, var: (b, layer, h, 0, 0))
    grid_spec = pltpu.PrefetchScalarGridSpec(
        num_scalar_prefetch=1,
        grid=(nb, nh, nkb),
        in_specs=[pl.BlockSpec((nq, hd), lambda b, h, k, var: (b * nkb + k, q_col + h)),
                  pl.BlockSpec((dec_seq, hd), lambda b, h, k, var: (b, q_col + nh + h)),
                  pl.BlockSpec((dec_seq, hd), lambda b, h, k, var: (b, q_col + 2 * nh + h)),
                  ctx_spec, ctx_spec,
                  pl.BlockSpec((None, None, nq, NA_K_ROWS * GRID_W),
                               lambda b, h, k, var: (h, var[k], 0, 0))],
        out_specs=pl.BlockSpec((nq, hd), lambda b, h, k, var: (b * nkb + k, h)),
    )
    return pl.pallas_call(
        functools.partial(_na_attn_kernel, rows=rows, scale=hd ** -0.5),
        grid_spec=grid_spec,
        out_shape=jax.ShapeDtypeStruct((nb * dec_seq, nh * hd), BF16),
        compiler_params=_cparams("parallel", "parallel", "arbitrary"),
        name="neighbourhood_attention",
    )(var_ids, proj, proj, proj, cache_k, cache_v, tbl)


def _merge_kernel(f_ref, p_ref, a_ref, c_ref, g0_ref, g1_ref, g2_ref, g3_ref, w_ref, o_ref):
    acc = None
    for n, (br, gt) in enumerate(((f_ref, g0_ref), (p_ref, g1_ref), (a_ref, g2_ref), (c_ref, g3_ref))):
        y = jax.nn.sigmoid(gt[...]) * jnp.dot(br[...], w_ref[n], preferred_element_type=F32)
        acc = y if acc is None else acc + y
    o_ref[...] = acc.astype(o_ref.dtype)


def _merge_call(branches, proj, gate_col0, w_branch, tm, tn):
    t, bw = branches[0].shape
    d = w_branch.shape[2]
    nj = d // tn
    c0 = gate_col0 // tn
    br_spec = pl.BlockSpec((tm, bw), lambda i, j: (i, 0))
    gate_specs = [pl.BlockSpec((tm, tn), functools.partial(lambda i, j, n: (i, c0 + n * nj + j), n=n))
                  for n in range(N_BRANCH)]
    return pl.pallas_call(
        _merge_kernel,
        grid=(t // tm, nj),
        in_specs=[br_spec] * N_BRANCH + gate_specs
                 + [pl.BlockSpec((N_BRANCH, bw, tn), lambda i, j: (0, 0, j))],
        out_specs=pl.BlockSpec((tm, tn), lambda i, j: (i, j)),
        out_shape=jax.ShapeDtypeStruct((t, d), BF16),
        compiler_params=_cparams("parallel", "parallel"),
        name="branch_merge",
    )(*branches, proj, proj, proj, proj, w_branch)


def _out_kernel(a_ref, w_ref, x_ref, mod_ref, g_ref, o_ref, acc_ref, *, gate_row):
    k = pl.program_id(1)

    @pl.when(k == 0)
    def _():
        acc_ref[...] = jnp.zeros_like(acc_ref)

    acc_ref[...] += jnp.dot(a_ref[...], w_ref[...], preferred_element_type=F32)

    @pl.when(k == pl.num_programs(1) - 1)
    def _():
        o_ref[...] = x_ref[...] + mod_ref[gate_row:gate_row + 1, :] * _rms(acc_ref[...], g_ref[...])


def _out_call(a, w, x, mod, g, gate_row, mod_row, tm, tk):
    t, d = x.shape
    kdim = a.shape[1]
    return pl.pallas_call(
        functools.partial(_out_kernel, gate_row=gate_row),
        grid=(t // tm, kdim // tk),
        in_specs=[pl.BlockSpec((tm, tk), lambda i, k: (i, k)),
                  pl.BlockSpec((tk, d), lambda i, k: (k, 0)),
                  pl.BlockSpec((tm, d), lambda i, k: (i, 0)),
                  pl.BlockSpec((None, 6, d), lambda i, k: (mod_row(i * tm), 0, 0)),
                  pl.BlockSpec((1, d), lambda i, k: (0, 0))],
        out_specs=pl.BlockSpec((tm, d), lambda i, k: (i, 0)),
        out_shape=jax.ShapeDtypeStruct((t, d), F32),
        scratch_shapes=[pltpu.VMEM((tm, d), F32)],
        compiler_params=_cparams("parallel", "arbitrary"),
        name="proj_norm_residual",
    )(a, w, x, mod, g)


def _ffn_up_kernel(x_ref, mod_ref, g_ref, wg_ref, wu_ref, *rest, blocks_per_expert):
    if blocks_per_expert:
        gate_ref, o_ref, h_ref = rest
    else:
        o_ref, h_ref = rest
    j = pl.program_id(1)

    @pl.when(j == 0)
    def _():
        h = _rms(x_ref[...], g_ref[...]) * (1.0 + mod_ref[4:5, :]) + mod_ref[3:4, :]
        h_ref[...] = h.astype(BF16)

    h = h_ref[...]
    y = _silu(jnp.dot(h, wg_ref[...], preferred_element_type=F32)) * jnp.dot(
        h, wu_ref[...], preferred_element_type=F32)
    if blocks_per_expert:
        gates = gate_ref[...]
        lane = lax.broadcasted_iota(jnp.int32, gates.shape, 1)
        y = y * jnp.sum(jnp.where(lane == j // blocks_per_expert, gates, 0.0), axis=-1, keepdims=True)
    o_ref[...] = y.astype(o_ref.dtype)


def _ffn_up_call(x, mod, g, wg, wu, gates, mod_row, tm, tn):
    t, d = x.shape
    ne, _, f = wg.shape
    bpe = f // tn
    w_spec = pl.BlockSpec((None, d, tn), lambda i, j: (j // bpe, 0, j % bpe))
    in_specs = [pl.BlockSpec((tm, d), lambda i, j: (i, 0)),
                pl.BlockSpec((None, 6, d), lambda i, j: (mod_row(i * tm), 0, 0)),
                pl.BlockSpec((1, d), lambda i, j: (0, 0)),
                w_spec, w_spec]
    args = [x, mod, g, wg, wu]
    if gates is not None:
        in_specs.append(pl.BlockSpec((tm, gates.shape[1]), lambda i, j: (i, 0)))
        args.append(gates)
    return pl.pallas_call(
        functools.partial(_ffn_up_kernel, blocks_per_expert=bpe if gates is not None else 0),
        grid=(t // tm, ne * bpe),
        in_specs=in_specs,
        out_specs=pl.BlockSpec((tm, tn), lambda i, j: (i, j)),
        out_shape=jax.ShapeDtypeStruct((t, ne * f), BF16),
        scratch_shapes=[pltpu.VMEM((tm, d), BF16)],
        compiler_params=_cparams("parallel", "arbitrary"),
        name="ffn_up",
    )(*args)


def _split_bf16(x):
    hi = x.astype(BF16)
    return hi, (x - hi.astype(F32)).astype(BF16)


def _router_kernel(x_ref, mod_ref, g_ref, w_ref, b_ref, o_ref, *, n_experts):
    h = _rms(x_ref[...], g_ref[...]) * (1.0 + mod_ref[4:5, :]) + mod_ref[3:4, :]
    h_hi, h_lo = _split_bf16(h)
    w_hi, w_lo = _split_bf16(w_ref[...])
    logits = (jnp.dot(h_hi, w_hi, preferred_element_type=F32)
              + jnp.dot(h_lo, w_hi, preferred_element_type=F32)
              + jnp.dot(h_hi, w_lo, preferred_element_type=F32)) + b_ref[...]
    lane = lax.broadcasted_iota(jnp.int32, logits.shape, 1).astype(F32)
    neg = -jnp.inf
    no_lane = float(logits.shape[1])
    logits = jnp.where(lane < n_experts, logits, neg)
    m1 = jnp.max(logits, axis=-1, keepdims=True)
    i1 = jnp.min(jnp.where(logits == m1, lane, no_lane), axis=-1, keepdims=True)
    rest = jnp.where(lane == i1, neg, logits)
    m2 = jnp.max(rest, axis=-1, keepdims=True)
    i2 = jnp.min(jnp.where(rest == m2, lane, no_lane), axis=-1, keepdims=True)
    e2 = jnp.exp(m2 - m1)
    den = 1.0 + e2
    o_ref[...] = jnp.where(lane == i1, 1.0 / den, 0.0) + jnp.where(lane == i2, e2 / den, 0.0)


def _router_call(x, mod, g, w_router, b_router, mod_row, tm):
    t, d = x.shape
    ne = w_router.shape[1]
    lanes = 128
    w = jnp.zeros((d, lanes), F32).at[:, :ne].set(w_router)
    b = jnp.zeros((1, lanes), F32).at[0, :ne].set(b_router)
    return pl.pallas_call(
        functools.partial(_router_kernel, n_experts=ne),
        grid=(t // tm,),
        in_specs=[pl.BlockSpec((tm, d), lambda i: (i, 0)),
                  pl.BlockSpec((None, 6, d), lambda i: (mod_row(i * tm), 0, 0)),
                  pl.BlockSpec((1, d), lambda i: (0, 0)),
                  pl.BlockSpec((d, lanes), lambda i: (0, 0)),
                  pl.BlockSpec((1, lanes), lambda i: (0, 0))],
        out_specs=pl.BlockSpec((tm, lanes), lambda i: (i, 0)),
        out_shape=jax.ShapeDtypeStruct((t, lanes), F32),
        compiler_params=_cparams("parallel"),
        name="router",
    )(x, mod, g, w, b)


def kernel(x_prompt, x_sample, cache_k, cache_v, c, c_ctx, w_ada, b_ada, norm_g, w_in, b_in, pool_w, pool_scale, rpb, conv_w, conv_b, conv_ln_g, conv_ln_b, w_branch, w_out, w_gate_d, w_up_d, w_down_d, w_router, b_router, w_gate_e, w_up_e, w_down_e):
    nbp, seq, d = x_prompt.shape
    nbs, dec_seq, _ = x_sample.shape
    depth = w_ada.shape[0]
    bw = d // N_BRANCH
    hd = bw // NA_HEADS
    ns_rows = nbs * dec_seq
    np_rows = nbp * seq
    assert dec_seq & (dec_seq - 1) == 0 and seq & (seq - 1) == 0
    assert nbs + 1 <= MOD_ROWS

    tm = 512
    tile_seq = min(256, seq)
    assert ns_rows % tm == 0 and np_rows % tm == 0 and dec_seq % tm == 0 and seq % tile_seq == 0

    def mod_row(row0):
        return jnp.where(row0 < ns_rows, row0 // dec_seq, nbs)

    x = jnp.concatenate([x_sample.reshape(ns_rows, d), x_prompt.reshape(np_rows, d)], axis=0)
    cvec = jnp.zeros((MOD_ROWS, d), F32).at[:nbs].set(c).at[nbs].set(c_ctx)
    mods = _ada_call(cvec, w_ada, b_ada).reshape(depth, MOD_ROWS, 6, d)
    cc, sc = _channel_dft_mats(bw, FNET_GROUPS)

    pool_col, conv_col, gate_col0 = 1, 5, 7 * bw
    q_col = 2 * bw // hd

    new_k, new_v = [], []
    for l in range(depth):
        mod = mods[l]
        g = [norm_g[l, n][None, :] for n in range(4)]
        proj = _in_call(x, mod, g[0], w_in[l].astype(BF16), b_in[l][None, :], mod_row, tm, 512)

        o_f = jnp.concatenate([_fourier_call(proj, 0, nbs, dec_seq, bw, cc, sc),
                               _fourier_call(proj, ns_rows, nbp, seq, bw, cc, sc)], axis=0)
        o_p = _pool_call(proj, pool_w[l].astype(BF16), pool_scale[l][None, :], pool_col, bw,
                         tile_seq, ns_rows, dec_seq, seq)
        o_c = _conv_call(proj, conv_w[l], conv_b[l][None, :], conv_ln_g[l][None, :],
                         conv_ln_b[l][None, :], conv_col, bw, tile_seq, ns_rows, dec_seq, seq)
        o_a_lat = _na_attn_call(proj, cache_k, cache_v, l, rpb[l], nbs, dec_seq, q_col, hd)
        o_a_ctx, k_ctx, v_ctx = _ctx_attn_call(proj, ns_rows, nbp, seq, q_col, hd)
        new_k.append(k_ctx)
        new_v.append(v_ctx)
        o_a = jnp.concatenate([o_a_lat, o_a_ctx], axis=0)

        merged = _merge_call([o_f, o_p, o_a, o_c], proj, gate_col0, w_branch[l].astype(BF16), tm, 512)
        x = _out_call(merged, w_out[l].astype(BF16), x, mod, g[1], 2, mod_row, tm, 512)

        i = l // 2
        if l % 2 == 1:
            gates = _router_call(x, mod, g[2], w_router[i], b_router[i], mod_row, tm)
            hmid = _ffn_up_call(x, mod, g[2], w_gate_e[i].astype(BF16), w_up_e[i].astype(BF16),
                                gates, mod_row, tm, 512)
            w_down = w_down_e[i].reshape(-1, d).astype(BF16)
        else:
            hmid = _ffn_up_call(x, mod, g[2], w_gate_d[i][None].astype(BF16),
                                w_up_d[i][None].astype(BF16), None, mod_row, tm, 512)
            w_down = w_down_d[i].astype(BF16)
        x = _out_call(hmid, w_down, x, mod, g[3], 5, mod_row, tm, 512)

    y_sample = x[:ns_rows].reshape(nbs, dec_seq, d)
    y_prompt = x[ns_rows:].reshape(nbp, seq, d)
    return y_prompt, y_sample, jnp.stack(new_k, axis=1), jnp.stack(new_v, axis=1)
```

```python
import functools

import numpy as np
import jax
import jax.numpy as jnp
from jax import lax
from jax.experimental import pallas as pl
from jax.experimental.pallas import tpu as pltpu

F32 = jnp.float32
BF16 = jnp.bfloat16

N_BRANCH = 4
FNET_GROUPS = 4
POOL_WINDOWS = (2, 4, 8, 16)
NA_HEADS = 4
NA_WIN_ROWS = 8
NA_WIN_COLS = 16
GRID_W = 64
CONV_K = 31
TOP_K = 2
RMS_EPS = 1e-6
LN_EPS = 1e-5
MASK_VALUE = -1e30
MOD_ROWS = 8
POOL_HALO = 8
CONV_HALO = 16
NA_Q_ROWS = 8
NA_K_ROWS = 16
VMEM_LIMIT_BYTES = 48 * 1024 * 1024


def _cparams(*sem):
    return pltpu.CompilerParams(dimension_semantics=sem, vmem_limit_bytes=VMEM_LIMIT_BYTES)


def _silu(x):
    return x * jax.nn.sigmoid(x)


def _rms(x, g):
    return x * lax.rsqrt(jnp.mean(x * x, axis=-1, keepdims=True) + RMS_EPS) * g


def _nt_dot(a, b):
    return lax.dot_general(a, b, (((1,), (1,)), ((), ())), preferred_element_type=F32)


def _ada_kernel(c_ref, w_ref, b_ref, o_ref):
    s = _silu(c_ref[...]).astype(BF16)
    o_ref[...] = jnp.dot(s, w_ref[...].astype(BF16), preferred_element_type=F32) + b_ref[...]


def _ada_call(cvec, w_ada, b_ada):
    depth, d, n = w_ada.shape
    tn = 512
    return pl.pallas_call(
        _ada_kernel,
        grid=(depth, n // tn),
        in_specs=[pl.BlockSpec((MOD_ROWS, d), lambda l, j: (0, 0)),
                  pl.BlockSpec((None, d, tn), lambda l, j: (l, 0, j)),
                  pl.BlockSpec((None, 1, tn), lambda l, j: (l, 0, j))],
        out_specs=pl.BlockSpec((None, MOD_ROWS, tn), lambda l, j: (l, 0, j)),
        out_shape=jax.ShapeDtypeStruct((depth, MOD_ROWS, n), F32),
        compiler_params=_cparams("parallel", "parallel"),
        name="adaln",
    )(cvec, w_ada, b_ada.reshape(depth, 1, n))


def _in_kernel(x_ref, mod_ref, g_ref, w_ref, b_ref, o_ref, h_ref):
    @pl.when(pl.program_id(1) == 0)
    def _():
        h = _rms(x_ref[...], g_ref[...]) * (1.0 + mod_ref[1:2, :]) + mod_ref[0:1, :]
        h_ref[...] = h.astype(BF16)

    o_ref[...] = jnp.dot(h_ref[...], w_ref[...], preferred_element_type=F32) + b_ref[...]


def _in_call(x, mod, g, w, b, mod_row, tm, tn):
    t, d = x.shape
    n = w.shape[1]
    return pl.pallas_call(
        _in_kernel,
        grid=(t // tm, n // tn),
        in_specs=[pl.BlockSpec((tm, d), lambda i, j: (i, 0)),
                  pl.BlockSpec((None, 6, d), lambda i, j: (mod_row(i * tm), 0, 0)),
                  pl.BlockSpec((1, d), lambda i, j: (0, 0)),
                  pl.BlockSpec((d, tn), lambda i, j: (0, j)),
                  pl.BlockSpec((1, tn), lambda i, j: (0, j))],
        out_specs=pl.BlockSpec((tm, tn), lambda i, j: (i, j)),
        out_shape=jax.ShapeDtypeStruct((t, n), F32),
        scratch_shapes=[pltpu.VMEM((tm, d), BF16)],
        compiler_params=_cparams("parallel", "arbitrary"),
        name="in_proj",
    )(x, mod, g, w, b)


def _dft_mats(n):
    scale = 1.0 / np.sqrt(n)
    j = jnp.arange(n, dtype=jnp.int32)
    if n <= 1024:
        ang = ((j[:, None] * j[None, :]) % n).astype(F32) * (2.0 * np.pi / n)
        return (jnp.cos(ang) * scale).astype(BF16), (jnp.sin(ang) * scale).astype(BF16)
    base = 64
    hi = n // base
    k1 = jnp.arange(hi, dtype=jnp.int32)
    k0 = jnp.arange(base, dtype=jnp.int32)
    a = ((j[:, None] * k1[None, :]) % hi).astype(F32) * (2.0 * np.pi / hi)
    b = ((j[:, None] * k0[None, :]) % n).astype(F32) * (2.0 * np.pi / n)
    ca, sa = jnp.cos(a)[:, :, None], jnp.sin(a)[:, :, None]
    cb, sb = jnp.cos(b)[:, None, :] * scale, jnp.sin(b)[:, None, :] * scale
    c = (ca * cb - sa * sb).reshape(n, n)
    s = (sa * cb + ca * sb).reshape(n, n)
    return c.astype(BF16), s.astype(BF16)


def _channel_dft_mats(width, groups):
    gw = width // groups
    k = np.arange(gw)
    ang = 2.0 * np.pi * ((k[:, None] * k[None, :]) % gw) / gw
    c = np.zeros((width, width), np.float32)
    s = np.zeros((width, width), np.float32)
    for g in range(groups):
        sl = slice(g * gw, (g + 1) * gw)
        c[sl, sl] = np.cos(ang) / np.sqrt(gw)
        s[sl, sl] = np.sin(ang) / np.sqrt(gw)
    return jnp.asarray(c, BF16), jnp.asarray(s, BF16)


def _fnet1_kernel(u_ref, cc_ref, sc_ref, vc_ref, vs_ref):
    u = u_ref[...].astype(BF16)
    vc_ref[...] = jnp.dot(u, cc_ref[...], preferred_element_type=F32).astype(BF16)
    vs_ref[...] = jnp.dot(u, sc_ref[...], preferred_element_type=F32).astype(BF16)


def _fnet2_kernel(c_ref, s_ref, vc_ref, vs_ref, o_ref, acc_ref):
    k = pl.program_id(2)

    @pl.when(k == 0)
    def _():
        acc_ref[...] = jnp.zeros_like(acc_ref)

    acc_ref[...] += (jnp.dot(c_ref[...], vc_ref[...], preferred_element_type=F32)
                     - jnp.dot(s_ref[...], vs_ref[...], preferred_element_type=F32))

    @pl.when(k == pl.num_programs(2) - 1)
    def _():
        o_ref[...] = acc_ref[...].astype(o_ref.dtype)


def _fourier_call(proj, row_base, nb, seq, bw, cc, sc):
    c_l, s_l = _dft_mats(seq)
    tm1 = min(seq, 512)
    nt1 = seq // tm1
    base1 = row_base // tm1
    vc, vs = pl.pallas_call(
        _fnet1_kernel,
        grid=(nb, nt1),
        in_specs=[pl.BlockSpec((tm1, bw), lambda b, t: (base1 + b * nt1 + t, 0)),
                  pl.BlockSpec((bw, bw), lambda b, t: (0, 0)),
                  pl.BlockSpec((bw, bw), lambda b, t: (0, 0))],
        out_specs=[pl.BlockSpec((tm1, bw), lambda b, t: (t, b)),
                   pl.BlockSpec((tm1, bw), lambda b, t: (t, b))],
        out_shape=[jax.ShapeDtypeStruct((seq, nb * bw), BF16)] * 2,
        compiler_params=_cparams("parallel", "parallel"),
        name="fnet_channels",
    )(proj, cc, sc)
    tm2 = min(seq, 1024)
    tk = min(seq, 512)
    nt2 = seq // tm2
    return pl.pallas_call(
        _fnet2_kernel,
        grid=(nt2, nb, seq // tk),
        in_specs=[pl.BlockSpec((tm2, tk), lambda i, j, k: (i, k)),
                  pl.BlockSpec((tm2, tk), lambda i, j, k: (i, k)),
                  pl.BlockSpec((tk, bw), lambda i, j, k: (k, j)),
                  pl.BlockSpec((tk, bw), lambda i, j, k: (k, j))],
        out_specs=pl.BlockSpec((tm2, bw), lambda i, j, k: (j * nt2 + i, 0)),
        out_shape=jax.ShapeDtypeStruct((nb * seq, bw), BF16),
        scratch_shapes=[pltpu.VMEM((tm2, bw), F32)],
        compiler_params=_cparams("parallel", "parallel", "arbitrary"),
        name="fnet_positions",
    )(c_l, s_l, vc, vs)


def _seq_position(row0, ns_rows, dec_seq, seq):
    is_lat = row0 < ns_rows
    seq_len = jnp.where(is_lat, dec_seq, seq)
    pos0 = jnp.where(is_lat, row0 & (dec_seq - 1), row0 & (seq - 1))
    return seq_len, pos0


def _pool_kernel(prev_ref, cur_ref, nxt_ref, w_ref, sc_ref, o_ref, ext_ref, *,
                 tm, ns_rows, dec_seq, seq):
    seq_len, pos0 = _seq_position(pl.program_id(0) * tm, ns_rows, dec_seq, seq)
    h = POOL_HALO
    ext_ref[0:h, :] = jnp.where(pos0 == 0, 0.0, prev_ref[...])
    ext_ref[h:h + tm, :] = cur_ref[...]
    ext_ref[h + tm:2 * h + tm, :] = jnp.where(pos0 + tm == seq_len, 0.0, nxt_ref[...])
    t = pos0 + lax.broadcasted_iota(jnp.int32, (tm, 1), 0)
    gw = cur_ref.shape[1] // len(POOL_WINDOWS)
    for gi, win in enumerate(POOL_WINDOWS):
        lo = win // 2
        hi = win - lo
        cols = slice(gi * gw, (gi + 1) * gw)
        s = ext_ref[h - lo:h - lo + tm, cols]
        for j in range(1 - lo, hi):
            s = s + ext_ref[h + j:h + j + tm, cols]
        cnt = (jnp.minimum(t + hi, seq_len) - jnp.maximum(t - lo, 0)).astype(F32)
        dlt = s / cnt - cur_ref[:, cols]
        y = jnp.dot(dlt.astype(BF16), w_ref[gi], preferred_element_type=F32) * sc_ref[:, cols]
        o_ref[:, cols] = y.astype(o_ref.dtype)


def _pool_call(proj, pool_w, pool_scale, col_blk, bw, tm, ns_rows, dec_seq, seq):
    t = proj.shape[0]
    h = POOL_HALO
    r = tm // h
    last = t // h - 1
    gw = bw // len(POOL_WINDOWS)
    return pl.pallas_call(
        functools.partial(_pool_kernel, tm=tm, ns_rows=ns_rows, dec_seq=dec_seq, seq=seq),
        grid=(t // tm,),
        in_specs=[pl.BlockSpec((h, bw), lambda i: (jnp.maximum(i * r - 1, 0), col_blk)),
                  pl.BlockSpec((tm, bw), lambda i: (i, col_blk)),
                  pl.BlockSpec((h, bw), lambda i: (jnp.minimum((i + 1) * r, last), col_blk)),
                  pl.BlockSpec((len(POOL_WINDOWS), gw, gw), lambda i: (0, 0, 0)),
                  pl.BlockSpec((1, bw), lambda i: (0, 0))],
        out_specs=pl.BlockSpec((tm, bw), lambda i: (i, 0)),
        out_shape=jax.ShapeDtypeStruct((t, bw), BF16),
        scratch_shapes=[pltpu.VMEM((tm + 2 * h, bw), F32)],
        compiler_params=_cparams("parallel"),
        name="pool_mix",
    )(proj, proj, proj, pool_w, pool_scale)


def _conv_kernel(ap_ref, ac_ref, an_ref, gp_ref, gc_ref, gn_ref, w_ref, b_ref, lg_ref, lb_ref,
                 o_ref, ext_ref, y_ref, *, tm, ns_rows, dec_seq, seq):
    seq_len, pos0 = _seq_position(pl.program_id(0) * tm, ns_rows, dec_seq, seq)
    h = CONV_HALO
    ext_ref[0:h, :] = jnp.where(pos0 == 0, 0.0, ap_ref[...] * jax.nn.sigmoid(gp_ref[...]))
    ext_ref[h:h + tm, :] = ac_ref[...] * jax.nn.sigmoid(gc_ref[...])
    ext_ref[h + tm:2 * h + tm, :] = jnp.where(pos0 + tm == seq_len, 0.0,
                                              an_ref[...] * jax.nn.sigmoid(gn_ref[...]))
    bw = ac_ref.shape[1]
    lanes = 128
    for c in range(bw // lanes):
        cols = slice(c * lanes, (c + 1) * lanes)
        acc = jnp.zeros((tm, lanes), F32)
        for k in range(CONV_K):
            off = h + k - CONV_K // 2
            acc = acc + ext_ref[off:off + tm, cols] * w_ref[k:k + 1, cols]
        y_ref[:, cols] = acc + b_ref[:, cols]
    y = y_ref[...]
    mu = jnp.mean(y, axis=-1, keepdims=True)
    var = jnp.mean(jnp.square(y - mu), axis=-1, keepdims=True)
    z = (y - mu) * lax.rsqrt(var + LN_EPS) * lg_ref[...] + lb_ref[...]
    o_ref[...] = _silu(z).astype(o_ref.dtype)


def _conv_call(proj, conv_w, conv_b, ln_g, ln_b, col_blk, bw, tm, ns_rows, dec_seq, seq):
    t = proj.shape[0]
    h = CONV_HALO
    r = tm // h
    last = t // h - 1
    prev = lambda i: jnp.maximum(i * r - 1, 0)
    nxt = lambda i: jnp.minimum((i + 1) * r, last)
    vec = pl.BlockSpec((1, bw), lambda i: (0, 0))
    return pl.pallas_call(
        functools.partial(_conv_kernel, tm=tm, ns_rows=ns_rows, dec_seq=dec_seq, seq=seq),
        grid=(t // tm,),
        in_specs=[pl.BlockSpec((h, bw), lambda i: (prev(i), col_blk)),
                  pl.BlockSpec((tm, bw), lambda i: (i, col_blk)),
                  pl.BlockSpec((h, bw), lambda i: (nxt(i), col_blk)),
                  pl.BlockSpec((h, bw), lambda i: (prev(i), col_blk + 1)),
                  pl.BlockSpec((tm, bw), lambda i: (i, col_blk + 1)),
                  pl.BlockSpec((h, bw), lambda i: (nxt(i), col_blk + 1)),
                  pl.BlockSpec((CONV_K, bw), lambda i: (0, 0)),
                  vec, vec, vec],
        out_specs=pl.BlockSpec((tm, bw), lambda i: (i, 0)),
        out_shape=jax.ShapeDtypeStruct((t, bw), BF16),
        scratch_shapes=[pltpu.VMEM((tm + 2 * h, bw), F32), pltpu.VMEM((tm, bw), F32)],
        compiler_params=_cparams("parallel"),
        name="conv_module",
    )(proj, proj, proj, proj, proj, proj, conv_w, conv_b, ln_g, ln_b)


def _ctx_attn_kernel(q_ref, k_ref, v_ref, o_ref, ko_ref, vo_ref, *, scale):
    k = k_ref[...]
    v = v_ref[...]
    s = _nt_dot(q_ref[...].astype(BF16), k.astype(BF16)) * scale
    p = jnp.exp(s - jnp.max(s, axis=-1, keepdims=True))
    l = jnp.sum(p, axis=-1, keepdims=True)
    o = jnp.dot(p.astype(BF16), v.astype(BF16), preferred_element_type=F32)
    o_ref[...] = (o / l).astype(o_ref.dtype)
    ko_ref[...] = k
    vo_ref[...] = v


def _ctx_attn_call(proj, row_base, nb, seq, q_col, hd):
    base = row_base // seq
    nh = NA_HEADS
    kv_spec = pl.BlockSpec((None, None, seq, hd), lambda b, h: (b, h, 0, 0))
    return pl.pallas_call(
        functools.partial(_ctx_attn_kernel, scale=hd ** -0.5),
        grid=(nb, nh),
        in_specs=[pl.BlockSpec((seq, hd), lambda b, h: (base + b, q_col + h)),
                  pl.BlockSpec((seq, hd), lambda b, h: (base + b, q_col + nh + h)),
                  pl.BlockSpec((seq, hd), lambda b, h: (base + b, q_col + 2 * nh + h))],
        out_specs=[pl.BlockSpec((seq, hd), lambda b, h: (b, h)), kv_spec, kv_spec],
        out_shape=[jax.ShapeDtypeStruct((nb * seq, nh * hd), BF16),
                   jax.ShapeDtypeStruct((nb, nh, seq, hd), F32),
                   jax.ShapeDtypeStruct((nb, nh, seq, hd), F32)],
        compiler_params=_cparams("parallel", "parallel"),
        name="context_attention",
    )(proj, proj, proj)


def _na_bias_tables(rpb, rows):
    nh = rpb.shape[0]
    w = GRID_W
    c = np.arange(w)
    cs = np.clip(c - NA_WIN_COLS // 2, 0, w - NA_WIN_COLS)
    col_ok = (c[None, :] >= cs[:, None]) & (c[None, :] < cs[:, None] + NA_WIN_COLS)
    rel_col = np.clip(c[None, :] - c[:, None] + NA_WIN_COLS - 1, 0, 2 * NA_WIN_COLS - 2)
    planes = jnp.where(col_ok[None, None], rpb[:, :, rel_col], MASK_VALUE)
    planes = jnp.concatenate([planes, jnp.full((nh, 1, w, w), MASK_VALUE, rpb.dtype)], axis=1)
    masked_plane = 2 * NA_WIN_ROWS - 1
    variants, var_ids = [], []
    for kb in range(rows // NA_Q_ROWS):
        r = kb * NA_Q_ROWS + np.arange(NA_Q_ROWS)
        rs = np.clip(r - NA_WIN_ROWS // 2, 0, rows - NA_WIN_ROWS)
        kr = _na_key_row0(kb, rows) + np.arange(NA_K_ROWS)
        ok = (kr[None, :] >= rs[:, None]) & (kr[None, :] < rs[:, None] + NA_WIN_ROWS)
        assert ok.sum() == NA_Q_ROWS * NA_WIN_ROWS
        plane = np.where(ok, kr[None, :] - r[:, None] + NA_WIN_ROWS - 1, masked_plane)
        for vi, known in enumerate(variants):
            if np.array_equal(known, plane):
                var_ids.append(vi)
                break
        else:
            var_ids.append(len(variants))
            variants.append(plane)
    idx = np.stack(variants)
    tbl = planes[:, idx]
    tbl = tbl.transpose(0, 1, 2, 4, 3, 5).reshape(nh, len(variants), NA_Q_ROWS * w, NA_K_ROWS * w)
    return tbl, jnp.asarray(np.array(var_ids, np.int32))


def _na_key_row0(kb, rows):
    lo = kb * NA_Q_ROWS - NA_WIN_ROWS // 2
    if isinstance(kb, (int, np.integer)):
        return int(np.clip(lo, 0, rows - NA_K_ROWS))
    return jnp.clip(lo, 0, rows - NA_K_ROWS)


def _na_attn_kernel(var_ref, q_ref, k_ref, v_ref, kc_ref, vc_ref, bias_ref, o_ref, *, rows, scale):
    del var_ref
    nk = NA_K_ROWS * GRID_W
    start = pl.multiple_of(_na_key_row0(pl.program_id(2), rows) * GRID_W, 256)
    kw = k_ref[pl.ds(start, nk), :].astype(BF16)
    vw = v_ref[pl.ds(start, nk), :].astype(BF16)
    q = q_ref[...].astype(BF16)
    s_loc = _nt_dot(q, kw) * scale + bias_ref[...]
    s_ctx = _nt_dot(q, kc_ref[...].astype(BF16)) * scale
    m = jnp.maximum(jnp.max(s_loc, axis=-1, keepdims=True), jnp.max(s_ctx, axis=-1, keepdims=True))
    p_loc = jnp.exp(s_loc - m)
    p_ctx = jnp.exp(s_ctx - m)
    l = jnp.sum(p_loc, axis=-1, keepdims=True) + jnp.sum(p_ctx, axis=-1, keepdims=True)
    o = (jnp.dot(p_loc.astype(BF16), vw, preferred_element_type=F32)
         + jnp.dot(p_ctx.astype(BF16), vc_ref[...].astype(BF16), preferred_element_type=F32))
    o_ref[...] = (o / l).astype(o_ref.dtype)


def _na_attn_call(proj, cache_k, cache_v, layer, rpb, nb, dec_seq, q_col, hd):
    rows = dec_seq // GRID_W
    assert rows >= NA_K_ROWS and rows % NA_Q_ROWS == 0
    nh = NA_HEADS
    nq = NA_Q_ROWS * GRID_W
    nkb = rows // NA_Q_ROWS
    past = cache_k.shape[3]
    tbl, var_ids = _na_bias_tables(rpb, rows)
    ctx_spec = pl.BlockSpec((None, None, None, past, hd), lambda b, h, k, var: (b, layer, h, 0, 0))
    grid_spec = pltpu.PrefetchScalarGridSpec(
        num_scalar_prefetch=1,
        grid=(nb, nh, nkb),
        in_specs=[pl.BlockSpec((nq, hd), lambda b, h, k, var: (b * nkb + k, q_col + h)),
                  pl.BlockSpec((dec_seq, hd), lambda b, h, k, var: (b, q_col + nh + h)),
                  pl.BlockSpec((dec_seq, hd), lambda b, h, k, var: (b, q_col + 2 * nh + h)),
                  ctx_spec, ctx_spec,
                  pl.BlockSpec((None, None, nq, NA_K_ROWS * GRID_W),
                               lambda b, h, k, var: (h, var[k], 0, 0))],
        out_specs=pl.BlockSpec((nq, hd), lambda b, h, k, var: (b * nkb + k, h)),
    )
    return pl.pallas_call(
        functools.partial(_na_attn_kernel, rows=rows, scale=hd ** -0.5),
        grid_spec=grid_spec,
        out_shape=jax.ShapeDtypeStruct((nb * dec_seq, nh * hd), BF16),
        compiler_params=_cparams("parallel", "parallel", "arbitrary"),
        name="neighbourhood_attention",
    )(var_ids, proj, proj, proj, cache_k, cache_v, tbl)


def _merge_kernel(f_ref, p_ref, a_ref, c_ref, g0_ref, g1_ref, g2_ref, g3_ref, w_ref, o_ref):
    acc = None
    for n, (br, gt) in enumerate(((f_ref, g0_ref), (p_ref, g1_ref), (a_ref, g2_ref), (c_ref, g3_ref))):
        y = jax.nn.sigmoid(gt[...]) * jnp.dot(br[...], w_ref[n], preferred_element_type=F32)
        acc = y if acc is None else acc + y
    o_ref[...] = acc.astype(o_ref.dtype)


def _merge_call(branches, proj, gate_col0, w_branch, tm, tn):
    t, bw = branches[0].shape
    d = w_branch.shape[2]
    nj = d // tn
    c0 = gate_col0 // tn
    br_spec = pl.BlockSpec((tm, bw), lambda i, j: (i, 0))
    gate_specs = [pl.BlockSpec((tm, tn), functools.partial(lambda i, j, n: (i, c0 + n * nj + j), n=n))
                  for n in range(N_BRANCH)]
    return pl.pallas_call(
        _merge_kernel,
        grid=(t // tm, nj),
        in_specs=[br_spec] * N_BRANCH + gate_specs
                 + [pl.BlockSpec((N_BRANCH, bw, tn), lambda i, j: (0, 0, j))],
        out_specs=pl.BlockSpec((tm, tn), lambda i, j: (i, j)),
        out_shape=jax.ShapeDtypeStruct((t, d), BF16),
        compiler_params=_cparams("parallel", "parallel"),
        name="branch_merge",
    )(*branches, proj, proj, proj, proj, w_branch)


def _out_kernel(a_ref, w_ref, x_ref, mod_ref, g_ref, o_ref, acc_ref, *, gate_row):
    k = pl.program_id(1)

    @pl.when(k == 0)
    def _():
        acc_ref[...] = jnp.zeros_like(acc_ref)

    acc_ref[...] += jnp.dot(a_ref[...], w_ref[...], preferred_element_type=F32)

    @pl.when(k == pl.num_programs(1) - 1)
    def _():
        o_ref[...] = x_ref[...] + mod_ref[gate_row:gate_row + 1, :] * _rms(acc_ref[...], g_ref[...])


def _out_call(a, w, x, mod, g, gate_row, mod_row, tm, tk):
    t, d = x.shape
    kdim = a.shape[1]
    return pl.pallas_call(
        functools.partial(_out_kernel, gate_row=gate_row),
        grid=(t // tm, kdim // tk),
        in_specs=[pl.BlockSpec((tm, tk), lambda i, k: (i, k)),
                  pl.BlockSpec((tk, d), lambda i, k: (k, 0)),
                  pl.BlockSpec((tm, d), lambda i, k: (i, 0)),
                  pl.BlockSpec((None, 6, d), lambda i, k: (mod_row(i * tm), 0, 0)),
                  pl.BlockSpec((1, d), lambda i, k: (0, 0))],
        out_specs=pl.BlockSpec((tm, d), lambda i, k: (i, 0)),
        out_shape=jax.ShapeDtypeStruct((t, d), F32),
        scratch_shapes=[pltpu.VMEM((tm, d), F32)],
        compiler_params=_cparams("parallel", "arbitrary"),
        name="proj_norm_residual",
    )(a, w, x, mod, g)


def _ffn_up_kernel(x_ref, mod_ref, g_ref, wg_ref, wu_ref, o_ref, h_ref):
    @pl.when(pl.program_id(1) == 0)
    def _():
        h = _rms(x_ref[...], g_ref[...]) * (1.0 + mod_ref[4:5, :]) + mod_ref[3:4, :]
        h_ref[...] = h.astype(BF16)

    h = h_ref[...]
    y = _silu(jnp.dot(h, wg_ref[...], preferred_element_type=F32)) * jnp.dot(
        h, wu_ref[...], preferred_element_type=F32)
    o_ref[...] = y.astype(o_ref.dtype)


def _ffn_up_call(x, mod, g, wg, wu, mod_row, tm, tn):
    t, d = x.shape
    f = wg.shape[1]
    w_spec = pl.BlockSpec((d, tn), lambda i, j: (0, j))
    return pl.pallas_call(
        _ffn_up_kernel,
        grid=(t // tm, f // tn),
        in_specs=[pl.BlockSpec((tm, d), lambda i, j: (i, 0)),
                  pl.BlockSpec((None, 6, d), lambda i, j: (mod_row(i * tm), 0, 0)),
                  pl.BlockSpec((1, d), lambda i, j: (0, 0)),
                  w_spec, w_spec],
        out_specs=pl.BlockSpec((tm, tn), lambda i, j: (i, j)),
        out_shape=jax.ShapeDtypeStruct((t, f), BF16),
        scratch_shapes=[pltpu.VMEM((tm, d), BF16)],
        compiler_params=_cparams("parallel", "arbitrary"),
        name="ffn_up",
    )(x, mod, g, wg, wu)


def _split_bf16(x):
    hi = x.astype(BF16)
    return hi, (x - hi.astype(F32)).astype(BF16)


def _pack_bf16_pairs(h):
    n = h.shape[1] // 2
    hb = h.astype(BF16).astype(F32)
    hi = pltpu.bitcast(hb[:, :n], jnp.int32)
    lo = pltpu.bitcast(hb[:, n:], jnp.int32)
    return hi | lax.shift_right_logical(lo, 16)


def _unpack_bf16_pairs(w):
    hi = pltpu.bitcast(w & jnp.int32(-65536), F32).astype(BF16)
    lo = pltpu.bitcast(w << 16, F32).astype(BF16)
    return hi, lo


def _router_kernel(x_ref, mod_ref, g_ref, w_ref, b_ref, o_ref, hp_ref, *, n_experts):
    h = _rms(x_ref[...], g_ref[...]) * (1.0 + mod_ref[4:5, :]) + mod_ref[3:4, :]
    hp_ref[...] = _pack_bf16_pairs(h)
    h_hi, h_lo = _split_bf16(h)
    w_hi, w_lo = _split_bf16(w_ref[...])
    logits = (jnp.dot(h_hi, w_hi, preferred_element_type=F32)
              + jnp.dot(h_lo, w_hi, preferred_element_type=F32)
              + jnp.dot(h_hi, w_lo, preferred_element_type=F32)) + b_ref[...]
    lane = lax.broadcasted_iota(jnp.int32, logits.shape, 1).astype(F32)
    neg = -jnp.inf
    no_lane = float(logits.shape[1])
    logits = jnp.where(lane < n_experts, logits, neg)
    m1 = jnp.max(logits, axis=-1, keepdims=True)
    i1 = jnp.min(jnp.where(logits == m1, lane, no_lane), axis=-1, keepdims=True)
    rest = jnp.where(lane == i1, neg, logits)
    m2 = jnp.max(rest, axis=-1, keepdims=True)
    i2 = jnp.min(jnp.where(rest == m2, lane, no_lane), axis=-1, keepdims=True)
    e2 = jnp.exp(m2 - m1)
    den = 1.0 + e2
    o_ref[...] = (jnp.where(lane == 0.0, i1, 0.0) + jnp.where(lane == 1.0, i2, 0.0)
                  + jnp.where(lane == 2.0, 1.0 / den, 0.0) + jnp.where(lane == 3.0, e2 / den, 0.0))


def _router_call(x, mod, g, w_router, b_router, mod_row, tm):
    t, d = x.shape
    ne = w_router.shape[1]
    lanes = 128
    w = jnp.zeros((d, lanes), F32).at[:, :ne].set(w_router)
    b = jnp.zeros((1, lanes), F32).at[0, :ne].set(b_router)
    return pl.pallas_call(
        functools.partial(_router_kernel, n_experts=ne),
        grid=(t // tm,),
        in_specs=[pl.BlockSpec((tm, d), lambda i: (i, 0)),
                  pl.BlockSpec((None, 6, d), lambda i: (mod_row(i * tm), 0, 0)),
                  pl.BlockSpec((1, d), lambda i: (0, 0)),
                  pl.BlockSpec((d, lanes), lambda i: (0, 0)),
                  pl.BlockSpec((1, lanes), lambda i: (0, 0))],
        out_specs=[pl.BlockSpec((tm, lanes), lambda i: (i, 0)),
                   pl.BlockSpec((tm, d // 2), lambda i: (i, 0))],
        out_shape=[jax.ShapeDtypeStruct((t, lanes), F32),
                   jax.ShapeDtypeStruct((t, d // 2), jnp.int32)],
        compiler_params=_cparams("parallel"),
        name="router",
    )(x, mod, g, w, b)


def _routing_tables(route, n_experts, tm):
    t = route.shape[0]
    na = TOP_K * t
    p_rows = na + n_experts * tm
    nt = p_rows // tm
    e_flat = route[:, :TOP_K].astype(jnp.int32).reshape(na)
    w_flat = route[:, TOP_K:2 * TOP_K].reshape(na)
    onehot = (e_flat[:, None] == jnp.arange(n_experts, dtype=jnp.int32)[None, :]).astype(jnp.int32)
    csum = jnp.cumsum(onehot, axis=0)
    rank = jnp.sum((csum - onehot) * onehot, axis=1)
    counts = csum[-1]
    padded = ((counts + tm - 1) // tm) * tm
    ends = jnp.cumsum(padded)
    pos = (ends - padded)[e_flat] + rank
    src = jnp.zeros((p_rows,), jnp.int32).at[pos].set(jnp.arange(na, dtype=jnp.int32) // TOP_K)
    wrow = jnp.zeros((p_rows,), F32).at[pos].set(w_flat)
    tile_expert = jnp.minimum(
        jnp.searchsorted(ends, jnp.arange(nt, dtype=jnp.int32) * tm, side="right"), n_experts - 1)
    meta = jnp.concatenate([tile_expert.astype(jnp.int32), (ends[-1:] // tm).astype(jnp.int32)])
    return src, wrow[:, None], pos.astype(jnp.int32), meta


def _gather_rows_kernel(idx_ref, src_ref, o_ref, buf_ref, sem_ref, *, rows):
    i = pl.program_id(0)
    n = pl.num_programs(0)

    def issue(tile, slot):
        def body(r, carry):
            pltpu.make_async_copy(src_ref.at[pl.ds(idx_ref[tile * rows + r], 1)],
                                  buf_ref.at[slot, pl.ds(r, 1)], sem_ref.at[slot]).start()
            return carry
        lax.fori_loop(0, rows, body, 0)

    @pl.when(i == 0)
    def _():
        issue(0, 0)

    @pl.when(i + 1 < n)
    def _():
        issue(i + 1, (i + 1) % 2)

    slot = i % 2
    pltpu.make_async_copy(src_ref.at[pl.ds(0, rows)], buf_ref.at[slot], sem_ref.at[slot]).wait()
    o_ref[...] = buf_ref[slot]


def _gather_rows_call(src, idx, rows):
    p_rows = idx.shape[0]
    width = src.shape[1]
    grid_spec = pltpu.PrefetchScalarGridSpec(
        num_scalar_prefetch=1,
        grid=(p_rows // rows,),
        in_specs=[pl.BlockSpec(memory_space=pl.ANY)],
        out_specs=pl.BlockSpec((rows, width), lambda i, idx: (i, 0)),
        scratch_shapes=[pltpu.VMEM((2, rows, width), src.dtype), pltpu.SemaphoreType.DMA((2,))],
    )
    return pl.pallas_call(
        functools.partial(_gather_rows_kernel, rows=rows),
        grid_spec=grid_spec,
        out_shape=jax.ShapeDtypeStruct((p_rows, width), src.dtype),
        compiler_params=_cparams("arbitrary"),
        name="moe_gather",
    )(idx, src)


def _moe_up_kernel(meta_ref, xs_ref, wrow_ref, wg_ref, wu_ref, o_ref, h_ref, *, n_tiles):
    i = pl.program_id(0)
    used = i < meta_ref[n_tiles]

    @pl.when(jnp.logical_and(used, pl.program_id(1) == 0))
    def _():
        half = xs_ref.shape[1]
        hi, lo = _unpack_bf16_pairs(xs_ref[...])
        h_ref[:, :half] = hi
        h_ref[:, half:] = lo

    @pl.when(used)
    def _():
        h = h_ref[...]
        y = _silu(jnp.dot(h, wg_ref[...], preferred_element_type=F32)) * jnp.dot(
            h, wu_ref[...], preferred_element_type=F32)
        o_ref[...] = (y * wrow_ref[...]).astype(o_ref.dtype)

    @pl.when(jnp.logical_not(used))
    def _():
        o_ref[...] = jnp.zeros_like(o_ref)


def _moe_up_call(xs, wrow, meta, wg, wu, tm, tn):
    p_rows, half = xs.shape
    _, d, f = wg.shape
    nt = p_rows // tm
    w_spec = pl.BlockSpec((None, d, tn), lambda i, j, meta: (meta[i], 0, j))
    grid_spec = pltpu.PrefetchScalarGridSpec(
        num_scalar_prefetch=1,
        grid=(nt, f // tn),
        in_specs=[pl.BlockSpec((tm, half), lambda i, j, meta: (i, 0)),
                  pl.BlockSpec((tm, 1), lambda i, j, meta: (i, 0)),
                  w_spec, w_spec],
        out_specs=pl.BlockSpec((tm, tn), lambda i, j, meta: (i, j)),
        scratch_shapes=[pltpu.VMEM((tm, d), BF16)],
    )
    return pl.pallas_call(
        functools.partial(_moe_up_kernel, n_tiles=nt),
        grid_spec=grid_spec,
        out_shape=jax.ShapeDtypeStruct((p_rows, f), BF16),
        compiler_params=_cparams("parallel", "arbitrary"),
        name="moe_up",
    )(meta, xs, wrow, wg, wu)


def _moe_down_kernel(meta_ref, a_ref, w_ref, o_ref, acc_ref, *, n_tiles):
    i = pl.program_id(0)
    k = pl.program_id(1)
    used = i < meta_ref[n_tiles]

    @pl.when(k == 0)
    def _():
        acc_ref[...] = jnp.zeros_like(acc_ref)

    @pl.when(used)
    def _():
        acc_ref[...] += jnp.dot(a_ref[...], w_ref[...], preferred_element_type=F32)

    @pl.when(k == pl.num_programs(1) - 1)
    def _():
        o_ref[...] = acc_ref[...]


def _moe_down_call(a, meta, wd, tm, tk):
    p_rows, f = a.shape
    d = wd.shape[2]
    nt = p_rows // tm
    grid_spec = pltpu.PrefetchScalarGridSpec(
        num_scalar_prefetch=1,
        grid=(nt, f // tk),
        in_specs=[pl.BlockSpec((tm, tk), lambda i, k, meta: (i, k)),
                  pl.BlockSpec((None, tk, d), lambda i, k, meta: (meta[i], k, 0))],
        out_specs=pl.BlockSpec((tm, d), lambda i, k, meta: (i, 0)),
        scratch_shapes=[pltpu.VMEM((tm, d), F32)],
    )
    return pl.pallas_call(
        functools.partial(_moe_down_kernel, n_tiles=nt),
        grid_spec=grid_spec,
        out_shape=jax.ShapeDtypeStruct((p_rows, d), F32),
        compiler_params=_cparams("parallel", "arbitrary"),
        name="moe_down",
    )(meta, a, wd)


def _moe_combine_kernel(pos_ref, ys_ref, x_ref, mod_ref, g_ref, o_ref, buf_ref, sem_ref, *, rows):
    i = pl.program_id(0)
    n = pl.num_programs(0)

    def issue(tile, slot):
        def body(r, carry):
            for s in range(TOP_K):
                pltpu.make_async_copy(ys_ref.at[pl.ds(pos_ref[TOP_K * (tile * rows + r) + s], 1)],
                                      buf_ref.at[slot, s, pl.ds(r, 1)], sem_ref.at[slot]).start()
            return carry
        lax.fori_loop(0, rows, body, 0)

    @pl.when(i == 0)
    def _():
        issue(0, 0)

    @pl.when(i + 1 < n)
    def _():
        issue(i + 1, (i + 1) % 2)

    slot = i % 2
    for s in range(TOP_K):
        pltpu.make_async_copy(ys_ref.at[pl.ds(0, rows)], buf_ref.at[slot, s], sem_ref.at[slot]).wait()
    y = buf_ref[slot, 0]
    for s in range(1, TOP_K):
        y = y + buf_ref[slot, s]
    o_ref[...] = x_ref[...] + mod_ref[5:6, :] * _rms(y, g_ref[...])


def _moe_combine_call(ys, pos, x, mod, g, mod_row, rows):
    t, d = x.shape
    grid_spec = pltpu.PrefetchScalarGridSpec(
        num_scalar_prefetch=1,
        grid=(t // rows,),
        in_specs=[pl.BlockSpec(memory_space=pl.ANY),
                  pl.BlockSpec((rows, d), lambda i, pos: (i, 0)),
                  pl.BlockSpec((None, 6, d), lambda i, pos: (mod_row(i * rows), 0, 0)),
                  pl.BlockSpec((1, d), lambda i, pos: (0, 0))],
        out_specs=pl.BlockSpec((rows, d), lambda i, pos: (i, 0)),
        scratch_shapes=[pltpu.VMEM((2, TOP_K, rows, d), F32), pltpu.SemaphoreType.DMA((2,))],
    )
    return pl.pallas_call(
        functools.partial(_moe_combine_kernel, rows=rows),
        grid_spec=grid_spec,
        out_shape=jax.ShapeDtypeStruct((t, d), F32),
        compiler_params=_cparams("arbitrary"),
        name="moe_combine",
    )(pos, ys, x, mod, g)


def kernel(x_prompt, x_sample, cache_k, cache_v, c, c_ctx, w_ada, b_ada, norm_g, w_in, b_in, pool_w, pool_scale, rpb, conv_w, conv_b, conv_ln_g, conv_ln_b, w_branch, w_out, w_gate_d, w_up_d, w_down_d, w_router, b_router, w_gate_e, w_up_e, w_down_e):
    nbp, seq, d = x_prompt.shape
    nbs, dec_seq, _ = x_sample.shape
    depth = w_ada.shape[0]
    bw = d // N_BRANCH
    hd = bw // NA_HEADS
    ns_rows = nbs * dec_seq
    np_rows = nbp * seq
    assert dec_seq & (dec_seq - 1) == 0 and seq & (seq - 1) == 0
    assert nbs + 1 <= MOD_ROWS

    tm = 512
    tile_seq = min(256, seq)
    assert ns_rows % tm == 0 and np_rows % tm == 0 and dec_seq % tm == 0 and seq % tile_seq == 0

    def mod_row(row0):
        return jnp.where(row0 < ns_rows, row0 // dec_seq, nbs)

    x = jnp.concatenate([x_sample.reshape(ns_rows, d), x_prompt.reshape(np_rows, d)], axis=0)
    cvec = jnp.zeros((MOD_ROWS, d), F32).at[:nbs].set(c).at[nbs].set(c_ctx)
    mods = _ada_call(cvec, w_ada, b_ada).reshape(depth, MOD_ROWS, 6, d)
    cc, sc = _channel_dft_mats(bw, FNET_GROUPS)

    pool_col, conv_col, gate_col0 = 1, 5, 7 * bw
    q_col = 2 * bw // hd

    new_k, new_v = [], []
    for l in range(depth):
        mod = mods[l]
        g = [norm_g[l, n][None, :] for n in range(4)]
        proj = _in_call(x, mod, g[0], w_in[l].astype(BF16), b_in[l][None, :], mod_row, tm, 512)

        o_f = jnp.concatenate([_fourier_call(proj, 0, nbs, dec_seq, bw, cc, sc),
                               _fourier_call(proj, ns_rows, nbp, seq, bw, cc, sc)], axis=0)
        o_p = _pool_call(proj, pool_w[l].astype(BF16), pool_scale[l][None, :], pool_col, bw,
                         tile_seq, ns_rows, dec_seq, seq)
        o_c = _conv_call(proj, conv_w[l], conv_b[l][None, :], conv_ln_g[l][None, :],
                         conv_ln_b[l][None, :], conv_col, bw, tile_seq, ns_rows, dec_seq, seq)
        o_a_lat = _na_attn_call(proj, cache_k, cache_v, l, rpb[l], nbs, dec_seq, q_col, hd)
        o_a_ctx, k_ctx, v_ctx = _ctx_attn_call(proj, ns_rows, nbp, seq, q_col, hd)
        new_k.append(k_ctx)
        new_v.append(v_ctx)
        o_a = jnp.concatenate([o_a_lat, o_a_ctx], axis=0)

        merged = _merge_call([o_f, o_p, o_a, o_c], proj, gate_col0, w_branch[l].astype(BF16), tm, 512)
        x = _out_call(merged, w_out[l].astype(BF16), x, mod, g[1], 2, mod_row, tm, 512)

        i = l // 2
        if l % 2 == 1:
            ne = w_router.shape[2]
            route, h_packed = _router_call(x, mod, g[2], w_router[i], b_router[i], mod_row, tm)
            src, wrow, pos, meta = _routing_tables(route, ne, tm)
            xs = _gather_rows_call(h_packed, src, 256)
            hmid = _moe_up_call(xs, wrow, meta, w_gate_e[i].astype(BF16), w_up_e[i].astype(BF16), tm, 512)
            ys = _moe_down_call(hmid, meta, w_down_e[i].astype(BF16), tm, 512)
            x = _moe_combine_call(ys, pos, x, mod, g[3], mod_row, 256)
        else:
            hmid = _ffn_up_call(x, mod, g[2], w_gate_d[i].astype(BF16), w_up_d[i].astype(BF16),
                                mod_row, tm, 512)
            x = _out_call(hmid, w_down_d[i].astype(BF16), x, mod, g[3], 5, mod_row, tm, 512)

    y_sample = x[:ns_rows].reshape(nbs, dec_seq, d)
    y_prompt = x[ns_rows:].reshape(nbp, seq, d)
    return y_prompt, y_sample, jnp.stack(new_k, axis=1), jnp.stack(new_v, axis=1)
```

```python
import functools

import numpy as np
import jax
import jax.numpy as jnp
from jax import lax
from jax.experimental import pallas as pl
from jax.experimental.pallas import tpu as pltpu

F32 = jnp.float32
BF16 = jnp.bfloat16

N_BRANCH = 4
FNET_GROUPS = 4
POOL_WINDOWS = (2, 4, 8, 16)
NA_HEADS = 4
NA_WIN_ROWS = 8
NA_WIN_COLS = 16
GRID_W = 64
CONV_K = 31
TOP_K = 2
RMS_EPS = 1e-6
LN_EPS = 1e-5
MASK_VALUE = -1e30
MOD_ROWS = 8
POOL_HALO = 8
CONV_HALO = 16
NA_Q_ROWS = 8
NA_K_ROWS = 16
VMEM_LIMIT_MIB = 48
VMEM_LIMIT_WIDE_MIB = 56

SHIFT1, SCALE1, GATE1, SHIFT2, SCALE2, GATE2 = range(6)


def _cparams(*sem, vmem_mib=VMEM_LIMIT_MIB):
    return pltpu.CompilerParams(dimension_semantics=sem, vmem_limit_bytes=vmem_mib * 1024 * 1024)


def _silu(x):
    return x * jax.nn.sigmoid(x)


def _rms(x, g):
    return x * lax.rsqrt(jnp.mean(x * x, axis=-1, keepdims=True) + RMS_EPS) * g


def _mod_norm(x, g, mod_ref, scale_row, shift_row):
    return (_rms(x, g) * (1.0 + mod_ref[scale_row:scale_row + 1, :])
            + mod_ref[shift_row:shift_row + 1, :])


def _nt_dot(a, b):
    return lax.dot_general(a, b, (((1,), (1,)), ((), ())), preferred_element_type=F32)


def _ada_kernel(c_ref, w_ref, b_ref, o_ref):
    s = _silu(c_ref[...]).astype(BF16)
    o_ref[...] = jnp.dot(s, w_ref[...].astype(BF16), preferred_element_type=F32) + b_ref[...]


def _ada_call(cvec, w_ada, b_ada):
    depth, d, n = w_ada.shape
    tn = 512
    return pl.pallas_call(
        _ada_kernel,
        grid=(depth, n // tn),
        in_specs=[pl.BlockSpec((MOD_ROWS, d), lambda l, j: (0, 0)),
                  pl.BlockSpec((None, d, tn), lambda l, j: (l, 0, j)),
                  pl.BlockSpec((None, 1, tn), lambda l, j: (l, 0, j))],
        out_specs=pl.BlockSpec((None, MOD_ROWS, tn), lambda l, j: (l, 0, j)),
        out_shape=jax.ShapeDtypeStruct((depth, MOD_ROWS, n), F32),
        compiler_params=_cparams("parallel", "parallel"),
        name="adaln",
    )(cvec, w_ada, b_ada.reshape(depth, 1, n))


def _norm_kernel(x_ref, mod_ref, g_ref, o_ref, *, scale_row, shift_row):
    o_ref[...] = _mod_norm(x_ref[...], g_ref[...], mod_ref, scale_row, shift_row).astype(o_ref.dtype)


def _norm_call(x, mod, g, scale_row, shift_row, mod_row, tm):
    t, d = x.shape
    return pl.pallas_call(
        functools.partial(_norm_kernel, scale_row=scale_row, shift_row=shift_row),
        grid=(t // tm,),
        in_specs=[pl.BlockSpec((tm, d), lambda i: (i, 0)),
                  pl.BlockSpec((None, 6, d), lambda i: (mod_row(i * tm), 0, 0)),
                  pl.BlockSpec((1, d), lambda i: (0, 0))],
        out_specs=pl.BlockSpec((tm, d), lambda i: (i, 0)),
        out_shape=jax.ShapeDtypeStruct((t, d), BF16),
        compiler_params=_cparams("parallel"),
        name="mod_norm",
    )(x, mod, g)


def _in_kernel(h_ref, w_ref, b_ref, main_ref, gate_ref, *, n_main):
    j = pl.program_id(1)
    y = jnp.dot(h_ref[...], w_ref[...], preferred_element_type=F32) + b_ref[...]

    @pl.when(j < n_main)
    def _():
        main_ref[...] = y

    @pl.when(j >= n_main)
    def _():
        gate_ref[...] = jax.nn.sigmoid(y).astype(gate_ref.dtype)


def _in_call(h, w, b, main_cols, tm, tn):
    t, d = h.shape
    n = w.shape[1]
    n_main = main_cols // tn
    return pl.pallas_call(
        functools.partial(_in_kernel, n_main=n_main),
        grid=(t // tm, n // tn),
        in_specs=[pl.BlockSpec((tm, d), lambda i, j: (i, 0)),
                  pl.BlockSpec((d, tn), lambda i, j: (0, j)),
                  pl.BlockSpec((1, tn), lambda i, j: (0, j))],
        out_specs=[pl.BlockSpec((tm, tn), lambda i, j: (i, jnp.minimum(j, n_main - 1))),
                   pl.BlockSpec((tm, tn), lambda i, j: (i, jnp.maximum(j - n_main, 0)))],
        out_shape=[jax.ShapeDtypeStruct((t, main_cols), F32),
                   jax.ShapeDtypeStruct((t, n - main_cols), BF16)],
        compiler_params=_cparams("parallel", "arbitrary"),
        name="in_proj",
    )(h, w, b)


def _dft_mats(n):
    scale = 1.0 / np.sqrt(n)
    j = jnp.arange(n, dtype=jnp.int32)
    if n <= 1024:
        ang = ((j[:, None] * j[None, :]) % n).astype(F32) * (2.0 * np.pi / n)
        return (jnp.cos(ang) * scale).astype(BF16), (jnp.sin(ang) * scale).astype(BF16)
    base = 64
    hi = n // base
    k1 = jnp.arange(hi, dtype=jnp.int32)
    k0 = jnp.arange(base, dtype=jnp.int32)
    a = ((j[:, None] * k1[None, :]) % hi).astype(F32) * (2.0 * np.pi / hi)
    b = ((j[:, None] * k0[None, :]) % n).astype(F32) * (2.0 * np.pi / n)
    ca, sa = jnp.cos(a)[:, :, None], jnp.sin(a)[:, :, None]
    cb, sb = jnp.cos(b)[:, None, :] * scale, jnp.sin(b)[:, None, :] * scale
    c = (ca * cb - sa * sb).reshape(n, n)
    s = (sa * cb + ca * sb).reshape(n, n)
    return c.astype(BF16), s.astype(BF16)


def _channel_dft_mats(width, groups):
    gw = width // groups
    k = np.arange(gw)
    ang = 2.0 * np.pi * ((k[:, None] * k[None, :]) % gw) / gw
    c = np.zeros((width, width), np.float32)
    s = np.zeros((width, width), np.float32)
    for g in range(groups):
        sl = slice(g * gw, (g + 1) * gw)
        c[sl, sl] = np.cos(ang) / np.sqrt(gw)
        s[sl, sl] = np.sin(ang) / np.sqrt(gw)
    return jnp.asarray(c, BF16), jnp.asarray(s, BF16)


def _fnet1_kernel(u_ref, cc_ref, sc_ref, vc_ref, vs_ref):
    u = u_ref[...].astype(BF16)
    vc_ref[...] = jnp.dot(u, cc_ref[...], preferred_element_type=F32).astype(BF16)
    vs_ref[...] = jnp.dot(u, sc_ref[...], preferred_element_type=F32).astype(BF16)


def _fnet2_kernel(c_ref, s_ref, vc_ref, vs_ref, o_ref, acc_ref):
    k = pl.program_id(2)

    @pl.when(k == 0)
    def _():
        acc_ref[...] = jnp.zeros_like(acc_ref)

    acc_ref[...] += (jnp.dot(c_ref[...], vc_ref[...], preferred_element_type=F32)
                     - jnp.dot(s_ref[...], vs_ref[...], preferred_element_type=F32))

    @pl.when(k == pl.num_programs(2) - 1)
    def _():
        o_ref[...] = acc_ref[...].astype(o_ref.dtype)


def _fourier_call(proj, row_base, nb, seq, bw, cc, sc):
    c_l, s_l = _dft_mats(seq)
    tm1 = min(seq, 512)
    nt1 = seq // tm1
    base1 = row_base // tm1
    vc, vs = pl.pallas_call(
        _fnet1_kernel,
        grid=(nb, nt1),
        in_specs=[pl.BlockSpec((tm1, bw), lambda b, t: (base1 + b * nt1 + t, 0)),
                  pl.BlockSpec((bw, bw), lambda b, t: (0, 0)),
                  pl.BlockSpec((bw, bw), lambda b, t: (0, 0))],
        out_specs=[pl.BlockSpec((tm1, bw), lambda b, t: (t, b)),
                   pl.BlockSpec((tm1, bw), lambda b, t: (t, b))],
        out_shape=[jax.ShapeDtypeStruct((seq, nb * bw), BF16)] * 2,
        compiler_params=_cparams("parallel", "parallel"),
        name="fnet_channels",
    )(proj, cc, sc)
    tm2 = min(seq, 1024)
    tk = min(seq, 512)
    nt2 = seq // tm2
    return pl.pallas_call(
        _fnet2_kernel,
        grid=(nt2, nb, seq // tk),
        in_specs=[pl.BlockSpec((tm2, tk), lambda i, j, k: (i, k)),
                  pl.BlockSpec((tm2, tk), lambda i, j, k: (i, k)),
                  pl.BlockSpec((tk, bw), lambda i, j, k: (k, j)),
                  pl.BlockSpec((tk, bw), lambda i, j, k: (k, j))],
        out_specs=pl.BlockSpec((tm2, bw), lambda i, j, k: (j * nt2 + i, 0)),
        out_shape=jax.ShapeDtypeStruct((nb * seq, bw), BF16),
        scratch_shapes=[pltpu.VMEM((tm2, bw), F32)],
        compiler_params=_cparams("parallel", "parallel", "arbitrary"),
        name="fnet_positions",
    )(c_l, s_l, vc, vs)


def _seq_position(row0, ns_rows, dec_seq, seq):
    is_lat = row0 < ns_rows
    seq_len = jnp.where(is_lat, dec_seq, seq)
    pos0 = jnp.where(is_lat, row0 & (dec_seq - 1), row0 & (seq - 1))
    return seq_len, pos0


def _pool_kernel(prev_ref, cur_ref, nxt_ref, w_ref, sc_ref, o_ref, ext_ref, *,
                 tm, ns_rows, dec_seq, seq):
    seq_len, pos0 = _seq_position(pl.program_id(0) * tm, ns_rows, dec_seq, seq)
    h = POOL_HALO
    ext_ref[0:h, :] = jnp.where(pos0 == 0, 0.0, prev_ref[...])
    ext_ref[h:h + tm, :] = cur_ref[...]
    ext_ref[h + tm:2 * h + tm, :] = jnp.where(pos0 + tm == seq_len, 0.0, nxt_ref[...])
    t = pos0 + lax.broadcasted_iota(jnp.int32, (tm, 1), 0)
    gw = cur_ref.shape[1] // len(POOL_WINDOWS)
    for gi, win in enumerate(POOL_WINDOWS):
        lo = win // 2
        hi = win - lo
        cols = slice(gi * gw, (gi + 1) * gw)
        s = ext_ref[h - lo:h - lo + tm, cols]
        for j in range(1 - lo, hi):
            s = s + ext_ref[h + j:h + j + tm, cols]
        cnt = (jnp.minimum(t + hi, seq_len) - jnp.maximum(t - lo, 0)).astype(F32)
        dlt = s / cnt - cur_ref[:, cols]
        y = jnp.dot(dlt.astype(BF16), w_ref[gi], preferred_element_type=F32) * sc_ref[:, cols]
        o_ref[:, cols] = y.astype(o_ref.dtype)


def _pool_call(proj, pool_w, pool_scale, col_blk, bw, tm, ns_rows, dec_seq, seq):
    t = proj.shape[0]
    h = POOL_HALO
    r = tm // h
    last = t // h - 1
    gw = bw // len(POOL_WINDOWS)
    return pl.pallas_call(
        functools.partial(_pool_kernel, tm=tm, ns_rows=ns_rows, dec_seq=dec_seq, seq=seq),
        grid=(t // tm,),
        in_specs=[pl.BlockSpec((h, bw), lambda i: (jnp.maximum(i * r - 1, 0), col_blk)),
                  pl.BlockSpec((tm, bw), lambda i: (i, col_blk)),
                  pl.BlockSpec((h, bw), lambda i: (jnp.minimum((i + 1) * r, last), col_blk)),
                  pl.BlockSpec((len(POOL_WINDOWS), gw, gw), lambda i: (0, 0, 0)),
                  pl.BlockSpec((1, bw), lambda i: (0, 0))],
        out_specs=pl.BlockSpec((tm, bw), lambda i: (i, 0)),
        out_shape=jax.ShapeDtypeStruct((t, bw), BF16),
        scratch_shapes=[pltpu.VMEM((tm + 2 * h, bw), F32)],
        compiler_params=_cparams("parallel"),
        name="pool_mix",
    )(proj, proj, proj, pool_w, pool_scale)


def _conv_kernel(ap_ref, ac_ref, an_ref, gp_ref, gc_ref, gn_ref, w_ref, b_ref, lg_ref, lb_ref,
                 o_ref, ext_ref, y_ref, *, tm, ns_rows, dec_seq, seq):
    seq_len, pos0 = _seq_position(pl.program_id(0) * tm, ns_rows, dec_seq, seq)
    h = CONV_HALO
    ext_ref[0:h, :] = jnp.where(pos0 == 0, 0.0, ap_ref[...] * jax.nn.sigmoid(gp_ref[...]))
    ext_ref[h:h + tm, :] = ac_ref[...] * jax.nn.sigmoid(gc_ref[...])
    ext_ref[h + tm:2 * h + tm, :] = jnp.where(pos0 + tm == seq_len, 0.0,
                                              an_ref[...] * jax.nn.sigmoid(gn_ref[...]))
    bw = ac_ref.shape[1]
    lanes = 128
    for c in range(bw // lanes):
        cols = slice(c * lanes, (c + 1) * lanes)
        acc = jnp.zeros((tm, lanes), F32)
        for k in range(CONV_K):
            off = h + k - CONV_K // 2
            acc = acc + ext_ref[off:off + tm, cols] * w_ref[k:k + 1, cols]
        y_ref[:, cols] = acc + b_ref[:, cols]
    y = y_ref[...]
    mu = jnp.mean(y, axis=-1, keepdims=True)
    var = jnp.mean(jnp.square(y - mu), axis=-1, keepdims=True)
    z = (y - mu) * lax.rsqrt(var + LN_EPS) * lg_ref[...] + lb_ref[...]
    o_ref[...] = _silu(z).astype(o_ref.dtype)


def _conv_call(proj, conv_w, conv_b, ln_g, ln_b, col_blk, bw, tm, ns_rows, dec_seq, seq):
    t = proj.shape[0]
    h = CONV_HALO
    r = tm // h
    last = t // h - 1
    prev = lambda i: jnp.maximum(i * r - 1, 0)
    nxt = lambda i: jnp.minimum((i + 1) * r, last)
    vec = pl.BlockSpec((1, bw), lambda i: (0, 0))
    return pl.pallas_call(
        functools.partial(_conv_kernel, tm=tm, ns_rows=ns_rows, dec_seq=dec_seq, seq=seq),
        grid=(t // tm,),
        in_specs=[pl.BlockSpec((h, bw), lambda i: (prev(i), col_blk)),
                  pl.BlockSpec((tm, bw), lambda i: (i, col_blk)),
                  pl.BlockSpec((h, bw), lambda i: (nxt(i), col_blk)),
                  pl.BlockSpec((h, bw), lambda i: (prev(i), col_blk + 1)),
                  pl.BlockSpec((tm, bw), lambda i: (i, col_blk + 1)),
                  pl.BlockSpec((h, bw), lambda i: (nxt(i), col_blk + 1)),
                  pl.BlockSpec((CONV_K, bw), lambda i: (0, 0)),
                  vec, vec, vec],
        out_specs=pl.BlockSpec((tm, bw), lambda i: (i, 0)),
        out_shape=jax.ShapeDtypeStruct((t, bw), BF16),
        scratch_shapes=[pltpu.VMEM((tm + 2 * h, bw), F32), pltpu.VMEM((tm, bw), F32)],
        compiler_params=_cparams("parallel"),
        name="conv_module",
    )(proj, proj, proj, proj, proj, proj, conv_w, conv_b, ln_g, ln_b)


def _ctx_attn_kernel(q_ref, k_ref, v_ref, o_ref, ko_ref, vo_ref, *, scale):
    k = k_ref[...]
    v = v_ref[...]
    s = _nt_dot(q_ref[...].astype(BF16), k.astype(BF16)) * scale
    p = jnp.exp(s - jnp.max(s, axis=-1, keepdims=True))
    l = jnp.sum(p, axis=-1, keepdims=True)
    o = jnp.dot(p.astype(BF16), v.astype(BF16), preferred_element_type=F32)
    o_ref[...] = (o / l).astype(o_ref.dtype)
    ko_ref[...] = k
    vo_ref[...] = v


def _ctx_attn_call(proj, row_base, nb, seq, q_col, hd):
    base = row_base // seq
    nh = NA_HEADS
    kv_spec = pl.BlockSpec((None, None, seq, hd), lambda b, h: (b, h, 0, 0))
    return pl.pallas_call(
        functools.partial(_ctx_attn_kernel, scale=hd ** -0.5),
        grid=(nb, nh),
        in_specs=[pl.BlockSpec((seq, hd), lambda b, h: (base + b, q_col + h)),
                  pl.BlockSpec((seq, hd), lambda b, h: (base + b, q_col + nh + h)),
                  pl.BlockSpec((seq, hd), lambda b, h: (base + b, q_col + 2 * nh + h))],
        out_specs=[pl.BlockSpec((seq, hd), lambda b, h: (b, h)), kv_spec, kv_spec],
        out_shape=[jax.ShapeDtypeStruct((nb * seq, nh * hd), BF16),
                   jax.ShapeDtypeStruct((nb, nh, seq, hd), F32),
                   jax.ShapeDtypeStruct((nb, nh, seq, hd), F32)],
        compiler_params=_cparams("parallel", "parallel"),
        name="context_attention",
    )(proj, proj, proj)


def _na_key_row0(kb, rows):
    lo = kb * NA_Q_ROWS - NA_WIN_ROWS // 2
    if isinstance(kb, (int, np.integer)):
        return int(np.clip(lo, 0, rows - NA_K_ROWS))
    return jnp.clip(lo, 0, rows - NA_K_ROWS)


def _na_bias_tables(rpb, rows):
    nh = rpb.shape[0]
    w = GRID_W
    c = np.arange(w)
    cs = np.clip(c - NA_WIN_COLS // 2, 0, w - NA_WIN_COLS)
    col_ok = (c[None, :] >= cs[:, None]) & (c[None, :] < cs[:, None] + NA_WIN_COLS)
    rel_col = np.clip(c[None, :] - c[:, None] + NA_WIN_COLS - 1, 0, 2 * NA_WIN_COLS - 2)
    planes = jnp.where(col_ok[None, None], rpb[:, :, rel_col], MASK_VALUE)
    planes = jnp.concatenate([planes, jnp.full((nh, 1, w, w), MASK_VALUE, rpb.dtype)], axis=1)
    masked_plane = 2 * NA_WIN_ROWS - 1
    variants, var_ids = [], []
    for kb in range(rows // NA_Q_ROWS):
        r = kb * NA_Q_ROWS + np.arange(NA_Q_ROWS)
        rs = np.clip(r - NA_WIN_ROWS // 2, 0, rows - NA_WIN_ROWS)
        kr = _na_key_row0(kb, rows) + np.arange(NA_K_ROWS)
        ok = (kr[None, :] >= rs[:, None]) & (kr[None, :] < rs[:, None] + NA_WIN_ROWS)
        assert ok.sum() == NA_Q_ROWS * NA_WIN_ROWS
        plane = np.where(ok, kr[None, :] - r[:, None] + NA_WIN_ROWS - 1, masked_plane)
        for vi, known in enumerate(variants):
            if np.array_equal(known, plane):
                var_ids.append(vi)
                break
        else:
            var_ids.append(len(variants))
            variants.append(plane)
    idx = np.stack(variants)
    tbl = planes[:, idx]
    tbl = tbl.transpose(0, 1, 2, 4, 3, 5).reshape(nh, len(variants), NA_Q_ROWS * w, NA_K_ROWS * w)
    return tbl, jnp.asarray(np.array(var_ids, np.int32))


def _na_attn_kernel(var_ref, q_ref, k_ref, v_ref, kc_ref, vc_ref, bias_ref, o_ref, *, rows, scale):
    del var_ref
    nk = NA_K_ROWS * GRID_W
    start = pl.multiple_of(_na_key_row0(pl.program_id(2), rows) * GRID_W, 256)
    kw = k_ref[pl.ds(start, nk), :].astype(BF16)
    vw = v_ref[pl.ds(start, nk), :].astype(BF16)
    q = q_ref[...].astype(BF16)
    s_loc = _nt_dot(q, kw) * scale + bias_ref[...]
    s_ctx = _nt_dot(q, kc_ref[...].astype(BF16)) * scale
    m = jnp.maximum(jnp.max(s_loc, axis=-1, keepdims=True), jnp.max(s_ctx, axis=-1, keepdims=True))
    p_loc = jnp.exp(s_loc - m)
    p_ctx = jnp.exp(s_ctx - m)
    l = jnp.sum(p_loc, axis=-1, keepdims=True) + jnp.sum(p_ctx, axis=-1, keepdims=True)
    o = (jnp.dot(p_loc.astype(BF16), vw, preferred_element_type=F32)
         + jnp.dot(p_ctx.astype(BF16), vc_ref[...].astype(BF16), preferred_element_type=F32))
    o_ref[...] = (o / l).astype(o_ref.dtype)


def _na_attn_call(proj, cache_k, cache_v, layer, rpb, nb, dec_seq, q_col, hd):
    rows = dec_seq // GRID_W
    assert rows >= NA_K_ROWS and rows % NA_Q_ROWS == 0
    nh = NA_HEADS
    nq = NA_Q_ROWS * GRID_W
    nkb = rows // NA_Q_ROWS
    past = cache_k.shape[3]
    tbl, var_ids = _na_bias_tables(rpb, rows)
    ctx_spec = pl.BlockSpec((None, None, None, past, hd), lambda b, h, k, var: (b, layer, h, 0, 0))
    grid_spec = pltpu.PrefetchScalarGridSpec(
        num_scalar_prefetch=1,
        grid=(nb, nh, nkb),
        in_specs=[pl.BlockSpec((nq, hd), lambda b, h, k, var: (b * nkb + k, q_col + h)),
                  pl.BlockSpec((dec_seq, hd), lambda b, h, k, var: (b, q_col + nh + h)),
                  pl.BlockSpec((dec_seq, hd), lambda b, h, k, var: (b, q_col + 2 * nh + h)),
                  ctx_spec, ctx_spec,
                  pl.BlockSpec((None, None, nq, NA_K_ROWS * GRID_W),
                               lambda b, h, k, var: (h, var[k], 0, 0))],
        out_specs=pl.BlockSpec((nq, hd), lambda b, h, k, var: (b * nkb + k, h)),
    )
    return pl.pallas_call(
        functools.partial(_na_attn_kernel, rows=rows, scale=hd ** -0.5),
        grid_spec=grid_spec,
        out_shape=jax.ShapeDtypeStruct((nb * dec_seq, nh * hd), BF16),
        compiler_params=_cparams("parallel", "parallel", "arbitrary"),
        name="neighbourhood_attention",
    )(var_ids, proj, proj, proj, cache_k, cache_v, tbl)


def _merge_kernel(f_ref, p_ref, a_ref, c_ref, g0_ref, g1_ref, g2_ref, g3_ref, w_ref, o_ref):
    acc = None
    for n, (br, gt) in enumerate(((f_ref, g0_ref), (p_ref, g1_ref), (a_ref, g2_ref), (c_ref, g3_ref))):
        y = gt[...].astype(F32) * jnp.dot(br[...], w_ref[n], preferred_element_type=F32)
        acc = y if acc is None else acc + y
    o_ref[...] = acc.astype(o_ref.dtype)


def _merge_call(branches, gates, w_branch, tm, tn):
    t, bw = branches[0].shape
    d = w_branch.shape[2]
    nj = d // tn
    br_spec = pl.BlockSpec((tm, bw), lambda i, j: (i, 0))
    gate_specs = [pl.BlockSpec((tm, tn), functools.partial(lambda i, j, n: (i, n * nj + j), n=n))
                  for n in range(N_BRANCH)]
    return pl.pallas_call(
        _merge_kernel,
        grid=(t // tm, nj),
        in_specs=[br_spec] * N_BRANCH + gate_specs
                 + [pl.BlockSpec((N_BRANCH, bw, tn), lambda i, j: (0, 0, j))],
        out_specs=pl.BlockSpec((tm, tn), lambda i, j: (i, j)),
        out_shape=jax.ShapeDtypeStruct((t, d), BF16),
        compiler_params=_cparams("parallel", "parallel"),
        name="branch_merge",
    )(*branches, gates, gates, gates, gates, w_branch)


def _out_kernel(a_ref, w_ref, x_ref, mod_ref, g_ref, *rest, gate_row, next_rows):
    if next_rows is None:
        (o_ref,) = rest
    else:
        nmod_ref, ng_ref, o_ref, h_ref = rest
    k = pl.program_id(1)

    @pl.when(k == 0)
    def _():
        o_ref[...] = jnp.zeros_like(o_ref)

    o_ref[...] += jnp.dot(a_ref[...], w_ref[...], preferred_element_type=F32)

    @pl.when(k == pl.num_programs(1) - 1)
    def _():
        xn = x_ref[...] + mod_ref[gate_row:gate_row + 1, :] * _rms(o_ref[...], g_ref[...])
        o_ref[...] = xn
        if next_rows is not None:
            h_ref[...] = _mod_norm(xn, ng_ref[...], nmod_ref, *next_rows).astype(h_ref.dtype)


def _out_call(a, w, x, mod, g, gate_row, nxt, mod_row, tm, tk):
    t, d = x.shape
    kdim = a.shape[1]
    row = pl.BlockSpec((tm, d), lambda i, k: (i, 0))
    mod_spec = pl.BlockSpec((None, 6, d), lambda i, k: (mod_row(i * tm), 0, 0))
    vec = pl.BlockSpec((1, d), lambda i, k: (0, 0))
    in_specs = [pl.BlockSpec((tm, tk), lambda i, k: (i, k)),
                pl.BlockSpec((tk, d), lambda i, k: (k, 0)),
                row, mod_spec, vec]
    args = [a, w, x, mod, g]
    out_specs, out_shape = row, jax.ShapeDtypeStruct((t, d), F32)
    if nxt is not None:
        in_specs += [mod_spec, vec]
        args += [nxt[0], nxt[1]]
        out_specs = [row, row]
        out_shape = [out_shape, jax.ShapeDtypeStruct((t, d), BF16)]
    return pl.pallas_call(
        functools.partial(_out_kernel, gate_row=gate_row, next_rows=None if nxt is None else nxt[2]),
        grid=(t // tm, kdim // tk),
        in_specs=in_specs,
        out_specs=out_specs,
        out_shape=out_shape,
        compiler_params=_cparams("parallel", "arbitrary", vmem_mib=VMEM_LIMIT_WIDE_MIB),
        name="proj_norm_residual",
    )(*args)


def _ffn_up_kernel(h_ref, wg_ref, wu_ref, o_ref):
    h = h_ref[...]
    y = _silu(jnp.dot(h, wg_ref[...], preferred_element_type=F32)) * jnp.dot(
        h, wu_ref[...], preferred_element_type=F32)
    o_ref[...] = y.astype(o_ref.dtype)


def _ffn_up_call(h, wg, wu, tm, tn):
    t, d = h.shape
    f = wg.shape[1]
    w_spec = pl.BlockSpec((d, tn), lambda i, j: (0, j))
    return pl.pallas_call(
        _ffn_up_kernel,
        grid=(t // tm, f // tn),
        in_specs=[pl.BlockSpec((tm, d), lambda i, j: (i, 0)), w_spec, w_spec],
        out_specs=pl.BlockSpec((tm, tn), lambda i, j: (i, j)),
        out_shape=jax.ShapeDtypeStruct((t, f), BF16),
        compiler_params=_cparams("parallel", "parallel"),
        name="ffn_up",
    )(h, wg, wu)


def _split_bf16(x):
    hi = x.astype(BF16)
    return hi, (x - hi.astype(F32)).astype(BF16)


def _pack_bf16_pairs(h):
    n = h.shape[1] // 2
    hb = h.astype(BF16).astype(F32)
    hi = pltpu.bitcast(hb[:, :n], jnp.int32)
    lo = pltpu.bitcast(hb[:, n:], jnp.int32)
    return hi | lax.shift_right_logical(lo, 16)


def _unpack_bf16_pairs(w):
    hi = pltpu.bitcast(w & jnp.int32(-65536), F32).astype(BF16)
    lo = pltpu.bitcast(w << 16, F32).astype(BF16)
    return hi, lo


def _router_kernel(x_ref, mod_ref, g_ref, w_ref, b_ref, o_ref, hp_ref, *, n_experts):
    h = _mod_norm(x_ref[...], g_ref[...], mod_ref, SCALE2, SHIFT2)
    hp_ref[...] = _pack_bf16_pairs(h)
    h_hi, h_lo = _split_bf16(h)
    w_hi, w_lo = _split_bf16(w_ref[...])
    logits = (jnp.dot(h_hi, w_hi, preferred_element_type=F32)
              + jnp.dot(h_lo, w_hi, preferred_element_type=F32)
              + jnp.dot(h_hi, w_lo, preferred_element_type=F32)) + b_ref[...]
    lane = lax.broadcasted_iota(jnp.int32, logits.shape, 1).astype(F32)
    neg = -jnp.inf
    no_lane = float(logits.shape[1])
    logits = jnp.where(lane < n_experts, logits, neg)
    m1 = jnp.max(logits, axis=-1, keepdims=True)
    i1 = jnp.min(jnp.where(logits == m1, lane, no_lane), axis=-1, keepdims=True)
    rest = jnp.where(lane == i1, neg, logits)
    m2 = jnp.max(rest, axis=-1, keepdims=True)
    i2 = jnp.min(jnp.where(rest == m2, lane, no_lane), axis=-1, keepdims=True)
    e2 = jnp.exp(m2 - m1)
    den = 1.0 + e2
    o_ref[...] = (jnp.where(lane == 0.0, i1, 0.0) + jnp.where(lane == 1.0, i2, 0.0)
                  + jnp.where(lane == 2.0, 1.0 / den, 0.0) + jnp.where(lane == 3.0, e2 / den, 0.0))


def _router_call(x, mod, g, w_router, b_router, mod_row, tm):
    t, d = x.shape
    ne = w_router.shape[1]
    lanes = 128
    w = jnp.zeros((d, lanes), F32).at[:, :ne].set(w_router)
    b = jnp.zeros((1, lanes), F32).at[0, :ne].set(b_router)
    return pl.pallas_call(
        functools.partial(_router_kernel, n_experts=ne),
        grid=(t // tm,),
        in_specs=[pl.BlockSpec((tm, d), lambda i: (i, 0)),
                  pl.BlockSpec((None, 6, d), lambda i: (mod_row(i * tm), 0, 0)),
                  pl.BlockSpec((1, d), lambda i: (0, 0)),
                  pl.BlockSpec((d, lanes), lambda i: (0, 0)),
                  pl.BlockSpec((1, lanes), lambda i: (0, 0))],
        out_specs=[pl.BlockSpec((tm, lanes), lambda i: (i, 0)),
                   pl.BlockSpec((tm, d // 2), lambda i: (i, 0))],
        out_shape=[jax.ShapeDtypeStruct((t, lanes), F32),
                   jax.ShapeDtypeStruct((t, d // 2), jnp.int32)],
        compiler_params=_cparams("parallel"),
        name="router",
    )(x, mod, g, w, b)


def _routing_tables(route, n_experts, tm):
    t = route.shape[0]
    na = TOP_K * t
    p_rows = na + n_experts * tm
    nt = p_rows // tm
    e_flat = route[:, :TOP_K].astype(jnp.int32).reshape(na)
    w_flat = route[:, TOP_K:2 * TOP_K].reshape(na)
    onehot = (e_flat[:, None] == jnp.arange(n_experts, dtype=jnp.int32)[None, :]).astype(jnp.int32)
    csum = jnp.cumsum(onehot, axis=0)
    rank = jnp.sum((csum - onehot) * onehot, axis=1)
    counts = csum[-1]
    padded = ((counts + tm - 1) // tm) * tm
    ends = jnp.cumsum(padded)
    pos = (ends - padded)[e_flat] + rank
    src = jnp.zeros((p_rows,), jnp.int32).at[pos].set(jnp.arange(na, dtype=jnp.int32) // TOP_K)
    wrow = jnp.zeros((p_rows,), F32).at[pos].set(w_flat)
    tile_expert = jnp.minimum(
        jnp.searchsorted(ends, jnp.arange(nt, dtype=jnp.int32) * tm, side="right"), n_experts - 1)
    meta = jnp.concatenate([tile_expert.astype(jnp.int32), (ends[-1:] // tm).astype(jnp.int32)])
    return src, wrow[:, None], pos.astype(jnp.int32), meta


def _gather_rows_kernel(idx_ref, src_ref, o_ref, buf_ref, sem_ref, *, rows):
    i = pl.program_id(0)
    n = pl.num_programs(0)

    def issue(tile, slot):
        def body(r, carry):
            pltpu.make_async_copy(src_ref.at[pl.ds(idx_ref[tile * rows + r], 1)],
                                  buf_ref.at[slot, pl.ds(r, 1)], sem_ref.at[slot]).start()
            return carry
        lax.fori_loop(0, rows, body, 0)

    @pl.when(i == 0)
    def _():
        issue(0, 0)

    @pl.when(i + 1 < n)
    def _():
        issue(i + 1, (i + 1) % 2)

    slot = i % 2
    pltpu.make_async_copy(src_ref.at[pl.ds(0, rows)], buf_ref.at[slot], sem_ref.at[slot]).wait()
    half = buf_ref.shape[2]
    hi, lo = _unpack_bf16_pairs(buf_ref[slot])
    o_ref[:, :half] = hi
    o_ref[:, half:] = lo


def _gather_rows_call(src, idx, rows):
    p_rows = idx.shape[0]
    half = src.shape[1]
    grid_spec = pltpu.PrefetchScalarGridSpec(
        num_scalar_prefetch=1,
        grid=(p_rows // rows,),
        in_specs=[pl.BlockSpec(memory_space=pl.ANY)],
        out_specs=pl.BlockSpec((rows, 2 * half), lambda i, idx: (i, 0)),
        scratch_shapes=[pltpu.VMEM((2, rows, half), src.dtype), pltpu.SemaphoreType.DMA((2,))],
    )
    return pl.pallas_call(
        functools.partial(_gather_rows_kernel, rows=rows),
        grid_spec=grid_spec,
        out_shape=jax.ShapeDtypeStruct((p_rows, 2 * half), BF16),
        compiler_params=_cparams("arbitrary"),
        name="moe_gather",
    )(idx, src)


def _moe_up_kernel(meta_ref, xs_ref, wrow_ref, wg_ref, wu_ref, o_ref, wgb_ref, wub_ref, *, n_tiles):
    i = pl.program_id(1)
    used = i < meta_ref[n_tiles]
    fresh = jnp.logical_or(i == 0, meta_ref[i] != meta_ref[jnp.maximum(i - 1, 0)])

    @pl.when(jnp.logical_and(used, fresh))
    def _():
        wgb_ref[...] = wg_ref[...].astype(BF16)
        wub_ref[...] = wu_ref[...].astype(BF16)

    @pl.when(used)
    def _():
        h = xs_ref[...]
        y = _silu(jnp.dot(h, wgb_ref[...], preferred_element_type=F32)) * jnp.dot(
            h, wub_ref[...], preferred_element_type=F32)
        o_ref[...] = (y * wrow_ref[...]).astype(o_ref.dtype)

    @pl.when(jnp.logical_not(used))
    def _():
        o_ref[...] = jnp.zeros_like(o_ref)


def _moe_up_call(xs, wrow, meta, wg, wu, tm, tn):
    p_rows, d = xs.shape
    f = wg.shape[2]
    nt = p_rows // tm
    w_spec = pl.BlockSpec((None, d, tn), lambda j, i, meta: (meta[i], 0, j))
    grid_spec = pltpu.PrefetchScalarGridSpec(
        num_scalar_prefetch=1,
        grid=(f // tn, nt),
        in_specs=[pl.BlockSpec((tm, d), lambda j, i, meta: (i, 0)),
                  pl.BlockSpec((tm, 1), lambda j, i, meta: (i, 0)),
                  w_spec, w_spec],
        out_specs=pl.BlockSpec((None, tm, tn), lambda j, i, meta: (j, i, 0)),
        scratch_shapes=[pltpu.VMEM((d, tn), BF16), pltpu.VMEM((d, tn), BF16)],
    )
    return pl.pallas_call(
        functools.partial(_moe_up_kernel, n_tiles=nt),
        grid_spec=grid_spec,
        out_shape=jax.ShapeDtypeStruct((f // tn, p_rows, tn), BF16),
        compiler_params=_cparams("arbitrary", "arbitrary"),
        name="moe_up",
    )(meta, xs, wrow, wg, wu)


def _moe_down_kernel(meta_ref, a_ref, w_ref, o_ref, *, n_tiles):
    i = pl.program_id(0)

    @pl.when(pl.program_id(1) == 0)
    def _():
        o_ref[...] = jnp.zeros_like(o_ref)

    @pl.when(i < meta_ref[n_tiles])
    def _():
        o_ref[...] += jnp.dot(a_ref[...], w_ref[...], preferred_element_type=F32)


def _moe_down_call(a, meta, wd, tm):
    nk, p_rows, tk = a.shape
    d = wd.shape[2]
    nt = p_rows // tm
    grid_spec = pltpu.PrefetchScalarGridSpec(
        num_scalar_prefetch=1,
        grid=(nt, nk),
        in_specs=[pl.BlockSpec((None, tm, tk), lambda i, k, meta: (k, i, 0)),
                  pl.BlockSpec((None, tk, d), lambda i, k, meta: (meta[i], k, 0))],
        out_specs=pl.BlockSpec((tm, d), lambda i, k, meta: (i, 0)),
    )
    return pl.pallas_call(
        functools.partial(_moe_down_kernel, n_tiles=nt),
        grid_spec=grid_spec,
        out_shape=jax.ShapeDtypeStruct((p_rows, d), F32),
        compiler_params=_cparams("parallel", "arbitrary"),
        name="moe_down",
    )(meta, a, wd)


def _moe_combine_kernel(pos_ref, ys_ref, x_ref, mod_ref, g_ref, o_ref, buf_ref, sem_ref, *, rows):
    i = pl.program_id(0)
    n = pl.num_programs(0)

    def issue(tile, slot):
        def body(r, carry):
            for s in range(TOP_K):
                pltpu.make_async_copy(ys_ref.at[pl.ds(pos_ref[TOP_K * (tile * rows + r) + s], 1)],
                                      buf_ref.at[slot, s, pl.ds(r, 1)], sem_ref.at[slot]).start()
            return carry
        lax.fori_loop(0, rows, body, 0)

    @pl.when(i == 0)
    def _():
        issue(0, 0)

    @pl.when(i + 1 < n)
    def _():
        issue(i + 1, (i + 1) % 2)

    slot = i % 2
    for s in range(TOP_K):
        pltpu.make_async_copy(ys_ref.at[pl.ds(0, rows)], buf_ref.at[slot, s], sem_ref.at[slot]).wait()
    y = buf_ref[slot, 0]
    for s in range(1, TOP_K):
        y = y + buf_ref[slot, s]
    o_ref[...] = x_ref[...] + mod_ref[GATE2:GATE2 + 1, :] * _rms(y, g_ref[...])


def _moe_combine_call(ys, pos, x, mod, g, mod_row, rows):
    t, d = x.shape
    grid_spec = pltpu.PrefetchScalarGridSpec(
        num_scalar_prefetch=1,
        grid=(t // rows,),
        in_specs=[pl.BlockSpec(memory_space=pl.ANY),
                  pl.BlockSpec((rows, d), lambda i, pos: (i, 0)),
                  pl.BlockSpec((None, 6, d), lambda i, pos: (mod_row(i * rows), 0, 0)),
                  pl.BlockSpec((1, d), lambda i, pos: (0, 0))],
        out_specs=pl.BlockSpec((rows, d), lambda i, pos: (i, 0)),
        scratch_shapes=[pltpu.VMEM((2, TOP_K, rows, d), F32), pltpu.SemaphoreType.DMA((2,))],
    )
    return pl.pallas_call(
        functools.partial(_moe_combine_kernel, rows=rows),
        grid_spec=grid_spec,
        out_shape=jax.ShapeDtypeStruct((t, d), F32),
        compiler_params=_cparams("arbitrary"),
        name="moe_combine",
    )(pos, ys, x, mod, g)


def kernel(x_prompt, x_sample, cache_k, cache_v, c, c_ctx, w_ada, b_ada, norm_g, w_in, b_in, pool_w, pool_scale, rpb, conv_w, conv_b, conv_ln_g, conv_ln_b, w_branch, w_out, w_gate_d, w_up_d, w_down_d, w_router, b_router, w_gate_e, w_up_e, w_down_e):
    nbp, seq, d = x_prompt.shape
    nbs, dec_seq, _ = x_sample.shape
    depth = w_ada.shape[0]
    bw = d // N_BRANCH
    hd = bw // NA_HEADS
    ns_rows = nbs * dec_seq
    np_rows = nbp * seq
    assert dec_seq & (dec_seq - 1) == 0 and seq & (seq - 1) == 0
    assert nbs + 1 <= MOD_ROWS

    t_rows = ns_rows + np_rows
    tm_wide = max(tm for tm in (2048, 1024, 512) if dec_seq % tm == 0 and np_rows % tm == 0)
    tm_mid = min(tm_wide, 1024)
    tm_emit = 512
    tm_moe = 512
    tile_seq = min(256, seq)
    tn = 512
    assert t_rows % tm_wide == 0 and seq % tile_seq == 0

    def mod_row(row0):
        return jnp.where(row0 < ns_rows, row0 // dec_seq, nbs)

    x = jnp.concatenate([x_sample.reshape(ns_rows, d), x_prompt.reshape(np_rows, d)], axis=0)
    cvec = jnp.zeros((MOD_ROWS, d), F32).at[:nbs].set(c).at[nbs].set(c_ctx)
    mods = _ada_call(cvec, w_ada, b_ada).reshape(depth, MOD_ROWS, 6, d)
    cc, sc = _channel_dft_mats(bw, FNET_GROUPS)

    pool_col, conv_col, main_cols = 1, 5, 7 * bw
    q_col = 2 * bw // hd

    def gain(l, n):
        return norm_g[l, n][None, :]

    h = _norm_call(x, mods[0], gain(0, 0), SCALE1, SHIFT1, mod_row, tm_mid)
    new_k, new_v = [], []
    for l in range(depth):
        mod = mods[l]
        proj, gates = _in_call(h, w_in[l].astype(BF16), b_in[l][None, :], main_cols, tm_wide, tn)

        o_f = jnp.concatenate([_fourier_call(proj, 0, nbs, dec_seq, bw, cc, sc),
                               _fourier_call(proj, ns_rows, nbp, seq, bw, cc, sc)], axis=0)
        o_p = _pool_call(proj, pool_w[l].astype(BF16), pool_scale[l][None, :], pool_col, bw,
                         tile_seq, ns_rows, dec_seq, seq)
        o_c = _conv_call(proj, conv_w[l], conv_b[l][None, :], conv_ln_g[l][None, :],
                         conv_ln_b[l][None, :], conv_col, bw, tile_seq, ns_rows, dec_seq, seq)
        o_a_lat = _na_attn_call(proj, cache_k, cache_v, l, rpb[l], nbs, dec_seq, q_col, hd)
        o_a_ctx, k_ctx, v_ctx = _ctx_attn_call(proj, ns_rows, nbp, seq, q_col, hd)
        new_k.append(k_ctx)
        new_v.append(v_ctx)
        o_a = jnp.concatenate([o_a_lat, o_a_ctx], axis=0)

        merged = _merge_call([o_f, o_p, o_a, o_c], gates, w_branch[l].astype(BF16), tm_mid, tn)
        nxt = None if l + 1 == depth else (mods[l + 1], gain(l + 1, 0), (SCALE1, SHIFT1))
        i = l // 2
        if l % 2 == 1:
            x = _out_call(merged, w_out[l].astype(BF16), x, mod, gain(l, 1), GATE1, None,
                          mod_row, tm_mid, tn)
            ne = w_router.shape[2]
            route, h_packed = _router_call(x, mod, gain(l, 2), w_router[i], b_router[i], mod_row, 512)
            src, wrow, pos, meta = _routing_tables(route, ne, tm_moe)
            xs = _gather_rows_call(h_packed, src, 256)
            hmid = _moe_up_call(xs, wrow, meta, w_gate_e[i], w_up_e[i], tm_moe, tn)
            ys = _moe_down_call(hmid, meta, w_down_e[i].astype(BF16), tm_moe)
            x = _moe_combine_call(ys, pos, x, mod, gain(l, 3), mod_row, 256)
            if nxt is not None:
                h = _norm_call(x, nxt[0], nxt[1], *nxt[2], mod_row, tm_mid)
        else:
            x, h2 = _out_call(merged, w_out[l].astype(BF16), x, mod, gain(l, 1), GATE1,
                              (mod, gain(l, 2), (SCALE2, SHIFT2)), mod_row, tm_emit, tn)
            hmid = _ffn_up_call(h2, w_gate_d[i].astype(BF16), w_up_d[i].astype(BF16), tm_wide, tn)
            out = _out_call(hmid, w_down_d[i].astype(BF16), x, mod, gain(l, 3), GATE2, nxt,
                            mod_row, tm_mid if nxt is None else tm_emit, tn)
            x, h = out if nxt is not None else (out, None)

    y_sample = x[:ns_rows].reshape(nbs, dec_seq, d)
    y_prompt = x[ns_rows:].reshape(nbp, seq, d)
    return y_prompt, y_sample, jnp.stack(new_k, axis=1), jnp.stack(new_v, axis=1)
```

```python
import functools

import numpy as np
import jax
import jax.numpy as jnp
from jax import lax
from jax.experimental import pallas as pl
from jax.experimental.pallas import tpu as pltpu

F32 = jnp.float32
BF16 = jnp.bfloat16

N_BRANCH = 4
FNET_GROUPS = 4
POOL_WINDOWS = (2, 4, 8, 16)
NA_HEADS = 4
NA_WIN_ROWS = 8
NA_WIN_COLS = 16
GRID_W = 64
CONV_K = 31
TOP_K = 2
RMS_EPS = 1e-6
LN_EPS = 1e-5
MASK_VALUE = -1e30
MOD_ROWS = 8
POOL_HALO = 8
CONV_HALO = 16
NA_Q_ROWS = 8
NA_K_ROWS = 16
DMA_ISSUE_UNROLL = 8
VMEM_LIMIT_MIB = 48
VMEM_LIMIT_WIDE_MIB = 56

SHIFT1, SCALE1, GATE1, SHIFT2, SCALE2, GATE2 = range(6)


def _cparams(*sem, vmem_mib=VMEM_LIMIT_MIB):
    return pltpu.CompilerParams(dimension_semantics=sem, vmem_limit_bytes=vmem_mib * 1024 * 1024)


def _silu(x):
    return x * jax.nn.sigmoid(x)


def _rms(x, g):
    return x * lax.rsqrt(jnp.mean(x * x, axis=-1, keepdims=True) + RMS_EPS) * g


def _mod_norm(x, g, mod_ref, scale_row, shift_row):
    return (_rms(x, g) * (1.0 + mod_ref[scale_row:scale_row + 1, :])
            + mod_ref[shift_row:shift_row + 1, :])


def _nt_dot(a, b):
    return lax.dot_general(a, b, (((1,), (1,)), ((), ())), preferred_element_type=F32)


def _ada_kernel(c_ref, w_ref, b_ref, o_ref):
    s = _silu(c_ref[...]).astype(BF16)
    o_ref[...] = jnp.dot(s, w_ref[...].astype(BF16), preferred_element_type=F32) + b_ref[...]


def _ada_call(cvec, w_ada, b_ada):
    depth, d, n = w_ada.shape
    tn = 512
    return pl.pallas_call(
        _ada_kernel,
        grid=(depth, n // tn),
        in_specs=[pl.BlockSpec((MOD_ROWS, d), lambda l, j: (0, 0)),
                  pl.BlockSpec((None, d, tn), lambda l, j: (l, 0, j)),
                  pl.BlockSpec((None, 1, tn), lambda l, j: (l, 0, j))],
        out_specs=pl.BlockSpec((None, MOD_ROWS, tn), lambda l, j: (l, 0, j)),
        out_shape=jax.ShapeDtypeStruct((depth, MOD_ROWS, n), F32),
        compiler_params=_cparams("parallel", "parallel"),
        name="adaln",
    )(cvec, w_ada, b_ada.reshape(depth, 1, n))


def _norm_kernel(x_ref, mod_ref, g_ref, o_ref, *, scale_row, shift_row):
    o_ref[...] = _mod_norm(x_ref[...], g_ref[...], mod_ref, scale_row, shift_row).astype(o_ref.dtype)


def _norm_call(x, mod, g, scale_row, shift_row, mod_row, tm):
    t, d = x.shape
    return pl.pallas_call(
        functools.partial(_norm_kernel, scale_row=scale_row, shift_row=shift_row),
        grid=(t // tm,),
        in_specs=[pl.BlockSpec((tm, d), lambda i: (i, 0)),
                  pl.BlockSpec((None, 6, d), lambda i: (mod_row(i * tm), 0, 0)),
                  pl.BlockSpec((1, d), lambda i: (0, 0))],
        out_specs=pl.BlockSpec((tm, d), lambda i: (i, 0)),
        out_shape=jax.ShapeDtypeStruct((t, d), BF16),
        compiler_params=_cparams("parallel"),
        name="mod_norm",
    )(x, mod, g)


def _in_kernel(h_ref, w_ref, b_ref, main_ref, gate_ref, *, n_main):
    j = pl.program_id(1)
    y = jnp.dot(h_ref[...], w_ref[...], preferred_element_type=F32) + b_ref[...]

    @pl.when(j < n_main)
    def _():
        main_ref[...] = y

    @pl.when(j >= n_main)
    def _():
        gate_ref[...] = jax.nn.sigmoid(y).astype(gate_ref.dtype)


def _in_call(h, w, b, main_cols, tm, tn):
    t, d = h.shape
    n = w.shape[1]
    n_main = main_cols // tn
    return pl.pallas_call(
        functools.partial(_in_kernel, n_main=n_main),
        grid=(t // tm, n // tn),
        in_specs=[pl.BlockSpec((tm, d), lambda i, j: (i, 0)),
                  pl.BlockSpec((d, tn), lambda i, j: (0, j)),
                  pl.BlockSpec((1, tn), lambda i, j: (0, j))],
        out_specs=[pl.BlockSpec((tm, tn), lambda i, j: (i, jnp.minimum(j, n_main - 1))),
                   pl.BlockSpec((tm, tn), lambda i, j: (i, jnp.maximum(j - n_main, 0)))],
        out_shape=[jax.ShapeDtypeStruct((t, main_cols), F32),
                   jax.ShapeDtypeStruct((t, n - main_cols), BF16)],
        compiler_params=_cparams("parallel", "arbitrary"),
        name="in_proj",
    )(h, w, b)


def _dft_mats(n):
    scale = 1.0 / np.sqrt(n)
    j = jnp.arange(n, dtype=jnp.int32)
    if n <= 1024:
        ang = ((j[:, None] * j[None, :]) % n).astype(F32) * (2.0 * np.pi / n)
        return (jnp.cos(ang) * scale).astype(BF16), (jnp.sin(ang) * scale).astype(BF16)
    base = 64
    hi = n // base
    k1 = jnp.arange(hi, dtype=jnp.int32)
    k0 = jnp.arange(base, dtype=jnp.int32)
    a = ((j[:, None] * k1[None, :]) % hi).astype(F32) * (2.0 * np.pi / hi)
    b = ((j[:, None] * k0[None, :]) % n).astype(F32) * (2.0 * np.pi / n)
    ca, sa = jnp.cos(a)[:, :, None], jnp.sin(a)[:, :, None]
    cb, sb = jnp.cos(b)[:, None, :] * scale, jnp.sin(b)[:, None, :] * scale
    c = (ca * cb - sa * sb).reshape(n, n)
    s = (sa * cb + ca * sb).reshape(n, n)
    return c.astype(BF16), s.astype(BF16)


def _channel_dft_mats(width, groups):
    gw = width // groups
    k = np.arange(gw)
    ang = 2.0 * np.pi * ((k[:, None] * k[None, :]) % gw) / gw
    c = np.zeros((width, width), np.float32)
    s = np.zeros((width, width), np.float32)
    for g in range(groups):
        sl = slice(g * gw, (g + 1) * gw)
        c[sl, sl] = np.cos(ang) / np.sqrt(gw)
        s[sl, sl] = np.sin(ang) / np.sqrt(gw)
    return jnp.asarray(c, BF16), jnp.asarray(s, BF16)


def _fnet1_kernel(u_ref, cc_ref, sc_ref, vc_ref, vs_ref):
    u = u_ref[...].astype(BF16)
    vc_ref[...] = jnp.dot(u, cc_ref[...], preferred_element_type=F32).astype(BF16)
    vs_ref[...] = jnp.dot(u, sc_ref[...], preferred_element_type=F32).astype(BF16)


def _fnet2_kernel(c_ref, s_ref, vc_ref, vs_ref, o_ref, acc_ref):
    k = pl.program_id(2)

    @pl.when(k == 0)
    def _():
        acc_ref[...] = jnp.zeros_like(acc_ref)

    acc_ref[...] += (jnp.dot(c_ref[...], vc_ref[...], preferred_element_type=F32)
                     - jnp.dot(s_ref[...], vs_ref[...], preferred_element_type=F32))

    @pl.when(k == pl.num_programs(2) - 1)
    def _():
        o_ref[...] = acc_ref[...].astype(o_ref.dtype)


def _fourier_call(proj, row_base, nb, seq, bw, cc, sc):
    c_l, s_l = _dft_mats(seq)
    tm1 = min(seq, 512)
    nt1 = seq // tm1
    base1 = row_base // tm1
    vc, vs = pl.pallas_call(
        _fnet1_kernel,
        grid=(nb, nt1),
        in_specs=[pl.BlockSpec((tm1, bw), lambda b, t: (base1 + b * nt1 + t, 0)),
                  pl.BlockSpec((bw, bw), lambda b, t: (0, 0)),
                  pl.BlockSpec((bw, bw), lambda b, t: (0, 0))],
        out_specs=[pl.BlockSpec((tm1, bw), lambda b, t: (t, b)),
                   pl.BlockSpec((tm1, bw), lambda b, t: (t, b))],
        out_shape=[jax.ShapeDtypeStruct((seq, nb * bw), BF16)] * 2,
        compiler_params=_cparams("parallel", "parallel"),
        name="fnet_channels",
    )(proj, cc, sc)
    tm2 = min(seq, 1024)
    tk = min(seq, 2048)
    nt2 = seq // tm2
    return pl.pallas_call(
        _fnet2_kernel,
        grid=(nt2, nb, seq // tk),
        in_specs=[pl.BlockSpec((tm2, tk), lambda i, j, k: (i, k)),
                  pl.BlockSpec((tm2, tk), lambda i, j, k: (i, k)),
                  pl.BlockSpec((tk, bw), lambda i, j, k: (k, j)),
                  pl.BlockSpec((tk, bw), lambda i, j, k: (k, j))],
        out_specs=pl.BlockSpec((tm2, bw), lambda i, j, k: (j * nt2 + i, 0)),
        out_shape=jax.ShapeDtypeStruct((nb * seq, bw), BF16),
        scratch_shapes=[pltpu.VMEM((tm2, bw), F32)],
        compiler_params=_cparams("parallel", "parallel", "arbitrary"),
        name="fnet_positions",
    )(c_l, s_l, vc, vs)


def _seq_position(row0, ns_rows, dec_seq, seq):
    is_lat = row0 < ns_rows
    seq_len = jnp.where(is_lat, dec_seq, seq)
    pos0 = jnp.where(is_lat, row0 & (dec_seq - 1), row0 & (seq - 1))
    return seq_len, pos0


def _pool_kernel(prev_ref, cur_ref, nxt_ref, w_ref, sc_ref, o_ref, ext_ref, *,
                 tm, ns_rows, dec_seq, seq):
    seq_len, pos0 = _seq_position(pl.program_id(0) * tm, ns_rows, dec_seq, seq)
    h = POOL_HALO
    ext_ref[0:h, :] = jnp.where(pos0 == 0, 0.0, prev_ref[...])
    ext_ref[h:h + tm, :] = cur_ref[...]
    ext_ref[h + tm:2 * h + tm, :] = jnp.where(pos0 + tm == seq_len, 0.0, nxt_ref[...])
    t = pos0 + lax.broadcasted_iota(jnp.int32, (tm, 1), 0)
    gw = cur_ref.shape[1] // len(POOL_WINDOWS)
    for gi, win in enumerate(POOL_WINDOWS):
        lo = win // 2
        hi = win - lo
        cols = slice(gi * gw, (gi + 1) * gw)
        s = ext_ref[h - lo:h - lo + tm, cols]
        for j in range(1 - lo, hi):
            s = s + ext_ref[h + j:h + j + tm, cols]
        cnt = (jnp.minimum(t + hi, seq_len) - jnp.maximum(t - lo, 0)).astype(F32)
        dlt = s / cnt - cur_ref[:, cols]
        y = jnp.dot(dlt.astype(BF16), w_ref[gi], preferred_element_type=F32) * sc_ref[:, cols]
        o_ref[:, cols] = y.astype(o_ref.dtype)


def _pool_call(proj, pool_w, pool_scale, col_blk, bw, tm, ns_rows, dec_seq, seq):
    t = proj.shape[0]
    h = POOL_HALO
    r = tm // h
    last = t // h - 1
    gw = bw // len(POOL_WINDOWS)
    return pl.pallas_call(
        functools.partial(_pool_kernel, tm=tm, ns_rows=ns_rows, dec_seq=dec_seq, seq=seq),
        grid=(t // tm,),
        in_specs=[pl.BlockSpec((h, bw), lambda i: (jnp.maximum(i * r - 1, 0), col_blk)),
                  pl.BlockSpec((tm, bw), lambda i: (i, col_blk)),
                  pl.BlockSpec((h, bw), lambda i: (jnp.minimum((i + 1) * r, last), col_blk)),
                  pl.BlockSpec((len(POOL_WINDOWS), gw, gw), lambda i: (0, 0, 0)),
                  pl.BlockSpec((1, bw), lambda i: (0, 0))],
        out_specs=pl.BlockSpec((tm, bw), lambda i: (i, 0)),
        out_shape=jax.ShapeDtypeStruct((t, bw), BF16),
        scratch_shapes=[pltpu.VMEM((tm + 2 * h, bw), F32)],
        compiler_params=_cparams("parallel"),
        name="pool_mix",
    )(proj, proj, proj, pool_w, pool_scale)


def _conv_kernel(ap_ref, ac_ref, an_ref, gp_ref, gc_ref, gn_ref, w_ref, b_ref, lg_ref, lb_ref,
                 o_ref, ext_ref, y_ref, *, tm, ns_rows, dec_seq, seq):
    seq_len, pos0 = _seq_position(pl.program_id(0) * tm, ns_rows, dec_seq, seq)
    h = CONV_HALO
    ext_ref[0:h, :] = jnp.where(pos0 == 0, 0.0, ap_ref[...] * jax.nn.sigmoid(gp_ref[...]))
    ext_ref[h:h + tm, :] = ac_ref[...] * jax.nn.sigmoid(gc_ref[...])
    ext_ref[h + tm:2 * h + tm, :] = jnp.where(pos0 + tm == seq_len, 0.0,
                                              an_ref[...] * jax.nn.sigmoid(gn_ref[...]))
    bw = ac_ref.shape[1]
    lanes = 128
    for c in range(bw // lanes):
        cols = slice(c * lanes, (c + 1) * lanes)
        acc = jnp.zeros((tm, lanes), F32)
        for k in range(CONV_K):
            off = h + k - CONV_K // 2
            acc = acc + ext_ref[off:off + tm, cols] * w_ref[k:k + 1, cols]
        y_ref[:, cols] = acc + b_ref[:, cols]
    y = y_ref[...]
    mu = jnp.mean(y, axis=-1, keepdims=True)
    var = jnp.mean(jnp.square(y - mu), axis=-1, keepdims=True)
    z = (y - mu) * lax.rsqrt(var + LN_EPS) * lg_ref[...] + lb_ref[...]
    o_ref[...] = _silu(z).astype(o_ref.dtype)


def _conv_call(proj, conv_w, conv_b, ln_g, ln_b, col_blk, bw, tm, ns_rows, dec_seq, seq):
    t = proj.shape[0]
    h = CONV_HALO
    r = tm // h
    last = t // h - 1
    prev = lambda i: jnp.maximum(i * r - 1, 0)
    nxt = lambda i: jnp.minimum((i + 1) * r, last)
    vec = pl.BlockSpec((1, bw), lambda i: (0, 0))
    return pl.pallas_call(
        functools.partial(_conv_kernel, tm=tm, ns_rows=ns_rows, dec_seq=dec_seq, seq=seq),
        grid=(t // tm,),
        in_specs=[pl.BlockSpec((h, bw), lambda i: (prev(i), col_blk)),
                  pl.BlockSpec((tm, bw), lambda i: (i, col_blk)),
                  pl.BlockSpec((h, bw), lambda i: (nxt(i), col_blk)),
                  pl.BlockSpec((h, bw), lambda i: (prev(i), col_blk + 1)),
                  pl.BlockSpec((tm, bw), lambda i: (i, col_blk + 1)),
                  pl.BlockSpec((h, bw), lambda i: (nxt(i), col_blk + 1)),
                  pl.BlockSpec((CONV_K, bw), lambda i: (0, 0)),
                  vec, vec, vec],
        out_specs=pl.BlockSpec((tm, bw), lambda i: (i, 0)),
        out_shape=jax.ShapeDtypeStruct((t, bw), BF16),
        scratch_shapes=[pltpu.VMEM((tm + 2 * h, bw), F32), pltpu.VMEM((tm, bw), F32)],
        compiler_params=_cparams("parallel"),
        name="conv_module",
    )(proj, proj, proj, proj, proj, proj, conv_w, conv_b, ln_g, ln_b)


def _ctx_attn_kernel(q_ref, k_ref, v_ref, o_ref, ko_ref, vo_ref, *, scale):
    k = k_ref[...]
    v = v_ref[...]
    s = _nt_dot(q_ref[...].astype(BF16), k.astype(BF16)) * scale
    p = jnp.exp(s - jnp.max(s, axis=-1, keepdims=True))
    l = jnp.sum(p, axis=-1, keepdims=True)
    o = jnp.dot(p.astype(BF16), v.astype(BF16), preferred_element_type=F32)
    o_ref[...] = (o / l).astype(o_ref.dtype)
    ko_ref[...] = k
    vo_ref[...] = v


def _ctx_attn_call(proj, row_base, nb, seq, q_col, hd):
    base = row_base // seq
    nh = NA_HEADS
    kv_spec = pl.BlockSpec((None, None, seq, hd), lambda b, h: (b, h, 0, 0))
    return pl.pallas_call(
        functools.partial(_ctx_attn_kernel, scale=hd ** -0.5),
        grid=(nb, nh),
        in_specs=[pl.BlockSpec((seq, hd), lambda b, h: (base + b, q_col + h)),
                  pl.BlockSpec((seq, hd), lambda b, h: (base + b, q_col + nh + h)),
                  pl.BlockSpec((seq, hd), lambda b, h: (base + b, q_col + 2 * nh + h))],
        out_specs=[pl.BlockSpec((seq, hd), lambda b, h: (b, h)), kv_spec, kv_spec],
        out_shape=[jax.ShapeDtypeStruct((nb * seq, nh * hd), BF16),
                   jax.ShapeDtypeStruct((nb, nh, seq, hd), F32),
                   jax.ShapeDtypeStruct((nb, nh, seq, hd), F32)],
        compiler_params=_cparams("parallel", "parallel"),
        name="context_attention",
    )(proj, proj, proj)


def _na_key_row0(kb, rows):
    lo = kb * NA_Q_ROWS - NA_WIN_ROWS // 2
    if isinstance(kb, (int, np.integer)):
        return int(np.clip(lo, 0, rows - NA_K_ROWS))
    return jnp.clip(lo, 0, rows - NA_K_ROWS)


def _na_bias_tables(rpb, rows):
    nh = rpb.shape[0]
    w = GRID_W
    c = np.arange(w)
    cs = np.clip(c - NA_WIN_COLS // 2, 0, w - NA_WIN_COLS)
    col_ok = (c[None, :] >= cs[:, None]) & (c[None, :] < cs[:, None] + NA_WIN_COLS)
    rel_col = np.clip(c[None, :] - c[:, None] + NA_WIN_COLS - 1, 0, 2 * NA_WIN_COLS - 2)
    planes = jnp.where(col_ok[None, None], rpb[:, :, rel_col], MASK_VALUE)
    planes = jnp.concatenate([planes, jnp.full((nh, 1, w, w), MASK_VALUE, rpb.dtype)], axis=1)
    masked_plane = 2 * NA_WIN_ROWS - 1
    variants, var_ids = [], []
    for kb in range(rows // NA_Q_ROWS):
        r = kb * NA_Q_ROWS + np.arange(NA_Q_ROWS)
        rs = np.clip(r - NA_WIN_ROWS // 2, 0, rows - NA_WIN_ROWS)
        kr = _na_key_row0(kb, rows) + np.arange(NA_K_ROWS)
        ok = (kr[None, :] >= rs[:, None]) & (kr[None, :] < rs[:, None] + NA_WIN_ROWS)
        assert ok.sum() == NA_Q_ROWS * NA_WIN_ROWS
        plane = np.where(ok, kr[None, :] - r[:, None] + NA_WIN_ROWS - 1, masked_plane)
        for vi, known in enumerate(variants):
            if np.array_equal(known, plane):
                var_ids.append(vi)
                break
        else:
            var_ids.append(len(variants))
            variants.append(plane)
    idx = np.stack(variants)
    tbl = planes[:, idx]
    tbl = tbl.transpose(0, 1, 2, 4, 3, 5).reshape(nh, len(variants), NA_Q_ROWS * w, NA_K_ROWS * w)
    return tbl, jnp.asarray(np.array(var_ids, np.int32))


def _na_attn_kernel(var_ref, q_ref, k_ref, v_ref, kc_ref, vc_ref, bias_ref, o_ref, *, rows, scale):
    del var_ref
    nk = NA_K_ROWS * GRID_W
    start = pl.multiple_of(_na_key_row0(pl.program_id(2), rows) * GRID_W, 256)
    kw = k_ref[pl.ds(start, nk), :].astype(BF16)
    vw = v_ref[pl.ds(start, nk), :].astype(BF16)
    q = q_ref[...].astype(BF16)
    s_loc = _nt_dot(q, kw) * scale + bias_ref[...]
    s_ctx = _nt_dot(q, kc_ref[...].astype(BF16)) * scale
    m = jnp.maximum(jnp.max(s_loc, axis=-1, keepdims=True), jnp.max(s_ctx, axis=-1, keepdims=True))
    p_loc = jnp.exp(s_loc - m)
    p_ctx = jnp.exp(s_ctx - m)
    l = jnp.sum(p_loc, axis=-1, keepdims=True) + jnp.sum(p_ctx, axis=-1, keepdims=True)
    o = (jnp.dot(p_loc.astype(BF16), vw, preferred_element_type=F32)
         + jnp.dot(p_ctx.astype(BF16), vc_ref[...].astype(BF16), preferred_element_type=F32))
    o_ref[...] = (o / l).astype(o_ref.dtype)


def _na_attn_call(proj, cache_k, cache_v, layer, rpb, nb, dec_seq, q_col, hd):
    rows = dec_seq // GRID_W
    assert rows >= NA_K_ROWS and rows % NA_Q_ROWS == 0
    nh = NA_HEADS
    nq = NA_Q_ROWS * GRID_W
    nkb = rows // NA_Q_ROWS
    past = cache_k.shape[3]
    tbl, var_ids = _na_bias_tables(rpb, rows)
    ctx_spec = pl.BlockSpec((None, None, None, past, hd), lambda b, h, k, var: (b, layer, h, 0, 0))
    grid_spec = pltpu.PrefetchScalarGridSpec(
        num_scalar_prefetch=1,
        grid=(nb, nh, nkb),
        in_specs=[pl.BlockSpec((nq, hd), lambda b, h, k, var: (b * nkb + k, q_col + h)),
                  pl.BlockSpec((dec_seq, hd), lambda b, h, k, var: (b, q_col + nh + h)),
                  pl.BlockSpec((dec_seq, hd), lambda b, h, k, var: (b, q_col + 2 * nh + h)),
                  ctx_spec, ctx_spec,
                  pl.BlockSpec((None, None, nq, NA_K_ROWS * GRID_W),
                               lambda b, h, k, var: (h, var[k], 0, 0))],
        out_specs=pl.BlockSpec((nq, hd), lambda b, h, k, var: (b * nkb + k, h)),
    )
    return pl.pallas_call(
        functools.partial(_na_attn_kernel, rows=rows, scale=hd ** -0.5),
        grid_spec=grid_spec,
        out_shape=jax.ShapeDtypeStruct((nb * dec_seq, nh * hd), BF16),
        compiler_params=_cparams("parallel", "parallel", "arbitrary"),
        name="neighbourhood_attention",
    )(var_ids, proj, proj, proj, cache_k, cache_v, tbl)


def _merge_kernel(fl_ref, fc_ref, p_ref, al_ref, ac_ref, c_ref, g0_ref, g1_ref, g2_ref, g3_ref, w_ref,
                  o_ref, *, n_lat_tiles):
    is_lat = pl.program_id(0) < n_lat_tiles
    f = jnp.where(is_lat, fl_ref[...], fc_ref[...])
    a = jnp.where(is_lat, al_ref[...], ac_ref[...])
    acc = None
    for n, (br, gt) in enumerate(((f, g0_ref), (p_ref[...], g1_ref), (a, g2_ref), (c_ref[...], g3_ref))):
        y = gt[...].astype(F32) * jnp.dot(br, w_ref[n], preferred_element_type=F32)
        acc = y if acc is None else acc + y
    o_ref[...] = acc.astype(o_ref.dtype)


def _merge_call(f_lat, f_ctx, o_p, a_lat, a_ctx, o_c, gates, w_branch, tm, tn):
    t, bw = o_p.shape
    d = w_branch.shape[2]
    nj = d // tn
    nl = f_lat.shape[0] // tm
    br_spec = pl.BlockSpec((tm, bw), lambda i, j: (i, 0))
    lat_spec = pl.BlockSpec((tm, bw), lambda i, j: (jnp.minimum(i, nl - 1), 0))
    ctx_spec = pl.BlockSpec((tm, bw), lambda i, j: (jnp.maximum(i - nl, 0), 0))
    gate_specs = [pl.BlockSpec((tm, tn), functools.partial(lambda i, j, n: (i, n * nj + j), n=n))
                  for n in range(N_BRANCH)]
    return pl.pallas_call(
        functools.partial(_merge_kernel, n_lat_tiles=nl),
        grid=(t // tm, nj),
        in_specs=[lat_spec, ctx_spec, br_spec, lat_spec, ctx_spec, br_spec] + gate_specs
                 + [pl.BlockSpec((N_BRANCH, bw, tn), lambda i, j: (0, 0, j))],
        out_specs=pl.BlockSpec((tm, tn), lambda i, j: (i, j)),
        out_shape=jax.ShapeDtypeStruct((t, d), BF16),
        compiler_params=_cparams("parallel", "parallel"),
        name="branch_merge",
    )(f_lat, f_ctx, o_p, a_lat, a_ctx, o_c, gates, gates, gates, gates, w_branch)


def _out_kernel(a_ref, w_ref, x_ref, mod_ref, g_ref, *rest, gate_row, next_rows):
    if next_rows is None:
        (o_ref,) = rest
    else:
        nmod_ref, ng_ref, o_ref, h_ref = rest
    k = pl.program_id(1)

    @pl.when(k == 0)
    def _():
        o_ref[...] = jnp.zeros_like(o_ref)

    o_ref[...] += jnp.dot(a_ref[...], w_ref[...], preferred_element_type=F32)

    @pl.when(k == pl.num_programs(1) - 1)
    def _():
        xn = x_ref[...] + mod_ref[gate_row:gate_row + 1, :] * _rms(o_ref[...], g_ref[...])
        o_ref[...] = xn
        if next_rows is not None:
            h_ref[...] = _mod_norm(xn, ng_ref[...], nmod_ref, *next_rows).astype(h_ref.dtype)


def _k_tile(kdim):
    lanes = 128
    return max(t for t in range(lanes, 11 * lanes + 1, lanes) if kdim % t == 0)


def _out_call(a, w, x, mod, g, gate_row, nxt, mod_row, tm, tk):
    t, d = x.shape
    kdim = a.shape[1]
    row = pl.BlockSpec((tm, d), lambda i, k: (i, 0))
    mod_spec = pl.BlockSpec((None, 6, d), lambda i, k: (mod_row(i * tm), 0, 0))
    vec = pl.BlockSpec((1, d), lambda i, k: (0, 0))
    in_specs = [pl.BlockSpec((tm, tk), lambda i, k: (i, k)),
                pl.BlockSpec((tk, d), lambda i, k: (k, 0)),
                row, mod_spec, vec]
    args = [a, w, x, mod, g]
    out_specs, out_shape = row, jax.ShapeDtypeStruct((t, d), F32)
    if nxt is not None:
        in_specs += [mod_spec, vec]
        args += [nxt[0], nxt[1]]
        out_specs = [row, row]
        out_shape = [out_shape, jax.ShapeDtypeStruct((t, d), BF16)]
    return pl.pallas_call(
        functools.partial(_out_kernel, gate_row=gate_row, next_rows=None if nxt is None else nxt[2]),
        grid=(t // tm, kdim // tk),
        in_specs=in_specs,
        out_specs=out_specs,
        out_shape=out_shape,
        compiler_params=_cparams("parallel", "arbitrary", vmem_mib=VMEM_LIMIT_WIDE_MIB),
        name="proj_norm_residual",
    )(*args)


def _ffn_up_kernel(h_ref, wg_ref, wu_ref, o_ref):
    h = h_ref[...]
    y = _silu(jnp.dot(h, wg_ref[...], preferred_element_type=F32)) * jnp.dot(
        h, wu_ref[...], preferred_element_type=F32)
    o_ref[...] = y.astype(o_ref.dtype)


def _ffn_up_call(h, wg, wu, tm, tn):
    t, d = h.shape
    f = wg.shape[1]
    w_spec = pl.BlockSpec((d, tn), lambda i, j: (0, j))
    return pl.pallas_call(
        _ffn_up_kernel,
        grid=(t // tm, f // tn),
        in_specs=[pl.BlockSpec((tm, d), lambda i, j: (i, 0)), w_spec, w_spec],
        out_specs=pl.BlockSpec((tm, tn), lambda i, j: (i, j)),
        out_shape=jax.ShapeDtypeStruct((t, f), BF16),
        compiler_params=_cparams("parallel", "parallel"),
        name="ffn_up",
    )(h, wg, wu)


def _split_bf16(x):
    hi = x.astype(BF16)
    return hi, (x - hi.astype(F32)).astype(BF16)


def _pack_bf16_pairs(h):
    n = h.shape[1] // 2
    hb = h.astype(BF16).astype(F32)
    hi = pltpu.bitcast(hb[:, :n], jnp.int32)
    lo = pltpu.bitcast(hb[:, n:], jnp.int32)
    return hi | lax.shift_right_logical(lo, 16)


def _unpack_bf16_pairs(w):
    hi = pltpu.bitcast(w & jnp.int32(-65536), F32).astype(BF16)
    lo = pltpu.bitcast(w << 16, F32).astype(BF16)
    return hi, lo


def _router_kernel(x_ref, mod_ref, g_ref, w_ref, b_ref, o_ref, hp_ref, *, n_experts):
    h = _mod_norm(x_ref[...], g_ref[...], mod_ref, SCALE2, SHIFT2)
    hp_ref[...] = _pack_bf16_pairs(h)
    h_hi, h_lo = _split_bf16(h)
    w_hi, w_lo = _split_bf16(w_ref[...])
    logits = (jnp.dot(h_hi, w_hi, preferred_element_type=F32)
              + jnp.dot(h_lo, w_hi, preferred_element_type=F32)
              + jnp.dot(h_hi, w_lo, preferred_element_type=F32)) + b_ref[...]
    lane = lax.broadcasted_iota(jnp.int32, logits.shape, 1).astype(F32)
    neg = -jnp.inf
    no_lane = float(logits.shape[1])
    logits = jnp.where(lane < n_experts, logits, neg)
    m1 = jnp.max(logits, axis=-1, keepdims=True)
    i1 = jnp.min(jnp.where(logits == m1, lane, no_lane), axis=-1, keepdims=True)
    rest = jnp.where(lane == i1, neg, logits)
    m2 = jnp.max(rest, axis=-1, keepdims=True)
    i2 = jnp.min(jnp.where(rest == m2, lane, no_lane), axis=-1, keepdims=True)
    e2 = jnp.exp(m2 - m1)
    den = 1.0 + e2
    o_ref[...] = (jnp.where(lane == 0.0, i1, 0.0) + jnp.where(lane == 1.0, i2, 0.0)
                  + jnp.where(lane == 2.0, 1.0 / den, 0.0) + jnp.where(lane == 3.0, e2 / den, 0.0))


def _router_call(x, mod, g, w_router, b_router, mod_row, tm):
    t, d = x.shape
    ne = w_router.shape[1]
    lanes = 128
    w = jnp.zeros((d, lanes), F32).at[:, :ne].set(w_router)
    b = jnp.zeros((1, lanes), F32).at[0, :ne].set(b_router)
    return pl.pallas_call(
        functools.partial(_router_kernel, n_experts=ne),
        grid=(t // tm,),
        in_specs=[pl.BlockSpec((tm, d), lambda i: (i, 0)),
                  pl.BlockSpec((None, 6, d), lambda i: (mod_row(i * tm), 0, 0)),
                  pl.BlockSpec((1, d), lambda i: (0, 0)),
                  pl.BlockSpec((d, lanes), lambda i: (0, 0)),
                  pl.BlockSpec((1, lanes), lambda i: (0, 0))],
        out_specs=[pl.BlockSpec((tm, lanes), lambda i: (i, 0)),
                   pl.BlockSpec((tm, d // 2), lambda i: (i, 0))],
        out_shape=[jax.ShapeDtypeStruct((t, lanes), F32),
                   jax.ShapeDtypeStruct((t, d // 2), jnp.int32)],
        compiler_params=_cparams("parallel"),
        name="router",
    )(x, mod, g, w, b)


def _routing_tables(route, n_experts, tm):
    t = route.shape[0]
    na = TOP_K * t
    p_rows = na + n_experts * tm
    nt = p_rows // tm
    e_flat = route[:, :TOP_K].astype(jnp.int32).reshape(na)
    onehot = (e_flat[:, None] == jnp.arange(n_experts, dtype=jnp.int32)[None, :]).astype(jnp.int32)
    csum = jnp.cumsum(onehot, axis=0)
    rank = jnp.sum((csum - onehot) * onehot, axis=1)
    counts = csum[-1]
    padded = ((counts + tm - 1) // tm) * tm
    ends = jnp.cumsum(padded)
    pos = (ends - padded)[e_flat] + rank
    src = jnp.zeros((p_rows,), jnp.int32).at[pos].set(jnp.arange(na, dtype=jnp.int32) // TOP_K)
    tile_expert = jnp.minimum(
        jnp.searchsorted(ends, jnp.arange(nt, dtype=jnp.int32) * tm, side="right"), n_experts - 1)
    meta = jnp.concatenate([tile_expert.astype(jnp.int32), (ends[-1:] // tm).astype(jnp.int32)])
    return src, pos.astype(jnp.int32), meta


def _gather_rows_kernel(idx_ref, src_ref, o_ref, buf_ref, sem_ref, *, rows):
    i = pl.program_id(0)
    n = pl.num_programs(0)

    def issue(tile, slot):
        def body(r, carry):
            pltpu.make_async_copy(src_ref.at[pl.ds(idx_ref[tile * rows + r], 1)],
                                  buf_ref.at[slot, pl.ds(r, 1)], sem_ref.at[slot]).start()
            return carry
        lax.fori_loop(0, rows, body, 0, unroll=DMA_ISSUE_UNROLL)

    @pl.when(i == 0)
    def _():
        issue(0, 0)

    @pl.when(i + 1 < n)
    def _():
        issue(i + 1, (i + 1) % 2)

    slot = i % 2
    pltpu.make_async_copy(src_ref.at[pl.ds(0, rows)], buf_ref.at[slot], sem_ref.at[slot]).wait()
    half = buf_ref.shape[2]
    hi, lo = _unpack_bf16_pairs(buf_ref[slot])
    o_ref[:, :half] = hi
    o_ref[:, half:] = lo


def _gather_rows_call(src, idx, rows):
    p_rows = idx.shape[0]
    half = src.shape[1]
    grid_spec = pltpu.PrefetchScalarGridSpec(
        num_scalar_prefetch=1,
        grid=(p_rows // rows,),
        in_specs=[pl.BlockSpec(memory_space=pl.ANY)],
        out_specs=pl.BlockSpec((rows, 2 * half), lambda i, idx: (i, 0)),
        scratch_shapes=[pltpu.VMEM((2, rows, half), src.dtype), pltpu.SemaphoreType.DMA((2,))],
    )
    return pl.pallas_call(
        functools.partial(_gather_rows_kernel, rows=rows),
        grid_spec=grid_spec,
        out_shape=jax.ShapeDtypeStruct((p_rows, 2 * half), BF16),
        compiler_params=_cparams("arbitrary"),
        name="moe_gather",
    )(idx, src)


def _moe_up_kernel(meta_ref, xs_ref, wg_ref, wu_ref, o_ref, wgb_ref, wub_ref, *, n_tiles):
    i = pl.program_id(1)
    used = i < meta_ref[n_tiles]
    fresh = jnp.logical_or(i == 0, meta_ref[i] != meta_ref[jnp.maximum(i - 1, 0)])

    @pl.when(jnp.logical_and(used, fresh))
    def _():
        wgb_ref[...] = wg_ref[...].astype(BF16)
        wub_ref[...] = wu_ref[...].astype(BF16)

    @pl.when(used)
    def _():
        h = xs_ref[...]
        y = _silu(jnp.dot(h, wgb_ref[...], preferred_element_type=F32)) * jnp.dot(
            h, wub_ref[...], preferred_element_type=F32)
        o_ref[...] = y.astype(o_ref.dtype)

    @pl.when(jnp.logical_not(used))
    def _():
        o_ref[...] = jnp.zeros_like(o_ref)


def _moe_up_call(xs, meta, wg, wu, tm, tn):
    p_rows, d = xs.shape
    f = wg.shape[2]
    nt = p_rows // tm
    w_spec = pl.BlockSpec((None, d, tn), lambda j, i, meta: (meta[i], 0, j))
    grid_spec = pltpu.PrefetchScalarGridSpec(
        num_scalar_prefetch=1,
        grid=(f // tn, nt),
        in_specs=[pl.BlockSpec((tm, d), lambda j, i, meta: (i, 0)), w_spec, w_spec],
        out_specs=pl.BlockSpec((None, tm, tn), lambda j, i, meta: (j, i, 0)),
        scratch_shapes=[pltpu.VMEM((d, tn), BF16), pltpu.VMEM((d, tn), BF16)],
    )
    return pl.pallas_call(
        functools.partial(_moe_up_kernel, n_tiles=nt),
        grid_spec=grid_spec,
        out_shape=jax.ShapeDtypeStruct((f // tn, p_rows, tn), BF16),
        compiler_params=_cparams("arbitrary", "arbitrary"),
        name="moe_up",
    )(meta, xs, wg, wu)


def _moe_down_kernel(meta_ref, a_ref, w_ref, o_ref, *, n_tiles):
    i = pl.program_id(0)

    @pl.when(pl.program_id(1) == 0)
    def _():
        o_ref[...] = jnp.zeros_like(o_ref)

    @pl.when(i < meta_ref[n_tiles])
    def _():
        nsub, _, tk = a_ref.shape
        acc = o_ref[...]
        for s in range(nsub):
            acc = acc + jnp.dot(a_ref[s], w_ref[s * tk:(s + 1) * tk, :], preferred_element_type=F32)
        o_ref[...] = acc


def _moe_down_call(a, meta, wd, tm, nsub):
    nk, p_rows, tk = a.shape
    d = wd.shape[2]
    nt = p_rows // tm
    grid_spec = pltpu.PrefetchScalarGridSpec(
        num_scalar_prefetch=1,
        grid=(nt, nk // nsub),
        in_specs=[pl.BlockSpec((nsub, tm, tk), lambda i, k, meta: (k, i, 0)),
                  pl.BlockSpec((None, nsub * tk, d), lambda i, k, meta: (meta[i], k, 0))],
        out_specs=pl.BlockSpec((tm, d), lambda i, k, meta: (i, 0)),
    )
    return pl.pallas_call(
        functools.partial(_moe_down_kernel, n_tiles=nt),
        grid_spec=grid_spec,
        out_shape=jax.ShapeDtypeStruct((p_rows, d), F32),
        compiler_params=_cparams("parallel", "arbitrary"),
        name="moe_down",
    )(meta, a, wd)


def _moe_combine_kernel(pos_ref, ys_ref, route_ref, x_ref, mod_ref, g_ref, *rest, rows, n_lat_tiles):
    if n_lat_tiles is None:
        o_ref, buf_ref, sem_ref = rest
    else:
        lat_ref, ctx_ref, buf_ref, sem_ref = rest
    i = pl.program_id(0)
    n = pl.num_programs(0)

    def issue(tile, slot):
        def body(r, carry):
            for s in range(TOP_K):
                pltpu.make_async_copy(ys_ref.at[pl.ds(pos_ref[TOP_K * (tile * rows + r) + s], 1)],
                                      buf_ref.at[slot, s, pl.ds(r, 1)], sem_ref.at[slot]).start()
            return carry
        lax.fori_loop(0, rows, body, 0, unroll=DMA_ISSUE_UNROLL)

    @pl.when(i == 0)
    def _():
        issue(0, 0)

    @pl.when(i + 1 < n)
    def _():
        issue(i + 1, (i + 1) % 2)

    slot = i % 2
    for s in range(TOP_K):
        pltpu.make_async_copy(ys_ref.at[pl.ds(0, rows)], buf_ref.at[slot, s], sem_ref.at[slot]).wait()
    route = route_ref[...]
    y = None
    for s in range(TOP_K):
        term = route[:, TOP_K + s:TOP_K + s + 1] * buf_ref[slot, s]
        y = term if y is None else y + term
    res = x_ref[...] + mod_ref[GATE2:GATE2 + 1, :] * _rms(y, g_ref[...])
    if n_lat_tiles is None:
        o_ref[...] = res
    else:
        @pl.when(i < n_lat_tiles)
        def _():
            lat_ref[...] = res

        @pl.when(i >= n_lat_tiles)
        def _():
            ctx_ref[...] = res


def _moe_combine_call(ys, pos, route, x, mod, g, mod_row, rows, split_rows):
    t, d = x.shape
    row = pl.BlockSpec((rows, d), lambda i, pos: (i, 0))
    if split_rows is None:
        nl = None
        out_specs, out_shape = row, jax.ShapeDtypeStruct((t, d), F32)
    else:
        nl = split_rows // rows
        out_specs = [pl.BlockSpec((rows, d), lambda i, pos: (jnp.minimum(i, nl - 1), 0)),
                     pl.BlockSpec((rows, d), lambda i, pos: (jnp.maximum(i - nl, 0), 0))]
        out_shape = [jax.ShapeDtypeStruct((split_rows, d), F32),
                     jax.ShapeDtypeStruct((t - split_rows, d), F32)]
    grid_spec = pltpu.PrefetchScalarGridSpec(
        num_scalar_prefetch=1,
        grid=(t // rows,),
        in_specs=[pl.BlockSpec(memory_space=pl.ANY),
                  pl.BlockSpec((rows, route.shape[1]), lambda i, pos: (i, 0)),
                  row,
                  pl.BlockSpec((None, 6, d), lambda i, pos: (mod_row(i * rows), 0, 0)),
                  pl.BlockSpec((1, d), lambda i, pos: (0, 0))],
        out_specs=out_specs,
        scratch_shapes=[pltpu.VMEM((2, TOP_K, rows, d), F32), pltpu.SemaphoreType.DMA((2,))],
    )
    return pl.pallas_call(
        functools.partial(_moe_combine_kernel, rows=rows, n_lat_tiles=nl),
        grid_spec=grid_spec,
        out_shape=out_shape,
        compiler_params=_cparams("arbitrary"),
        name="moe_combine",
    )(pos, ys, route, x, mod, g)


def kernel(x_prompt, x_sample, cache_k, cache_v, c, c_ctx, w_ada, b_ada, norm_g, w_in, b_in, pool_w, pool_scale, rpb, conv_w, conv_b, conv_ln_g, conv_ln_b, w_branch, w_out, w_gate_d, w_up_d, w_down_d, w_router, b_router, w_gate_e, w_up_e, w_down_e):
    nbp, seq, d = x_prompt.shape
    nbs, dec_seq, _ = x_sample.shape
    depth = w_ada.shape[0]
    bw = d // N_BRANCH
    hd = bw // NA_HEADS
    ns_rows = nbs * dec_seq
    np_rows = nbp * seq
    assert dec_seq & (dec_seq - 1) == 0 and seq & (seq - 1) == 0
    assert nbs + 1 <= MOD_ROWS

    t_rows = ns_rows + np_rows
    tm_wide = max(tm for tm in (2048, 1024, 512) if dec_seq % tm == 0 and np_rows % tm == 0)
    tm_mid = min(tm_wide, 1024)
    tm_emit = 512
    tm_moe = 512
    tile_seq = min(256, seq)
    tn = 512
    assert t_rows % tm_wide == 0 and seq % tile_seq == 0

    def mod_row(row0):
        return jnp.where(row0 < ns_rows, row0 // dec_seq, nbs)

    x = jnp.concatenate([x_sample.reshape(ns_rows, d), x_prompt.reshape(np_rows, d)], axis=0)
    cvec = jnp.zeros((MOD_ROWS, d), F32).at[:nbs].set(c).at[nbs].set(c_ctx)
    mods = _ada_call(cvec, w_ada, b_ada).reshape(depth, MOD_ROWS, 6, d)
    cc, sc = _channel_dft_mats(bw, FNET_GROUPS)

    pool_col, conv_col, main_cols = 1, 5, 7 * bw
    q_col = 2 * bw // hd

    def gain(l, n):
        return norm_g[l, n][None, :]

    h = _norm_call(x, mods[0], gain(0, 0), SCALE1, SHIFT1, mod_row, tm_mid)
    new_k, new_v = [], []
    y_split = None
    for l in range(depth):
        mod = mods[l]
        last = l + 1 == depth
        proj, gates = _in_call(h, w_in[l].astype(BF16), b_in[l][None, :], main_cols, tm_wide, tn)

        f_lat = _fourier_call(proj, 0, nbs, dec_seq, bw, cc, sc)
        f_ctx = _fourier_call(proj, ns_rows, nbp, seq, bw, cc, sc)
        o_p = _pool_call(proj, pool_w[l].astype(BF16), pool_scale[l][None, :], pool_col, bw,
                         tile_seq, ns_rows, dec_seq, seq)
        o_c = _conv_call(proj, conv_w[l], conv_b[l][None, :], conv_ln_g[l][None, :],
                         conv_ln_b[l][None, :], conv_col, bw, tile_seq, ns_rows, dec_seq, seq)
        a_lat = _na_attn_call(proj, cache_k, cache_v, l, rpb[l], nbs, dec_seq, q_col, hd)
        a_ctx, k_ctx, v_ctx = _ctx_attn_call(proj, ns_rows, nbp, seq, q_col, hd)
        new_k.append(k_ctx)
        new_v.append(v_ctx)

        merged = _merge_call(f_lat, f_ctx, o_p, a_lat, a_ctx, o_c, gates, w_branch[l].astype(BF16),
                             tm_mid, tn)
        nxt = None if last else (mods[l + 1], gain(l + 1, 0), (SCALE1, SHIFT1))
        i = l // 2
        if l % 2 == 1:
            x = _out_call(merged, w_out[l].astype(BF16), x, mod, gain(l, 1), GATE1, None,
                          mod_row, tm_mid, tn)
            ne = w_router.shape[2]
            route, h_packed = _router_call(x, mod, gain(l, 2), w_router[i], b_router[i], mod_row, 512)
            src, pos, meta = _routing_tables(route, ne, tm_moe)
            xs = _gather_rows_call(h_packed, src, 256)
            hmid = _moe_up_call(xs, meta, w_gate_e[i], w_up_e[i], tm_moe, tn)
            ys = _moe_down_call(hmid, meta, w_down_e[i].astype(BF16), tm_moe, 2)
            out = _moe_combine_call(ys, pos, route, x, mod, gain(l, 3), mod_row, 256,
                                    ns_rows if last else None)
            if last:
                y_split = out
            else:
                x = out
                h = _norm_call(x, nxt[0], nxt[1], *nxt[2], mod_row, tm_mid)
        else:
            x, h2 = _out_call(merged, w_out[l].astype(BF16), x, mod, gain(l, 1), GATE1,
                              (mod, gain(l, 2), (SCALE2, SHIFT2)), mod_row, tm_emit, _k_tile(d))
            hmid = _ffn_up_call(h2, w_gate_d[i].astype(BF16), w_up_d[i].astype(BF16), tm_wide, tn)
            if last:
                x = _out_call(hmid, w_down_d[i].astype(BF16), x, mod, gain(l, 3), GATE2, None,
                              mod_row, tm_mid, tn)
            else:
                x, h = _out_call(hmid, w_down_d[i].astype(BF16), x, mod, gain(l, 3), GATE2, nxt,
                                 mod_row, tm_emit, _k_tile(hmid.shape[1]))

    if y_split is None:
        y_split = x[:ns_rows], x[ns_rows:]
    y_sample = y_split[0].reshape(nbs, dec_seq, d)
    y_prompt = y_split[1].reshape(nbp, seq, d)
    return y_prompt, y_sample, jnp.stack(new_k, axis=1), jnp.stack(new_v, axis=1)
```

```python
import functools

import numpy as np
import jax
import jax.numpy as jnp
from jax import lax
from jax.experimental import pallas as pl
from jax.experimental.pallas import tpu as pltpu

F32 = jnp.float32
BF16 = jnp.bfloat16

N_BRANCH = 4
FNET_GROUPS = 4
POOL_WINDOWS = (2, 4, 8, 16)
NA_HEADS = 4
NA_WIN_ROWS = 8
NA_WIN_COLS = 16
GRID_W = 64
CONV_K = 31
TOP_K = 2
RMS_EPS = 1e-6
LN_EPS = 1e-5
MASK_VALUE = -1e30
MOD_ROWS = 8
POOL_HALO = 8
CONV_HALO = 16
NA_Q_ROWS = 8
NA_K_ROWS = 16
DMA_ISSUE_UNROLL = 8
VMEM_LIMIT_MIB = 48
VMEM_LIMIT_WIDE_MIB = 56

SHIFT1, SCALE1, GATE1, SHIFT2, SCALE2, GATE2 = range(6)


def _cparams(*sem, vmem_mib=VMEM_LIMIT_MIB):
    return pltpu.CompilerParams(dimension_semantics=sem, vmem_limit_bytes=vmem_mib * 1024 * 1024)


def _silu(x):
    return x * jax.nn.sigmoid(x)


def _rms(x, g):
    return x * lax.rsqrt(jnp.mean(x * x, axis=-1, keepdims=True) + RMS_EPS) * g


def _mod_norm(x, g, mod_ref, scale_row, shift_row):
    return (_rms(x, g) * (1.0 + mod_ref[scale_row:scale_row + 1, :])
            + mod_ref[shift_row:shift_row + 1, :])


def _nt_dot(a, b):
    return lax.dot_general(a, b, (((1,), (1,)), ((), ())), preferred_element_type=F32)


def _ada_kernel(c_ref, w_ref, b_ref, o_ref):
    s = _silu(c_ref[...]).astype(BF16)
    o_ref[...] = jnp.dot(s, w_ref[...].astype(BF16), preferred_element_type=F32) + b_ref[...]


def _ada_call(cvec, w_ada, b_ada):
    depth, d, n = w_ada.shape
    tn = 512
    return pl.pallas_call(
        _ada_kernel,
        grid=(depth, n // tn),
        in_specs=[pl.BlockSpec((MOD_ROWS, d), lambda l, j: (0, 0)),
                  pl.BlockSpec((None, d, tn), lambda l, j: (l, 0, j)),
                  pl.BlockSpec((None, 1, tn), lambda l, j: (l, 0, j))],
        out_specs=pl.BlockSpec((None, MOD_ROWS, tn), lambda l, j: (l, 0, j)),
        out_shape=jax.ShapeDtypeStruct((depth, MOD_ROWS, n), F32),
        compiler_params=_cparams("parallel", "parallel"),
        name="adaln",
    )(cvec, w_ada, b_ada.reshape(depth, 1, n))


def _norm_kernel(x_ref, mod_ref, g_ref, o_ref, *, scale_row, shift_row):
    o_ref[...] = _mod_norm(x_ref[...], g_ref[...], mod_ref, scale_row, shift_row).astype(o_ref.dtype)


def _norm_call(x, mod, g, scale_row, shift_row, mod_row, tm):
    t, d = x.shape
    return pl.pallas_call(
        functools.partial(_norm_kernel, scale_row=scale_row, shift_row=shift_row),
        grid=(t // tm,),
        in_specs=[pl.BlockSpec((tm, d), lambda i: (i, 0)),
                  pl.BlockSpec((None, 6, d), lambda i: (mod_row(i * tm), 0, 0)),
                  pl.BlockSpec((1, d), lambda i: (0, 0))],
        out_specs=pl.BlockSpec((tm, d), lambda i: (i, 0)),
        out_shape=jax.ShapeDtypeStruct((t, d), BF16),
        compiler_params=_cparams("parallel"),
        name="mod_norm",
    )(x, mod, g)


def _in_kernel(h_ref, w_ref, b_ref, main_ref, gate_ref, *, n_main):
    j = pl.program_id(1)
    y = jnp.dot(h_ref[...], w_ref[...], preferred_element_type=F32) + b_ref[...]

    @pl.when(j < n_main)
    def _():
        main_ref[...] = y

    @pl.when(j >= n_main)
    def _():
        gate_ref[...] = jax.nn.sigmoid(y).astype(gate_ref.dtype)


def _in_call(h, w, b, main_cols, tm, tn):
    t, d = h.shape
    n = w.shape[1]
    n_main = main_cols // tn
    return pl.pallas_call(
        functools.partial(_in_kernel, n_main=n_main),
        grid=(t // tm, n // tn),
        in_specs=[pl.BlockSpec((tm, d), lambda i, j: (i, 0)),
                  pl.BlockSpec((d, tn), lambda i, j: (0, j)),
                  pl.BlockSpec((1, tn), lambda i, j: (0, j))],
        out_specs=[pl.BlockSpec((tm, tn), lambda i, j: (i, jnp.minimum(j, n_main - 1))),
                   pl.BlockSpec((tm, tn), lambda i, j: (i, jnp.maximum(j - n_main, 0)))],
        out_shape=[jax.ShapeDtypeStruct((t, main_cols), F32),
                   jax.ShapeDtypeStruct((t, n - main_cols), BF16)],
        compiler_params=_cparams("parallel", "arbitrary"),
        name="in_proj",
    )(h, w, b)


def _dft_mats(n):
    scale = 1.0 / np.sqrt(n)
    j = jnp.arange(n, dtype=jnp.int32)
    if n <= 1024:
        ang = ((j[:, None] * j[None, :]) % n).astype(F32) * (2.0 * np.pi / n)
        return (jnp.cos(ang) * scale).astype(BF16), (jnp.sin(ang) * scale).astype(BF16)
    base = 64
    hi = n // base
    k1 = jnp.arange(hi, dtype=jnp.int32)
    k0 = jnp.arange(base, dtype=jnp.int32)
    a = ((j[:, None] * k1[None, :]) % hi).astype(F32) * (2.0 * np.pi / hi)
    b = ((j[:, None] * k0[None, :]) % n).astype(F32) * (2.0 * np.pi / n)
    ca, sa = jnp.cos(a)[:, :, None], jnp.sin(a)[:, :, None]
    cb, sb = jnp.cos(b)[:, None, :] * scale, jnp.sin(b)[:, None, :] * scale
    c = (ca * cb - sa * sb).reshape(n, n)
    s = (sa * cb + ca * sb).reshape(n, n)
    return c.astype(BF16), s.astype(BF16)


def _channel_dft_mats(width, groups):
    gw = width // groups
    k = np.arange(gw)
    ang = 2.0 * np.pi * ((k[:, None] * k[None, :]) % gw) / gw
    c = np.zeros((width, width), np.float32)
    s = np.zeros((width, width), np.float32)
    for g in range(groups):
        sl = slice(g * gw, (g + 1) * gw)
        c[sl, sl] = np.cos(ang) / np.sqrt(gw)
        s[sl, sl] = np.sin(ang) / np.sqrt(gw)
    return jnp.asarray(c, BF16), jnp.asarray(s, BF16)


def _fnet1_kernel(u_ref, cc_ref, sc_ref, vc_ref, vs_ref):
    u = u_ref[...].astype(BF16)
    vc_ref[...] = jnp.dot(u, cc_ref[...], preferred_element_type=F32).astype(BF16)
    vs_ref[...] = jnp.dot(u, sc_ref[...], preferred_element_type=F32).astype(BF16)


def _fnet2_kernel(c_ref, s_ref, vc_ref, vs_ref, o_ref, acc_ref):
    k = pl.program_id(2)

    @pl.when(k == 0)
    def _():
        acc_ref[...] = jnp.zeros_like(acc_ref)

    acc_ref[...] += (jnp.dot(c_ref[...], vc_ref[...], preferred_element_type=F32)
                     - jnp.dot(s_ref[...], vs_ref[...], preferred_element_type=F32))

    @pl.when(k == pl.num_programs(2) - 1)
    def _():
        o_ref[...] = acc_ref[...].astype(o_ref.dtype)


def _fourier_call(proj, row_base, nb, seq, bw, cc, sc):
    c_l, s_l = _dft_mats(seq)
    tm1 = min(seq, 512)
    nt1 = seq // tm1
    base1 = row_base // tm1
    vc, vs = pl.pallas_call(
        _fnet1_kernel,
        grid=(nb, nt1),
        in_specs=[pl.BlockSpec((tm1, bw), lambda b, t: (base1 + b * nt1 + t, 0)),
                  pl.BlockSpec((bw, bw), lambda b, t: (0, 0)),
                  pl.BlockSpec((bw, bw), lambda b, t: (0, 0))],
        out_specs=[pl.BlockSpec((tm1, bw), lambda b, t: (t, b)),
                   pl.BlockSpec((tm1, bw), lambda b, t: (t, b))],
        out_shape=[jax.ShapeDtypeStruct((seq, nb * bw), BF16)] * 2,
        compiler_params=_cparams("parallel", "parallel"),
        name="fnet_channels",
    )(proj, cc, sc)
    tm2 = min(seq, 1024)
    tk = min(seq, 2048)
    nt2 = seq // tm2
    return pl.pallas_call(
        _fnet2_kernel,
        grid=(nt2, nb, seq // tk),
        in_specs=[pl.BlockSpec((tm2, tk), lambda i, j, k: (i, k)),
                  pl.BlockSpec((tm2, tk), lambda i, j, k: (i, k)),
                  pl.BlockSpec((tk, bw), lambda i, j, k: (k, j)),
                  pl.BlockSpec((tk, bw), lambda i, j, k: (k, j))],
        out_specs=pl.BlockSpec((tm2, bw), lambda i, j, k: (j * nt2 + i, 0)),
        out_shape=jax.ShapeDtypeStruct((nb * seq, bw), BF16),
        scratch_shapes=[pltpu.VMEM((tm2, bw), F32)],
        compiler_params=_cparams("parallel", "parallel", "arbitrary"),
        name="fnet_positions",
    )(c_l, s_l, vc, vs)


def _seq_position(row0, ns_rows, dec_seq, seq):
    is_lat = row0 < ns_rows
    seq_len = jnp.where(is_lat, dec_seq, seq)
    pos0 = jnp.where(is_lat, row0 & (dec_seq - 1), row0 & (seq - 1))
    return seq_len, pos0


def _pool_kernel(prev_ref, cur_ref, nxt_ref, w_ref, sc_ref, o_ref, ext_ref, *,
                 tm, ns_rows, dec_seq, seq):
    seq_len, pos0 = _seq_position(pl.program_id(0) * tm, ns_rows, dec_seq, seq)
    h = POOL_HALO
    ext_ref[0:h, :] = jnp.where(pos0 == 0, 0.0, prev_ref[...])
    ext_ref[h:h + tm, :] = cur_ref[...]
    ext_ref[h + tm:2 * h + tm, :] = jnp.where(pos0 + tm == seq_len, 0.0, nxt_ref[...])
    t = pos0 + lax.broadcasted_iota(jnp.int32, (tm, 1), 0)
    gw = cur_ref.shape[1] // len(POOL_WINDOWS)
    for gi, win in enumerate(POOL_WINDOWS):
        lo = win // 2
        hi = win - lo
        cols = slice(gi * gw, (gi + 1) * gw)
        s = ext_ref[h - lo:h - lo + tm, cols]
        for j in range(1 - lo, hi):
            s = s + ext_ref[h + j:h + j + tm, cols]
        cnt = (jnp.minimum(t + hi, seq_len) - jnp.maximum(t - lo, 0)).astype(F32)
        dlt = s / cnt - cur_ref[:, cols]
        y = jnp.dot(dlt.astype(BF16), w_ref[gi], preferred_element_type=F32) * sc_ref[:, cols]
        o_ref[:, cols] = y.astype(o_ref.dtype)


def _pool_call(proj, pool_w, pool_scale, col_blk, bw, tm, ns_rows, dec_seq, seq):
    t = proj.shape[0]
    h = POOL_HALO
    r = tm // h
    last = t // h - 1
    gw = bw // len(POOL_WINDOWS)
    return pl.pallas_call(
        functools.partial(_pool_kernel, tm=tm, ns_rows=ns_rows, dec_seq=dec_seq, seq=seq),
        grid=(t // tm,),
        in_specs=[pl.BlockSpec((h, bw), lambda i: (jnp.maximum(i * r - 1, 0), col_blk)),
                  pl.BlockSpec((tm, bw), lambda i: (i, col_blk)),
                  pl.BlockSpec((h, bw), lambda i: (jnp.minimum((i + 1) * r, last), col_blk)),
                  pl.BlockSpec((len(POOL_WINDOWS), gw, gw), lambda i: (0, 0, 0)),
                  pl.BlockSpec((1, bw), lambda i: (0, 0))],
        out_specs=pl.BlockSpec((tm, bw), lambda i: (i, 0)),
        out_shape=jax.ShapeDtypeStruct((t, bw), BF16),
        scratch_shapes=[pltpu.VMEM((tm + 2 * h, bw), F32)],
        compiler_params=_cparams("parallel"),
        name="pool_mix",
    )(proj, proj, proj, pool_w, pool_scale)


def _conv_kernel(ap_ref, ac_ref, an_ref, gp_ref, gc_ref, gn_ref, w_ref, b_ref, lg_ref, lb_ref,
                 o_ref, ext_ref, y_ref, *, tm, ns_rows, dec_seq, seq):
    seq_len, pos0 = _seq_position(pl.program_id(0) * tm, ns_rows, dec_seq, seq)
    h = CONV_HALO
    ext_ref[0:h, :] = jnp.where(pos0 == 0, 0.0, ap_ref[...] * jax.nn.sigmoid(gp_ref[...]))
    ext_ref[h:h + tm, :] = ac_ref[...] * jax.nn.sigmoid(gc_ref[...])
    ext_ref[h + tm:2 * h + tm, :] = jnp.where(pos0 + tm == seq_len, 0.0,
                                              an_ref[...] * jax.nn.sigmoid(gn_ref[...]))
    bw = ac_ref.shape[1]
    lanes = 128
    for c in range(bw // lanes):
        cols = slice(c * lanes, (c + 1) * lanes)
        acc = jnp.zeros((tm, lanes), F32)
        for k in range(CONV_K):
            off = h + k - CONV_K // 2
            acc = acc + ext_ref[off:off + tm, cols] * w_ref[k:k + 1, cols]
        y_ref[:, cols] = acc + b_ref[:, cols]
    y = y_ref[...]
    mu = jnp.mean(y, axis=-1, keepdims=True)
    var = jnp.mean(jnp.square(y - mu), axis=-1, keepdims=True)
    z = (y - mu) * lax.rsqrt(var + LN_EPS) * lg_ref[...] + lb_ref[...]
    o_ref[...] = _silu(z).astype(o_ref.dtype)


def _conv_call(proj, conv_w, conv_b, ln_g, ln_b, col_blk, bw, tm, ns_rows, dec_seq, seq):
    t = proj.shape[0]
    h = CONV_HALO
    r = tm // h
    last = t // h - 1
    prev = lambda i: jnp.maximum(i * r - 1, 0)
    nxt = lambda i: jnp.minimum((i + 1) * r, last)
    vec = pl.BlockSpec((1, bw), lambda i: (0, 0))
    return pl.pallas_call(
        functools.partial(_conv_kernel, tm=tm, ns_rows=ns_rows, dec_seq=dec_seq, seq=seq),
        grid=(t // tm,),
        in_specs=[pl.BlockSpec((h, bw), lambda i: (prev(i), col_blk)),
                  pl.BlockSpec((tm, bw), lambda i: (i, col_blk)),
                  pl.BlockSpec((h, bw), lambda i: (nxt(i), col_blk)),
                  pl.BlockSpec((h, bw), lambda i: (prev(i), col_blk + 1)),
                  pl.BlockSpec((tm, bw), lambda i: (i, col_blk + 1)),
                  pl.BlockSpec((h, bw), lambda i: (nxt(i), col_blk + 1)),
                  pl.BlockSpec((CONV_K, bw), lambda i: (0, 0)),
                  vec, vec, vec],
        out_specs=pl.BlockSpec((tm, bw), lambda i: (i, 0)),
        out_shape=jax.ShapeDtypeStruct((t, bw), BF16),
        scratch_shapes=[pltpu.VMEM((tm + 2 * h, bw), F32), pltpu.VMEM((tm, bw), F32)],
        compiler_params=_cparams("parallel"),
        name="conv_module",
    )(proj, proj, proj, proj, proj, proj, conv_w, conv_b, ln_g, ln_b)


def _ctx_attn_kernel(q_ref, k_ref, v_ref, o_ref, ko_ref, vo_ref, *, scale):
    k = k_ref[...]
    v = v_ref[...]
    s = _nt_dot(q_ref[...].astype(BF16), k.astype(BF16)) * scale
    p = jnp.exp(s - jnp.max(s, axis=-1, keepdims=True))
    l = jnp.sum(p, axis=-1, keepdims=True)
    o = jnp.dot(p.astype(BF16), v.astype(BF16), preferred_element_type=F32)
    o_ref[...] = (o / l).astype(o_ref.dtype)
    ko_ref[...] = k
    vo_ref[...] = v


def _ctx_attn_call(proj, row_base, nb, seq, q_col, hd):
    base = row_base // seq
    nh = NA_HEADS
    kv_spec = pl.BlockSpec((None, None, seq, hd), lambda b, h: (b, h, 0, 0))
    return pl.pallas_call(
        functools.partial(_ctx_attn_kernel, scale=hd ** -0.5),
        grid=(nb, nh),
        in_specs=[pl.BlockSpec((seq, hd), lambda b, h: (base + b, q_col + h)),
                  pl.BlockSpec((seq, hd), lambda b, h: (base + b, q_col + nh + h)),
                  pl.BlockSpec((seq, hd), lambda b, h: (base + b, q_col + 2 * nh + h))],
        out_specs=[pl.BlockSpec((seq, hd), lambda b, h: (b, h)), kv_spec, kv_spec],
        out_shape=[jax.ShapeDtypeStruct((nb * seq, nh * hd), BF16),
                   jax.ShapeDtypeStruct((nb, nh, seq, hd), F32),
                   jax.ShapeDtypeStruct((nb, nh, seq, hd), F32)],
        compiler_params=_cparams("parallel", "parallel"),
        name="context_attention",
    )(proj, proj, proj)


def _na_key_row0(kb, rows):
    lo = kb * NA_Q_ROWS - NA_WIN_ROWS // 2
    if isinstance(kb, (int, np.integer)):
        return int(np.clip(lo, 0, rows - NA_K_ROWS))
    return jnp.clip(lo, 0, rows - NA_K_ROWS)


def _na_bias_tables(rpb, rows):
    nh = rpb.shape[0]
    w = GRID_W
    c = np.arange(w)
    cs = np.clip(c - NA_WIN_COLS // 2, 0, w - NA_WIN_COLS)
    col_ok = (c[None, :] >= cs[:, None]) & (c[None, :] < cs[:, None] + NA_WIN_COLS)
    rel_col = np.clip(c[None, :] - c[:, None] + NA_WIN_COLS - 1, 0, 2 * NA_WIN_COLS - 2)
    planes = jnp.where(col_ok[None, None], rpb[:, :, rel_col], MASK_VALUE)
    planes = jnp.concatenate([planes, jnp.full((nh, 1, w, w), MASK_VALUE, rpb.dtype)], axis=1)
    masked_plane = 2 * NA_WIN_ROWS - 1
    variants, var_ids = [], []
    for kb in range(rows // NA_Q_ROWS):
        r = kb * NA_Q_ROWS + np.arange(NA_Q_ROWS)
        rs = np.clip(r - NA_WIN_ROWS // 2, 0, rows - NA_WIN_ROWS)
        kr = _na_key_row0(kb, rows) + np.arange(NA_K_ROWS)
        ok = (kr[None, :] >= rs[:, None]) & (kr[None, :] < rs[:, None] + NA_WIN_ROWS)
        assert ok.sum() == NA_Q_ROWS * NA_WIN_ROWS
        plane = np.where(ok, kr[None, :] - r[:, None] + NA_WIN_ROWS - 1, masked_plane)
        for vi, known in enumerate(variants):
            if np.array_equal(known, plane):
                var_ids.append(vi)
                break
        else:
            var_ids.append(len(variants))
            variants.append(plane)
    idx = np.stack(variants)
    tbl = planes[:, idx]
    tbl = tbl.transpose(0, 1, 2, 4, 3, 5).reshape(nh, len(variants), NA_Q_ROWS * w, NA_K_ROWS * w)
    return tbl, jnp.asarray(np.array(var_ids, np.int32))


def _na_attn_kernel(var_ref, q_ref, k_ref, v_ref, kc_ref, vc_ref, bias_ref, o_ref, *, rows, scale):
    del var_ref
    nk = NA_K_ROWS * GRID_W
    start = pl.multiple_of(_na_key_row0(pl.program_id(2), rows) * GRID_W, 256)
    kw = k_ref[pl.ds(start, nk), :].astype(BF16)
    vw = v_ref[pl.ds(start, nk), :].astype(BF16)
    q = q_ref[...].astype(BF16)
    s_loc = _nt_dot(q, kw) * scale + bias_ref[...]
    s_ctx = _nt_dot(q, kc_ref[...].astype(BF16)) * scale
    m = jnp.maximum(jnp.max(s_loc, axis=-1, keepdims=True), jnp.max(s_ctx, axis=-1, keepdims=True))
    p_loc = jnp.exp(s_loc - m)
    p_ctx = jnp.exp(s_ctx - m)
    l = jnp.sum(p_loc, axis=-1, keepdims=True) + jnp.sum(p_ctx, axis=-1, keepdims=True)
    o = (jnp.dot(p_loc.astype(BF16), vw, preferred_element_type=F32)
         + jnp.dot(p_ctx.astype(BF16), vc_ref[...].astype(BF16), preferred_element_type=F32))
    o_ref[...] = (o / l).astype(o_ref.dtype)


def _na_attn_call(proj, cache_k, cache_v, layer, rpb, nb, dec_seq, q_col, hd):
    rows = dec_seq // GRID_W
    assert rows >= NA_K_ROWS and rows % NA_Q_ROWS == 0
    nh = NA_HEADS
    nq = NA_Q_ROWS * GRID_W
    nkb = rows // NA_Q_ROWS
    past = cache_k.shape[3]
    tbl, var_ids = _na_bias_tables(rpb, rows)
    ctx_spec = pl.BlockSpec((None, None, None, past, hd), lambda b, h, k, var: (b, layer, h, 0, 0))
    grid_spec = pltpu.PrefetchScalarGridSpec(
        num_scalar_prefetch=1,
        grid=(nb, nh, nkb),
        in_specs=[pl.BlockSpec((nq, hd), lambda b, h, k, var: (b * nkb + k, q_col + h)),
                  pl.BlockSpec((dec_seq, hd), lambda b, h, k, var: (b, q_col + nh + h)),
                  pl.BlockSpec((dec_seq, hd), lambda b, h, k, var: (b, q_col + 2 * nh + h)),
                  ctx_spec, ctx_spec,
                  pl.BlockSpec((None, None, nq, NA_K_ROWS * GRID_W),
                               lambda b, h, k, var: (h, var[k], 0, 0))],
        out_specs=pl.BlockSpec((nq, hd), lambda b, h, k, var: (b * nkb + k, h)),
    )
    return pl.pallas_call(
        functools.partial(_na_attn_kernel, rows=rows, scale=hd ** -0.5),
        grid_spec=grid_spec,
        out_shape=jax.ShapeDtypeStruct((nb * dec_seq, nh * hd), BF16),
        compiler_params=_cparams("parallel", "parallel", "arbitrary"),
        name="neighbourhood_attention",
    )(var_ids, proj, proj, proj, cache_k, cache_v, tbl)


def _merge_kernel(fl_ref, fc_ref, p_ref, al_ref, ac_ref, c_ref, g0_ref, g1_ref, g2_ref, g3_ref, w_ref,
                  o_ref, *, n_lat_tiles):
    is_lat = pl.program_id(0) < n_lat_tiles
    f = jnp.where(is_lat, fl_ref[...], fc_ref[...])
    a = jnp.where(is_lat, al_ref[...], ac_ref[...])
    acc = None
    for n, (br, gt) in enumerate(((f, g0_ref), (p_ref[...], g1_ref), (a, g2_ref), (c_ref[...], g3_ref))):
        y = gt[...].astype(F32) * jnp.dot(br, w_ref[n], preferred_element_type=F32)
        acc = y if acc is None else acc + y
    o_ref[...] = acc.astype(o_ref.dtype)


def _merge_call(f_lat, f_ctx, o_p, a_lat, a_ctx, o_c, gates, w_branch, tm, tn):
    t, bw = o_p.shape
    d = w_branch.shape[2]
    nj = d // tn
    nl = f_lat.shape[0] // tm
    br_spec = pl.BlockSpec((tm, bw), lambda i, j: (i, 0))
    lat_spec = pl.BlockSpec((tm, bw), lambda i, j: (jnp.minimum(i, nl - 1), 0))
    ctx_spec = pl.BlockSpec((tm, bw), lambda i, j: (jnp.maximum(i - nl, 0), 0))
    gate_specs = [pl.BlockSpec((tm, tn), functools.partial(lambda i, j, n: (i, n * nj + j), n=n))
                  for n in range(N_BRANCH)]
    return pl.pallas_call(
        functools.partial(_merge_kernel, n_lat_tiles=nl),
        grid=(t // tm, nj),
        in_specs=[lat_spec, ctx_spec, br_spec, lat_spec, ctx_spec, br_spec] + gate_specs
                 + [pl.BlockSpec((N_BRANCH, bw, tn), lambda i, j: (0, 0, j))],
        out_specs=pl.BlockSpec((tm, tn), lambda i, j: (i, j)),
        out_shape=jax.ShapeDtypeStruct((t, d), BF16),
        compiler_params=_cparams("parallel", "parallel"),
        name="branch_merge",
    )(f_lat, f_ctx, o_p, a_lat, a_ctx, o_c, gates, gates, gates, gates, w_branch)


def _out_kernel(a_ref, w_ref, x_hbm, mod_ref, g_ref, *rest, tm, gate_row, next_rows):
    if next_rows is None:
        o_ref, xbuf_ref, sem = rest
    else:
        nmod_ref, ng_ref, o_ref, h_ref, xbuf_ref, sem = rest
    k = pl.program_id(1)
    x_copy = pltpu.make_async_copy(x_hbm.at[pl.ds(pl.program_id(0) * tm, tm)], xbuf_ref, sem)

    @pl.when(k == 0)
    def _():
        x_copy.start()
        o_ref[...] = jnp.zeros_like(o_ref)

    o_ref[...] += jnp.dot(a_ref[...], w_ref[...], preferred_element_type=F32)

    @pl.when(k == pl.num_programs(1) - 1)
    def _():
        x_copy.wait()
        xn = xbuf_ref[...] + mod_ref[gate_row:gate_row + 1, :] * _rms(o_ref[...], g_ref[...])
        o_ref[...] = xn
        if next_rows is not None:
            h_ref[...] = _mod_norm(xn, ng_ref[...], nmod_ref, *next_rows).astype(h_ref.dtype)


def _k_tile(kdim, cap):
    lanes = 128
    return max(t for t in range(lanes, cap + 1, lanes) if kdim % t == 0)


def _out_call(a, w, x, mod, g, gate_row, nxt, mod_row, tm):
    t, d = x.shape
    kdim = a.shape[1]
    tk = _k_tile(kdim, 1024)
    row = pl.BlockSpec((tm, d), lambda i, k: (i, 0))
    mod_spec = pl.BlockSpec((None, 6, d), lambda i, k: (mod_row(i * tm), 0, 0))
    vec = pl.BlockSpec((1, d), lambda i, k: (0, 0))
    in_specs = [pl.BlockSpec((tm, tk), lambda i, k: (i, k)),
                pl.BlockSpec((tk, d), lambda i, k: (k, 0)),
                pl.BlockSpec(memory_space=pl.ANY), mod_spec, vec]
    args = [a, w, x, mod, g]
    out_specs, out_shape = row, jax.ShapeDtypeStruct((t, d), F32)
    if nxt is not None:
        in_specs += [mod_spec, vec]
        args += [nxt[0], nxt[1]]
        out_specs = [row, row]
        out_shape = [out_shape, jax.ShapeDtypeStruct((t, d), BF16)]
    return pl.pallas_call(
        functools.partial(_out_kernel, tm=tm, gate_row=gate_row,
                          next_rows=None if nxt is None else nxt[2]),
        grid=(t // tm, kdim // tk),
        in_specs=in_specs,
        out_specs=out_specs,
        out_shape=out_shape,
        scratch_shapes=[pltpu.VMEM((tm, d), F32), pltpu.SemaphoreType.DMA(())],
        compiler_params=_cparams("arbitrary", "arbitrary", vmem_mib=VMEM_LIMIT_WIDE_MIB),
        name="proj_norm_residual",
    )(*args)


def _ffn_up_kernel(h_ref, wg_ref, wu_ref, o_ref):
    h = h_ref[...]
    y = _silu(jnp.dot(h, wg_ref[...], preferred_element_type=F32)) * jnp.dot(
        h, wu_ref[...], preferred_element_type=F32)
    o_ref[...] = y.astype(o_ref.dtype)


def _ffn_up_call(h, wg, wu, tm, tn):
    t, d = h.shape
    f = wg.shape[1]
    w_spec = pl.BlockSpec((d, tn), lambda i, j: (0, j))
    return pl.pallas_call(
        _ffn_up_kernel,
        grid=(t // tm, f // tn),
        in_specs=[pl.BlockSpec((tm, d), lambda i, j: (i, 0)), w_spec, w_spec],
        out_specs=pl.BlockSpec((tm, tn), lambda i, j: (i, j)),
        out_shape=jax.ShapeDtypeStruct((t, f), BF16),
        compiler_params=_cparams("parallel", "parallel"),
        name="ffn_up",
    )(h, wg, wu)


def _split_bf16(x):
    hi = x.astype(BF16)
    return hi, (x - hi.astype(F32)).astype(BF16)


def _pack_bf16_pairs(h):
    n = h.shape[1] // 2
    hb = h.astype(BF16).astype(F32)
    hi = pltpu.bitcast(hb[:, :n], jnp.int32)
    lo = pltpu.bitcast(hb[:, n:], jnp.int32)
    return hi | lax.shift_right_logical(lo, 16)


def _unpack_bf16_pairs(w):
    hi = pltpu.bitcast(w & jnp.int32(-65536), F32).astype(BF16)
    lo = pltpu.bitcast(w << 16, F32).astype(BF16)
    return hi, lo


def _router_kernel(x_ref, mod_ref, g_ref, w_ref, b_ref, o_ref, hp_ref, *, n_experts):
    h = _mod_norm(x_ref[...], g_ref[...], mod_ref, SCALE2, SHIFT2)
    hp_ref[...] = _pack_bf16_pairs(h)
    h_hi, h_lo = _split_bf16(h)
    w_hi, w_lo = _split_bf16(w_ref[...])
    logits = (jnp.dot(h_hi, w_hi, preferred_element_type=F32)
              + jnp.dot(h_lo, w_hi, preferred_element_type=F32)
              + jnp.dot(h_hi, w_lo, preferred_element_type=F32)) + b_ref[...]
    lane = lax.broadcasted_iota(jnp.int32, logits.shape, 1).astype(F32)
    neg = -jnp.inf
    no_lane = float(logits.shape[1])
    logits = jnp.where(lane < n_experts, logits, neg)
    m1 = jnp.max(logits, axis=-1, keepdims=True)
    i1 = jnp.min(jnp.where(logits == m1, lane, no_lane), axis=-1, keepdims=True)
    rest = jnp.where(lane == i1, neg, logits)
    m2 = jnp.max(rest, axis=-1, keepdims=True)
    i2 = jnp.min(jnp.where(rest == m2, lane, no_lane), axis=-1, keepdims=True)
    e2 = jnp.exp(m2 - m1)
    den = 1.0 + e2
    o_ref[...] = (jnp.where(lane == 0.0, i1, 0.0) + jnp.where(lane == 1.0, i2, 0.0)
                  + jnp.where(lane == 2.0, 1.0 / den, 0.0) + jnp.where(lane == 3.0, e2 / den, 0.0))


def _router_call(x, mod, g, w_router, b_router, mod_row, tm):
    t, d = x.shape
    ne = w_router.shape[1]
    lanes = 128
    w = jnp.zeros((d, lanes), F32).at[:, :ne].set(w_router)
    b = jnp.zeros((1, lanes), F32).at[0, :ne].set(b_router)
    return pl.pallas_call(
        functools.partial(_router_kernel, n_experts=ne),
        grid=(t // tm,),
        in_specs=[pl.BlockSpec((tm, d), lambda i: (i, 0)),
                  pl.BlockSpec((None, 6, d), lambda i: (mod_row(i * tm), 0, 0)),
                  pl.BlockSpec((1, d), lambda i: (0, 0)),
                  pl.BlockSpec((d, lanes), lambda i: (0, 0)),
                  pl.BlockSpec((1, lanes), lambda i: (0, 0))],
        out_specs=[pl.BlockSpec((tm, lanes), lambda i: (i, 0)),
                   pl.BlockSpec((tm, d // 2), lambda i: (i, 0))],
        out_shape=[jax.ShapeDtypeStruct((t, lanes), F32),
                   jax.ShapeDtypeStruct((t, d // 2), jnp.int32)],
        compiler_params=_cparams("parallel"),
        name="router",
    )(x, mod, g, w, b)


def _routing_tables(route, n_experts, tm):
    t = route.shape[0]
    na = TOP_K * t
    p_rows = na + n_experts * tm
    nt = p_rows // tm
    e_flat = route[:, :TOP_K].astype(jnp.int32).reshape(na)
    onehot = (e_flat[:, None] == jnp.arange(n_experts, dtype=jnp.int32)[None, :]).astype(jnp.int32)
    csum = jnp.cumsum(onehot, axis=0)
    rank = jnp.sum((csum - onehot) * onehot, axis=1)
    counts = csum[-1]
    padded = ((counts + tm - 1) // tm) * tm
    ends = jnp.cumsum(padded)
    pos = (ends - padded)[e_flat] + rank
    src = jnp.zeros((p_rows,), jnp.int32).at[pos].set(jnp.arange(na, dtype=jnp.int32) // TOP_K)
    tile_expert = jnp.minimum(
        jnp.searchsorted(ends, jnp.arange(nt, dtype=jnp.int32) * tm, side="right"), n_experts - 1)
    meta = jnp.concatenate([tile_expert.astype(jnp.int32), (ends[-1:] // tm).astype(jnp.int32)])
    return src, pos.astype(jnp.int32), meta


def _gather_rows_kernel(idx_ref, src_ref, o_ref, buf_ref, sem_ref, *, rows):
    i = pl.program_id(0)
    n = pl.num_programs(0)

    def issue(tile, slot):
        def body(r, carry):
            pltpu.make_async_copy(src_ref.at[pl.ds(idx_ref[tile * rows + r], 1)],
                                  buf_ref.at[slot, pl.ds(r, 1)], sem_ref.at[slot]).start()
            return carry
        lax.fori_loop(0, rows, body, 0, unroll=DMA_ISSUE_UNROLL)

    @pl.when(i == 0)
    def _():
        issue(0, 0)

    @pl.when(i + 1 < n)
    def _():
        issue(i + 1, (i + 1) % 2)

    slot = i % 2
    pltpu.make_async_copy(src_ref.at[pl.ds(0, rows)], buf_ref.at[slot], sem_ref.at[slot]).wait()
    half = buf_ref.shape[2]
    hi, lo = _unpack_bf16_pairs(buf_ref[slot])
    o_ref[:, :half] = hi
    o_ref[:, half:] = lo


def _gather_rows_call(src, idx, rows):
    p_rows = idx.shape[0]
    half = src.shape[1]
    grid_spec = pltpu.PrefetchScalarGridSpec(
        num_scalar_prefetch=1,
        grid=(p_rows // rows,),
        in_specs=[pl.BlockSpec(memory_space=pl.ANY)],
        out_specs=pl.BlockSpec((rows, 2 * half), lambda i, idx: (i, 0)),
        scratch_shapes=[pltpu.VMEM((2, rows, half), src.dtype), pltpu.SemaphoreType.DMA((2,))],
    )
    return pl.pallas_call(
        functools.partial(_gather_rows_kernel, rows=rows),
        grid_spec=grid_spec,
        out_shape=jax.ShapeDtypeStruct((p_rows, 2 * half), BF16),
        compiler_params=_cparams("arbitrary"),
        name="moe_gather",
    )(idx, src)


def _moe_up_kernel(meta_ref, xs_ref, wg_ref, wu_ref, o_ref, wgb_ref, wub_ref, *, n_tiles):
    i = pl.program_id(1)
    used = i < meta_ref[n_tiles]
    fresh = jnp.logical_or(i == 0, meta_ref[i] != meta_ref[jnp.maximum(i - 1, 0)])

    @pl.when(jnp.logical_and(used, fresh))
    def _():
        wgb_ref[...] = wg_ref[...].astype(BF16)
        wub_ref[...] = wu_ref[...].astype(BF16)

    @pl.when(used)
    def _():
        h = xs_ref[...]
        y = _silu(jnp.dot(h, wgb_ref[...], preferred_element_type=F32)) * jnp.dot(
            h, wub_ref[...], preferred_element_type=F32)
        o_ref[...] = y.astype(o_ref.dtype)

    @pl.when(jnp.logical_not(used))
    def _():
        o_ref[...] = jnp.zeros_like(o_ref)


def _moe_up_call(xs, meta, wg, wu, tm, tn):
    p_rows, d = xs.shape
    f = wg.shape[2]
    nt = p_rows // tm
    w_spec = pl.BlockSpec((None, d, tn), lambda j, i, meta: (meta[i], 0, j))
    grid_spec = pltpu.PrefetchScalarGridSpec(
        num_scalar_prefetch=1,
        grid=(f // tn, nt),
        in_specs=[pl.BlockSpec((tm, d), lambda j, i, meta: (i, 0)), w_spec, w_spec],
        out_specs=pl.BlockSpec((None, tm, tn), lambda j, i, meta: (j, i, 0)),
        scratch_shapes=[pltpu.VMEM((d, tn), BF16), pltpu.VMEM((d, tn), BF16)],
    )
    return pl.pallas_call(
        functools.partial(_moe_up_kernel, n_tiles=nt),
        grid_spec=grid_spec,
        out_shape=jax.ShapeDtypeStruct((f // tn, p_rows, tn), BF16),
        compiler_params=_cparams("arbitrary", "arbitrary", vmem_mib=VMEM_LIMIT_WIDE_MIB),
        name="moe_up",
    )(meta, xs, wg, wu)


def _moe_down_kernel(meta_ref, a_ref, w_ref, o_ref, *, n_tiles):
    i = pl.program_id(0)

    @pl.when(pl.program_id(1) == 0)
    def _():
        o_ref[...] = jnp.zeros_like(o_ref)

    @pl.when(i < meta_ref[n_tiles])
    def _():
        nsub, _, tk = a_ref.shape
        acc = o_ref[...]
        for s in range(nsub):
            acc = acc + jnp.dot(a_ref[s], w_ref[s * tk:(s + 1) * tk, :], preferred_element_type=F32)
        o_ref[...] = acc


def _moe_down_call(a, meta, wd, tm, nsub):
    nk, p_rows, tk = a.shape
    d = wd.shape[2]
    nt = p_rows // tm
    grid_spec = pltpu.PrefetchScalarGridSpec(
        num_scalar_prefetch=1,
        grid=(nt, nk // nsub),
        in_specs=[pl.BlockSpec((nsub, tm, tk), lambda i, k, meta: (k, i, 0)),
                  pl.BlockSpec((None, nsub * tk, d), lambda i, k, meta: (meta[i], k, 0))],
        out_specs=pl.BlockSpec((tm, d), lambda i, k, meta: (i, 0)),
    )
    return pl.pallas_call(
        functools.partial(_moe_down_kernel, n_tiles=nt),
        grid_spec=grid_spec,
        out_shape=jax.ShapeDtypeStruct((p_rows, d), F32),
        compiler_params=_cparams("parallel", "arbitrary"),
        name="moe_down",
    )(meta, a, wd)


def _moe_combine_kernel(pos_ref, ys_ref, route_ref, x_ref, mod_ref, g_ref, *rest, rows, n_lat_tiles):
    if n_lat_tiles is None:
        o_ref, buf_ref, sem_ref = rest
    else:
        lat_ref, ctx_ref, buf_ref, sem_ref = rest
    i = pl.program_id(0)
    n = pl.num_programs(0)

    def issue(tile, slot):
        def body(r, carry):
            for s in range(TOP_K):
                pltpu.make_async_copy(ys_ref.at[pl.ds(pos_ref[TOP_K * (tile * rows + r) + s], 1)],
                                      buf_ref.at[slot, s, pl.ds(r, 1)], sem_ref.at[slot]).start()
            return carry
        lax.fori_loop(0, rows, body, 0, unroll=DMA_ISSUE_UNROLL)

    @pl.when(i == 0)
    def _():
        issue(0, 0)

    @pl.when(i + 1 < n)
    def _():
        issue(i + 1, (i + 1) % 2)

    slot = i % 2
    for s in range(TOP_K):
        pltpu.make_async_copy(ys_ref.at[pl.ds(0, rows)], buf_ref.at[slot, s], sem_ref.at[slot]).wait()
    route = route_ref[...]
    y = None
    for s in range(TOP_K):
        term = route[:, TOP_K + s:TOP_K + s + 1] * buf_ref[slot, s]
        y = term if y is None else y + term
    res = x_ref[...] + mod_ref[GATE2:GATE2 + 1, :] * _rms(y, g_ref[...])
    if n_lat_tiles is None:
        o_ref[...] = res
    else:
        @pl.when(i < n_lat_tiles)
        def _():
            lat_ref[...] = res

        @pl.when(i >= n_lat_tiles)
        def _():
            ctx_ref[...] = res


def _moe_combine_call(ys, pos, route, x, mod, g, mod_row, rows, split_rows):
    t, d = x.shape
    row = pl.BlockSpec((rows, d), lambda i, pos: (i, 0))
    if split_rows is None:
        nl = None
        out_specs, out_shape = row, jax.ShapeDtypeStruct((t, d), F32)
    else:
        nl = split_rows // rows
        out_specs = [pl.BlockSpec((rows, d), lambda i, pos: (jnp.minimum(i, nl - 1), 0)),
                     pl.BlockSpec((rows, d), lambda i, pos: (jnp.maximum(i - nl, 0), 0))]
        out_shape = [jax.ShapeDtypeStruct((split_rows, d), F32),
                     jax.ShapeDtypeStruct((t - split_rows, d), F32)]
    grid_spec = pltpu.PrefetchScalarGridSpec(
        num_scalar_prefetch=1,
        grid=(t // rows,),
        in_specs=[pl.BlockSpec(memory_space=pl.ANY),
                  pl.BlockSpec((rows, route.shape[1]), lambda i, pos: (i, 0)),
                  row,
                  pl.BlockSpec((None, 6, d), lambda i, pos: (mod_row(i * rows), 0, 0)),
                  pl.BlockSpec((1, d), lambda i, pos: (0, 0))],
        out_specs=out_specs,
        scratch_shapes=[pltpu.VMEM((2, TOP_K, rows, d), F32), pltpu.SemaphoreType.DMA((2,))],
    )
    return pl.pallas_call(
        functools.partial(_moe_combine_kernel, rows=rows, n_lat_tiles=nl),
        grid_spec=grid_spec,
        out_shape=out_shape,
        compiler_params=_cparams("arbitrary"),
        name="moe_combine",
    )(pos, ys, route, x, mod, g)


def kernel(x_prompt, x_sample, cache_k, cache_v, c, c_ctx, w_ada, b_ada, norm_g, w_in, b_in, pool_w, pool_scale, rpb, conv_w, conv_b, conv_ln_g, conv_ln_b, w_branch, w_out, w_gate_d, w_up_d, w_down_d, w_router, b_router, w_gate_e, w_up_e, w_down_e):
    nbp, seq, d = x_prompt.shape
    nbs, dec_seq, _ = x_sample.shape
    depth = w_ada.shape[0]
    bw = d // N_BRANCH
    hd = bw // NA_HEADS
    ns_rows = nbs * dec_seq
    np_rows = nbp * seq
    assert dec_seq & (dec_seq - 1) == 0 and seq & (seq - 1) == 0
    assert nbs + 1 <= MOD_ROWS

    t_rows = ns_rows + np_rows
    tm_wide = max(tm for tm in (2048, 1024, 512) if dec_seq % tm == 0 and np_rows % tm == 0)
    tm_mid = min(tm_wide, 1024)
    tm_moe = 512
    tile_seq = min(256, seq)
    tn = 512
    assert t_rows % tm_wide == 0 and seq % tile_seq == 0

    def mod_row(row0):
        return jnp.where(row0 < ns_rows, row0 // dec_seq, nbs)

    x = jnp.concatenate([x_sample.reshape(ns_rows, d), x_prompt.reshape(np_rows, d)], axis=0)
    cvec = jnp.zeros((MOD_ROWS, d), F32).at[:nbs].set(c).at[nbs].set(c_ctx)
    mods = _ada_call(cvec, w_ada, b_ada).reshape(depth, MOD_ROWS, 6, d)
    cc, sc = _channel_dft_mats(bw, FNET_GROUPS)

    pool_col, conv_col, main_cols = 1, 5, 7 * bw
    q_col = 2 * bw // hd

    def gain(l, n):
        return norm_g[l, n][None, :]

    h = _norm_call(x, mods[0], gain(0, 0), SCALE1, SHIFT1, mod_row, tm_mid)
    new_k, new_v = [], []
    y_split = None
    for l in range(depth):
        mod = mods[l]
        last = l + 1 == depth
        proj, gates = _in_call(h, w_in[l].astype(BF16), b_in[l][None, :], main_cols, tm_wide, tn)

        f_lat = _fourier_call(proj, 0, nbs, dec_seq, bw, cc, sc)
        f_ctx = _fourier_call(proj, ns_rows, nbp, seq, bw, cc, sc)
        o_p = _pool_call(proj, pool_w[l].astype(BF16), pool_scale[l][None, :], pool_col, bw,
                         tile_seq, ns_rows, dec_seq, seq)
        o_c = _conv_call(proj, conv_w[l], conv_b[l][None, :], conv_ln_g[l][None, :],
                         conv_ln_b[l][None, :], conv_col, bw, tile_seq, ns_rows, dec_seq, seq)
        a_lat = _na_attn_call(proj, cache_k, cache_v, l, rpb[l], nbs, dec_seq, q_col, hd)
        a_ctx, k_ctx, v_ctx = _ctx_attn_call(proj, ns_rows, nbp, seq, q_col, hd)
        new_k.append(k_ctx)
        new_v.append(v_ctx)

        merged = _merge_call(f_lat, f_ctx, o_p, a_lat, a_ctx, o_c, gates, w_branch[l].astype(BF16),
                             tm_mid, tn)
        nxt = None if last else (mods[l + 1], gain(l + 1, 0), (SCALE1, SHIFT1))
        i = l // 2
        if l % 2 == 1:
            x = _out_call(merged, w_out[l].astype(BF16), x, mod, gain(l, 1), GATE1, None,
                          mod_row, tm_mid)
            ne = w_router.shape[2]
            route, h_packed = _router_call(x, mod, gain(l, 2), w_router[i], b_router[i], mod_row, 512)
            src, pos, meta = _routing_tables(route, ne, tm_moe)
            xs = _gather_rows_call(h_packed, src, 256)
            hmid = _moe_up_call(xs, meta, w_gate_e[i], w_up_e[i], tm_moe, 2 * tn)
            ys = _moe_down_call(hmid, meta, w_down_e[i].astype(BF16), tm_moe, 1)
            out = _moe_combine_call(ys, pos, route, x, mod, gain(l, 3), mod_row, 256,
                                    ns_rows if last else None)
            if last:
                y_split = out
            else:
                x = out
                h = _norm_call(x, nxt[0], nxt[1], *nxt[2], mod_row, tm_mid)
        else:
            x, h2 = _out_call(merged, w_out[l].astype(BF16), x, mod, gain(l, 1), GATE1,
                              (mod, gain(l, 2), (SCALE2, SHIFT2)), mod_row, tm_mid)
            hmid = _ffn_up_call(h2, w_gate_d[i].astype(BF16), w_up_d[i].astype(BF16), tm_wide, tn)
            if last:
                x = _out_call(hmid, w_down_d[i].astype(BF16), x, mod, gain(l, 3), GATE2, None,
                              mod_row, tm_mid)
            else:
                x, h = _out_call(hmid, w_down_d[i].astype(BF16), x, mod, gain(l, 3), GATE2, nxt,
                                 mod_row, tm_mid)

    if y_split is None:
        y_split = x[:ns_rows], x[ns_rows:]
    y_sample = y_split[0].reshape(nbs, dec_seq, d)
    y_prompt = y_split[1].reshape(nbp, seq, d)
    return y_prompt, y_sample, jnp.stack(new_k, axis=1), jnp.stack(new_v, axis=1)
```

```python
import functools

import numpy as np
import jax
import jax.numpy as jnp
from jax import lax
from jax.experimental import pallas as pl
from jax.experimental.pallas import tpu as pltpu

F32 = jnp.float32
BF16 = jnp.bfloat16

N_BRANCH = 4
FNET_GROUPS = 4
POOL_WINDOWS = (2, 4, 8, 16)
NA_HEADS = 4
NA_WIN_ROWS = 8
NA_WIN_COLS = 16
GRID_W = 64
CONV_K = 31
TOP_K = 2
RMS_EPS = 1e-6
LN_EPS = 1e-5
MASK_VALUE = -1e30
MOD_ROWS = 8
POOL_HALO = 8
CONV_HALO = 16
NA_Q_ROWS = 8
NA_K_ROWS = 16
DMA_ISSUE_UNROLL = 8
VMEM_LIMIT_MIB = 48
VMEM_LIMIT_WIDE_MIB = 56

SHIFT1, SCALE1, GATE1, SHIFT2, SCALE2, GATE2 = range(6)


def _cparams(*sem, vmem_mib=VMEM_LIMIT_MIB):
    return pltpu.CompilerParams(dimension_semantics=sem, vmem_limit_bytes=vmem_mib * 1024 * 1024)


def _silu(x):
    return x * jax.nn.sigmoid(x)


def _rms(x, g):
    return x * lax.rsqrt(jnp.mean(x * x, axis=-1, keepdims=True) + RMS_EPS) * g


def _mod_norm(x, g, mod_ref, scale_row, shift_row):
    return (_rms(x, g) * (1.0 + mod_ref[scale_row:scale_row + 1, :])
            + mod_ref[shift_row:shift_row + 1, :])


def _nt_dot(a, b):
    return lax.dot_general(a, b, (((1,), (1,)), ((), ())), preferred_element_type=F32)


def _ada_kernel(c_ref, w_ref, b_ref, o_ref):
    s = _silu(c_ref[...]).astype(BF16)
    o_ref[...] = jnp.dot(s, w_ref[...].astype(BF16), preferred_element_type=F32) + b_ref[...]


def _ada_call(cvec, w_ada, b_ada):
    depth, d, n = w_ada.shape
    tn = 512
    return pl.pallas_call(
        _ada_kernel,
        grid=(depth, n // tn),
        in_specs=[pl.BlockSpec((MOD_ROWS, d), lambda l, j: (0, 0)),
                  pl.BlockSpec((None, d, tn), lambda l, j: (l, 0, j)),
                  pl.BlockSpec((None, 1, tn), lambda l, j: (l, 0, j))],
        out_specs=pl.BlockSpec((None, MOD_ROWS, tn), lambda l, j: (l, 0, j)),
        out_shape=jax.ShapeDtypeStruct((depth, MOD_ROWS, n), F32),
        compiler_params=_cparams("parallel", "parallel"),
        name="adaln",
    )(cvec, w_ada, b_ada.reshape(depth, 1, n))


def _entry_kernel(lat_ref, ctx_ref, mod_ref, g_ref, x_ref, h_ref, *, n_lat_tiles):
    x = jnp.where(pl.program_id(0) < n_lat_tiles, lat_ref[...], ctx_ref[...])
    x_ref[...] = x
    h_ref[...] = _mod_norm(x, g_ref[...], mod_ref, SCALE1, SHIFT1).astype(h_ref.dtype)


def _entry_call(x_lat, x_ctx, mod, g, mod_row, tm):
    d = x_lat.shape[1]
    nl = x_lat.shape[0] // tm
    t = x_lat.shape[0] + x_ctx.shape[0]
    row = pl.BlockSpec((tm, d), lambda i: (i, 0))
    return pl.pallas_call(
        functools.partial(_entry_kernel, n_lat_tiles=nl),
        grid=(t // tm,),
        in_specs=[pl.BlockSpec((tm, d), lambda i: (jnp.minimum(i, nl - 1), 0)),
                  pl.BlockSpec((tm, d), lambda i: (jnp.maximum(i - nl, 0), 0)),
                  pl.BlockSpec((None, 6, d), lambda i: (mod_row(i * tm), 0, 0)),
                  pl.BlockSpec((1, d), lambda i: (0, 0))],
        out_specs=[row, row],
        out_shape=[jax.ShapeDtypeStruct((t, d), F32), jax.ShapeDtypeStruct((t, d), BF16)],
        compiler_params=_cparams("parallel"),
        name="entry_norm",
    )(x_lat, x_ctx, mod, g)


def _norm_kernel(x_ref, mod_ref, g_ref, o_ref, *, scale_row, shift_row):
    o_ref[...] = _mod_norm(x_ref[...], g_ref[...], mod_ref, scale_row, shift_row).astype(o_ref.dtype)


def _norm_call(x, mod, g, scale_row, shift_row, mod_row, tm):
    t, d = x.shape
    return pl.pallas_call(
        functools.partial(_norm_kernel, scale_row=scale_row, shift_row=shift_row),
        grid=(t // tm,),
        in_specs=[pl.BlockSpec((tm, d), lambda i: (i, 0)),
                  pl.BlockSpec((None, 6, d), lambda i: (mod_row(i * tm), 0, 0)),
                  pl.BlockSpec((1, d), lambda i: (0, 0))],
        out_specs=pl.BlockSpec((tm, d), lambda i: (i, 0)),
        out_shape=jax.ShapeDtypeStruct((t, d), BF16),
        compiler_params=_cparams("parallel"),
        name="mod_norm",
    )(x, mod, g)


def _in_kernel(h_ref, w_ref, b_ref, main_ref, gate_ref, *, n_main):
    j = pl.program_id(1)
    y = jnp.dot(h_ref[...], w_ref[...], preferred_element_type=F32) + b_ref[...]

    @pl.when(j < n_main)
    def _():
        main_ref[...] = y

    @pl.when(j >= n_main)
    def _():
        gate_ref[...] = (0.5 * jnp.tanh(0.5 * y) + 0.5).astype(gate_ref.dtype)


def _in_call(h, w, b, layer, main_cols, tm, tn):
    t, d = h.shape
    n = w.shape[2]
    n_main = main_cols // tn
    return pl.pallas_call(
        functools.partial(_in_kernel, n_main=n_main),
        grid=(t // tm, n // tn),
        in_specs=[pl.BlockSpec((tm, d), lambda i, j: (i, 0)),
                  pl.BlockSpec((None, d, tn), lambda i, j: (layer, 0, j)),
                  pl.BlockSpec((1, tn), lambda i, j: (0, j))],
        out_specs=[pl.BlockSpec((tm, tn), lambda i, j: (i, jnp.minimum(j, n_main - 1))),
                   pl.BlockSpec((tm, tn), lambda i, j: (i, jnp.maximum(j - n_main, 0)))],
        out_shape=[jax.ShapeDtypeStruct((t, main_cols), F32),
                   jax.ShapeDtypeStruct((t, n - main_cols), BF16)],
        compiler_params=_cparams("parallel", "arbitrary"),
        name="in_proj",
    )(h, w, b)


def _dft_mats(n):
    scale = 1.0 / np.sqrt(n)
    j = jnp.arange(n, dtype=jnp.int32)
    if n <= 1024:
        ang = ((j[:, None] * j[None, :]) % n).astype(F32) * (2.0 * np.pi / n)
        return (jnp.cos(ang) * scale).astype(BF16), (jnp.sin(ang) * scale).astype(BF16)
    base = 64
    hi = n // base
    k1 = jnp.arange(hi, dtype=jnp.int32)
    k0 = jnp.arange(base, dtype=jnp.int32)
    a = ((j[:, None] * k1[None, :]) % hi).astype(F32) * (2.0 * np.pi / hi)
    b = ((j[:, None] * k0[None, :]) % n).astype(F32) * (2.0 * np.pi / n)
    ca, sa = jnp.cos(a)[:, :, None], jnp.sin(a)[:, :, None]
    cb, sb = jnp.cos(b)[:, None, :] * scale, jnp.sin(b)[:, None, :] * scale
    c = (ca * cb - sa * sb).reshape(n, n)
    s = (sa * cb + ca * sb).reshape(n, n)
    return c.astype(BF16), s.astype(BF16)


def _channel_dft_mats(width, groups):
    gw = width // groups
    k = np.arange(gw)
    ang = 2.0 * np.pi * ((k[:, None] * k[None, :]) % gw) / gw
    c = np.zeros((width, width), np.float32)
    s = np.zeros((width, width), np.float32)
    for g in range(groups):
        sl = slice(g * gw, (g + 1) * gw)
        c[sl, sl] = np.cos(ang) / np.sqrt(gw)
        s[sl, sl] = np.sin(ang) / np.sqrt(gw)
    return jnp.asarray(c, BF16), jnp.asarray(s, BF16)


def _fnet1_kernel(u_ref, cc_ref, sc_ref, vc_ref, vs_ref):
    u = u_ref[...].astype(BF16)
    vc_ref[...] = jnp.dot(u, cc_ref[...], preferred_element_type=F32).astype(BF16)
    vs_ref[...] = jnp.dot(u, sc_ref[...], preferred_element_type=F32).astype(BF16)


def _fnet2_kernel(c_ref, s_ref, vc_ref, vs_ref, o_ref, acc_ref):
    k = pl.program_id(2)

    @pl.when(k == 0)
    def _():
        acc_ref[...] = jnp.zeros_like(acc_ref)

    acc_ref[...] += (jnp.dot(c_ref[...], vc_ref[...], preferred_element_type=F32)
                     - jnp.dot(s_ref[...], vs_ref[...], preferred_element_type=F32))

    @pl.when(k == pl.num_programs(2) - 1)
    def _():
        o_ref[...] = acc_ref[...].astype(o_ref.dtype)


def _fourier_call(proj, row_base, nb, seq, bw, cc, sc):
    c_l, s_l = _dft_mats(seq)
    tm1 = min(seq, 512)
    nt1 = seq // tm1
    base1 = row_base // tm1
    vc, vs = pl.pallas_call(
        _fnet1_kernel,
        grid=(nb, nt1),
        in_specs=[pl.BlockSpec((tm1, bw), lambda b, t: (base1 + b * nt1 + t, 0)),
                  pl.BlockSpec((bw, bw), lambda b, t: (0, 0)),
                  pl.BlockSpec((bw, bw), lambda b, t: (0, 0))],
        out_specs=[pl.BlockSpec((tm1, bw), lambda b, t: (t, b)),
                   pl.BlockSpec((tm1, bw), lambda b, t: (t, b))],
        out_shape=[jax.ShapeDtypeStruct((seq, nb * bw), BF16)] * 2,
        compiler_params=_cparams("parallel", "parallel"),
        name="fnet_channels",
    )(proj, cc, sc)
    tm2 = min(seq, 1024)
    tk = min(seq, 2048)
    nt2 = seq // tm2
    return pl.pallas_call(
        _fnet2_kernel,
        grid=(nt2, nb, seq // tk),
        in_specs=[pl.BlockSpec((tm2, tk), lambda i, j, k: (i, k)),
                  pl.BlockSpec((tm2, tk), lambda i, j, k: (i, k)),
                  pl.BlockSpec((tk, bw), lambda i, j, k: (k, j)),
                  pl.BlockSpec((tk, bw), lambda i, j, k: (k, j))],
        out_specs=pl.BlockSpec((tm2, bw), lambda i, j, k: (j * nt2 + i, 0)),
        out_shape=jax.ShapeDtypeStruct((nb * seq, bw), BF16),
        scratch_shapes=[pltpu.VMEM((tm2, bw), F32)],
        compiler_params=_cparams("parallel", "parallel", "arbitrary"),
        name="fnet_positions",
    )(c_l, s_l, vc, vs)


def _seq_position(row0, ns_rows, dec_seq, seq):
    is_lat = row0 < ns_rows
    seq_len = jnp.where(is_lat, dec_seq, seq)
    pos0 = jnp.where(is_lat, row0 & (dec_seq - 1), row0 & (seq - 1))
    return seq_len, pos0


def _pool_kernel(prev_ref, cur_ref, nxt_ref, w_ref, sc_ref, o_ref, ext_ref, *,
                 tm, ns_rows, dec_seq, seq):
    seq_len, pos0 = _seq_position(pl.program_id(0) * tm, ns_rows, dec_seq, seq)
    h = POOL_HALO
    ext_ref[0:h, :] = jnp.where(pos0 == 0, 0.0, prev_ref[...])
    ext_ref[h:h + tm, :] = cur_ref[...]
    ext_ref[h + tm:2 * h + tm, :] = jnp.where(pos0 + tm == seq_len, 0.0, nxt_ref[...])
    t = pos0 + lax.broadcasted_iota(jnp.int32, (tm, 1), 0)
    gw = cur_ref.shape[1] // len(POOL_WINDOWS)
    for gi, win in enumerate(POOL_WINDOWS):
        lo = win // 2
        hi = win - lo
        cols = slice(gi * gw, (gi + 1) * gw)
        s = ext_ref[h - lo:h - lo + tm, cols]
        for j in range(1 - lo, hi):
            s = s + ext_ref[h + j:h + j + tm, cols]
        cnt = (jnp.minimum(t + hi, seq_len) - jnp.maximum(t - lo, 0)).astype(F32)
        dlt = s / cnt - cur_ref[:, cols]
        y = jnp.dot(dlt.astype(BF16), w_ref[gi], preferred_element_type=F32) * sc_ref[:, cols]
        o_ref[:, cols] = y.astype(o_ref.dtype)


def _pool_call(proj, pool_w, pool_scale, col_blk, bw, tm, ns_rows, dec_seq, seq):
    t = proj.shape[0]
    h = POOL_HALO
    r = tm // h
    last = t // h - 1
    gw = bw // len(POOL_WINDOWS)
    return pl.pallas_call(
        functools.partial(_pool_kernel, tm=tm, ns_rows=ns_rows, dec_seq=dec_seq, seq=seq),
        grid=(t // tm,),
        in_specs=[pl.BlockSpec((h, bw), lambda i: (jnp.maximum(i * r - 1, 0), col_blk)),
                  pl.BlockSpec((tm, bw), lambda i: (i, col_blk)),
                  pl.BlockSpec((h, bw), lambda i: (jnp.minimum((i + 1) * r, last), col_blk)),
                  pl.BlockSpec((len(POOL_WINDOWS), gw, gw), lambda i: (0, 0, 0)),
                  pl.BlockSpec((1, bw), lambda i: (0, 0))],
        out_specs=pl.BlockSpec((tm, bw), lambda i: (i, 0)),
        out_shape=jax.ShapeDtypeStruct((t, bw), BF16),
        scratch_shapes=[pltpu.VMEM((tm + 2 * h, bw), F32)],
        compiler_params=_cparams("parallel"),
        name="pool_mix",
    )(proj, proj, proj, pool_w, pool_scale)


def _conv_kernel(ap_ref, ac_ref, an_ref, gp_ref, gc_ref, gn_ref, w_ref, b_ref, lg_ref, lb_ref,
                 o_ref, ext_ref, y_ref, *, tm, ns_rows, dec_seq, seq):
    seq_len, pos0 = _seq_position(pl.program_id(0) * tm, ns_rows, dec_seq, seq)
    h = CONV_HALO
    ext_ref[0:h, :] = jnp.where(pos0 == 0, 0.0, ap_ref[...] * jax.nn.sigmoid(gp_ref[...]))
    ext_ref[h:h + tm, :] = ac_ref[...] * jax.nn.sigmoid(gc_ref[...])
    ext_ref[h + tm:2 * h + tm, :] = jnp.where(pos0 + tm == seq_len, 0.0,
                                              an_ref[...] * jax.nn.sigmoid(gn_ref[...]))
    bw = ac_ref.shape[1]
    lanes = 128
    for c in range(bw // lanes):
        cols = slice(c * lanes, (c + 1) * lanes)
        acc = jnp.zeros((tm, lanes), F32)
        for k in range(CONV_K):
            off = h + k - CONV_K // 2
            acc = acc + ext_ref[off:off + tm, cols] * w_ref[k:k + 1, cols]
        y_ref[:, cols] = acc + b_ref[:, cols]
    y = y_ref[...]
    mu = jnp.mean(y, axis=-1, keepdims=True)
    var = jnp.mean(jnp.square(y - mu), axis=-1, keepdims=True)
    z = (y - mu) * lax.rsqrt(var + LN_EPS) * lg_ref[...] + lb_ref[...]
    o_ref[...] = _silu(z).astype(o_ref.dtype)


def _conv_call(proj, conv_w, conv_b, ln_g, ln_b, col_blk, bw, tm, ns_rows, dec_seq, seq):
    t = proj.shape[0]
    h = CONV_HALO
    r = tm // h
    last = t // h - 1
    prev = lambda i: jnp.maximum(i * r - 1, 0)
    nxt = lambda i: jnp.minimum((i + 1) * r, last)
    vec = pl.BlockSpec((1, bw), lambda i: (0, 0))
    return pl.pallas_call(
        functools.partial(_conv_kernel, tm=tm, ns_rows=ns_rows, dec_seq=dec_seq, seq=seq),
        grid=(t // tm,),
        in_specs=[pl.BlockSpec((h, bw), lambda i: (prev(i), col_blk)),
                  pl.BlockSpec((tm, bw), lambda i: (i, col_blk)),
                  pl.BlockSpec((h, bw), lambda i: (nxt(i), col_blk)),
                  pl.BlockSpec((h, bw), lambda i: (prev(i), col_blk + 1)),
                  pl.BlockSpec((tm, bw), lambda i: (i, col_blk + 1)),
                  pl.BlockSpec((h, bw), lambda i: (nxt(i), col_blk + 1)),
                  pl.BlockSpec((CONV_K, bw), lambda i: (0, 0)),
                  vec, vec, vec],
        out_specs=pl.BlockSpec((tm, bw), lambda i: (i, 0)),
        out_shape=jax.ShapeDtypeStruct((t, bw), BF16),
        scratch_shapes=[pltpu.VMEM((tm + 2 * h, bw), F32), pltpu.VMEM((tm, bw), F32)],
        compiler_params=_cparams("parallel"),
        name="conv_module",
    )(proj, proj, proj, proj, proj, proj, conv_w, conv_b, ln_g, ln_b)


def _ctx_attn_kernel(q_ref, k_ref, v_ref, o_ref, ko_ref, vo_ref, *, scale):
    k = k_ref[...]
    v = v_ref[...]
    s = _nt_dot(q_ref[...].astype(BF16), k.astype(BF16)) * scale
    p = jnp.exp(s - jnp.max(s, axis=-1, keepdims=True))
    l = jnp.sum(p, axis=-1, keepdims=True)
    o = jnp.dot(p.astype(BF16), v.astype(BF16), preferred_element_type=F32)
    o_ref[...] = (o / l).astype(o_ref.dtype)
    ko_ref[...] = k
    vo_ref[...] = v


def _ctx_attn_call(proj, row_base, nb, seq, q_col, hd):
    base = row_base // seq
    nh = NA_HEADS
    kv_spec = pl.BlockSpec((None, None, seq, hd), lambda b, h: (b, h, 0, 0))
    return pl.pallas_call(
        functools.partial(_ctx_attn_kernel, scale=hd ** -0.5),
        grid=(nb, nh),
        in_specs=[pl.BlockSpec((seq, hd), lambda b, h: (base + b, q_col + h)),
                  pl.BlockSpec((seq, hd), lambda b, h: (base + b, q_col + nh + h)),
                  pl.BlockSpec((seq, hd), lambda b, h: (base + b, q_col + 2 * nh + h))],
        out_specs=[pl.BlockSpec((seq, hd), lambda b, h: (b, h)), kv_spec, kv_spec],
        out_shape=[jax.ShapeDtypeStruct((nb * seq, nh * hd), BF16),
                   jax.ShapeDtypeStruct((nb, nh, seq, hd), F32),
                   jax.ShapeDtypeStruct((nb, nh, seq, hd), F32)],
        compiler_params=_cparams("parallel", "parallel"),
        name="context_attention",
    )(proj, proj, proj)


def _na_key_row0(kb, rows):
    lo = kb * NA_Q_ROWS - NA_WIN_ROWS // 2
    if isinstance(kb, (int, np.integer)):
        return int(np.clip(lo, 0, rows - NA_K_ROWS))
    return jnp.clip(lo, 0, rows - NA_K_ROWS)


def _na_bias_tables(rpb, rows):
    nh = rpb.shape[0]
    w = GRID_W
    c = np.arange(w)
    cs = np.clip(c - NA_WIN_COLS // 2, 0, w - NA_WIN_COLS)
    col_ok = (c[None, :] >= cs[:, None]) & (c[None, :] < cs[:, None] + NA_WIN_COLS)
    rel_col = np.clip(c[None, :] - c[:, None] + NA_WIN_COLS - 1, 0, 2 * NA_WIN_COLS - 2)
    planes = jnp.where(col_ok[None, None], rpb[:, :, rel_col], MASK_VALUE)
    planes = jnp.concatenate([planes, jnp.full((nh, 1, w, w), MASK_VALUE, rpb.dtype)], axis=1)
    masked_plane = 2 * NA_WIN_ROWS - 1
    variants, var_ids = [], []
    for kb in range(rows // NA_Q_ROWS):
        r = kb * NA_Q_ROWS + np.arange(NA_Q_ROWS)
        rs = np.clip(r - NA_WIN_ROWS // 2, 0, rows - NA_WIN_ROWS)
        kr = _na_key_row0(kb, rows) + np.arange(NA_K_ROWS)
        ok = (kr[None, :] >= rs[:, None]) & (kr[None, :] < rs[:, None] + NA_WIN_ROWS)
        assert ok.sum() == NA_Q_ROWS * NA_WIN_ROWS
        plane = np.where(ok, kr[None, :] - r[:, None] + NA_WIN_ROWS - 1, masked_plane)
        for vi, known in enumerate(variants):
            if np.array_equal(known, plane):
                var_ids.append(vi)
                break
        else:
            var_ids.append(len(variants))
            variants.append(plane)
    tbl = pl.pallas_call(
        functools.partial(_na_table_kernel, variants=[v.tolist() for v in variants]),
        grid=(nh,),
        in_specs=[pl.BlockSpec((None, masked_plane + 1, w, w), lambda h: (h, 0, 0, 0))],
        out_specs=pl.BlockSpec((None, len(variants), NA_Q_ROWS * w, NA_K_ROWS * w),
                               lambda h: (h, 0, 0, 0)),
        out_shape=jax.ShapeDtypeStruct((nh, len(variants), NA_Q_ROWS * w, NA_K_ROWS * w), rpb.dtype),
        compiler_params=_cparams("parallel"),
        name="na_bias_table",
    )(planes)
    return tbl, jnp.asarray(np.array(var_ids, np.int32))


def _na_table_kernel(planes_ref, o_ref, *, variants):
    w = GRID_W
    for v, plane_of in enumerate(variants):
        for qr in range(NA_Q_ROWS):
            for kr in range(0, NA_K_ROWS, 2):
                pair = jnp.concatenate([planes_ref[plane_of[qr][kr]], planes_ref[plane_of[qr][kr + 1]]], axis=1)
                o_ref[v, qr * w:(qr + 1) * w, kr * w:(kr + 2) * w] = pair


def _na_attn_kernel(var_ref, q_ref, k_ref, v_ref, kc_ref, vc_ref, bias_ref, o_ref, *, rows, scale):
    del var_ref
    nk = NA_K_ROWS * GRID_W
    start = pl.multiple_of(_na_key_row0(pl.program_id(2), rows) * GRID_W, 256)
    kw = k_ref[pl.ds(start, nk), :].astype(BF16)
    vw = v_ref[pl.ds(start, nk), :].astype(BF16)
    q = q_ref[...].astype(BF16)
    s_loc = _nt_dot(q, kw) * scale + bias_ref[...]
    s_ctx = _nt_dot(q, kc_ref[...].astype(BF16)) * scale
    m = jnp.maximum(jnp.max(s_loc, axis=-1, keepdims=True), jnp.max(s_ctx, axis=-1, keepdims=True))
    p_loc = jnp.exp(s_loc - m)
    p_ctx = jnp.exp(s_ctx - m)
    l = jnp.sum(p_loc, axis=-1, keepdims=True) + jnp.sum(p_ctx, axis=-1, keepdims=True)
    o = (jnp.dot(p_loc.astype(BF16), vw, preferred_element_type=F32)
         + jnp.dot(p_ctx.astype(BF16), vc_ref[...].astype(BF16), preferred_element_type=F32))
    o_ref[...] = (o / l).astype(o_ref.dtype)


def _na_attn_call(proj, cache_k, cache_v, layer, rpb, nb, dec_seq, q_col, hd):
    rows = dec_seq // GRID_W
    assert rows >= NA_K_ROWS and rows % NA_Q_ROWS == 0
    nh = NA_HEADS
    nq = NA_Q_ROWS * GRID_W
    nkb = rows // NA_Q_ROWS
    past = cache_k.shape[3]
    tbl, var_ids = _na_bias_tables(rpb, rows)
    ctx_spec = pl.BlockSpec((None, None, None, past, hd), lambda b, h, k, var: (b, layer, h, 0, 0))
    grid_spec = pltpu.PrefetchScalarGridSpec(
        num_scalar_prefetch=1,
        grid=(nb, nh, nkb),
        in_specs=[pl.BlockSpec((nq, hd), lambda b, h, k, var: (b * nkb + k, q_col + h)),
                  pl.BlockSpec((dec_seq, hd), lambda b, h, k, var: (b, q_col + nh + h)),
                  pl.BlockSpec((dec_seq, hd), lambda b, h, k, var: (b, q_col + 2 * nh + h)),
                  ctx_spec, ctx_spec,
                  pl.BlockSpec((None, None, nq, NA_K_ROWS * GRID_W),
                               lambda b, h, k, var: (h, var[k], 0, 0))],
        out_specs=pl.BlockSpec((nq, hd), lambda b, h, k, var: (b * nkb + k, h)),
    )
    return pl.pallas_call(
        functools.partial(_na_attn_kernel, rows=rows, scale=hd ** -0.5),
        grid_spec=grid_spec,
        out_shape=jax.ShapeDtypeStruct((nb * dec_seq, nh * hd), BF16),
        compiler_params=_cparams("parallel", "parallel", "arbitrary"),
        name="neighbourhood_attention",
    )(var_ids, proj, proj, proj, cache_k, cache_v, tbl)


def _merge_kernel(fl_ref, fc_ref, p_ref, al_ref, ac_ref, c_ref, g0_ref, g1_ref, g2_ref, g3_ref, w_ref,
                  o_ref, *, n_lat_tiles):
    is_lat = pl.program_id(0) < n_lat_tiles
    f = jnp.where(is_lat, fl_ref[...], fc_ref[...])
    a = jnp.where(is_lat, al_ref[...], ac_ref[...])
    acc = None
    for n, (br, gt) in enumerate(((f, g0_ref), (p_ref[...], g1_ref), (a, g2_ref), (c_ref[...], g3_ref))):
        y = gt[...].astype(F32) * jnp.dot(br, w_ref[n], preferred_element_type=F32)
        acc = y if acc is None else acc + y
    o_ref[...] = acc.astype(o_ref.dtype)


def _merge_call(f_lat, f_ctx, o_p, a_lat, a_ctx, o_c, gates, w_branch, tm, tn):
    t, bw = o_p.shape
    d = w_branch.shape[2]
    nj = d // tn
    nl = f_lat.shape[0] // tm
    br_spec = pl.BlockSpec((tm, bw), lambda i, j: (i, 0))
    lat_spec = pl.BlockSpec((tm, bw), lambda i, j: (jnp.minimum(i, nl - 1), 0))
    ctx_spec = pl.BlockSpec((tm, bw), lambda i, j: (jnp.maximum(i - nl, 0), 0))
    gate_specs = [pl.BlockSpec((tm, tn), functools.partial(lambda i, j, n: (i, n * nj + j), n=n))
                  for n in range(N_BRANCH)]
    return pl.pallas_call(
        functools.partial(_merge_kernel, n_lat_tiles=nl),
        grid=(t // tm, nj),
        in_specs=[lat_spec, ctx_spec, br_spec, lat_spec, ctx_spec, br_spec] + gate_specs
                 + [pl.BlockSpec((N_BRANCH, bw, tn), lambda i, j: (0, 0, j))],
        out_specs=pl.BlockSpec((tm, tn), lambda i, j: (i, j)),
        out_shape=jax.ShapeDtypeStruct((t, d), BF16),
        compiler_params=_cparams("parallel", "parallel"),
        name="branch_merge",
    )(f_lat, f_ctx, o_p, a_lat, a_ctx, o_c, gates, gates, gates, gates, w_branch)


def _out_kernel(a_ref, w_ref, x_hbm, mod_ref, g_ref, *rest, tm, gate_row, next_rows):
    if next_rows is None:
        o_ref, xbuf_ref, sem = rest
    else:
        nmod_ref, ng_ref, o_ref, h_ref, xbuf_ref, sem = rest
    k = pl.program_id(1)
    x_copy = pltpu.make_async_copy(x_hbm.at[pl.ds(pl.program_id(0) * tm, tm)], xbuf_ref, sem)

    @pl.when(k == 0)
    def _():
        x_copy.start()
        o_ref[...] = jnp.zeros_like(o_ref)

    o_ref[...] += jnp.dot(a_ref[...], w_ref[...], preferred_element_type=F32)

    @pl.when(k == pl.num_programs(1) - 1)
    def _():
        x_copy.wait()
        xn = xbuf_ref[...] + mod_ref[gate_row:gate_row + 1, :] * _rms(o_ref[...], g_ref[...])
        o_ref[...] = xn
        if next_rows is not None:
            h_ref[...] = _mod_norm(xn, ng_ref[...], nmod_ref, *next_rows).astype(h_ref.dtype)


def _k_tile(kdim, cap):
    lanes = 128
    return max(t for t in range(lanes, cap + 1, lanes) if kdim % t == 0)


def _out_call(a, w, x, mod, g, gate_row, nxt, mod_row, tm):
    t, d = x.shape
    kdim = a.shape[1]
    tk = _k_tile(kdim, 1024)
    row = pl.BlockSpec((tm, d), lambda i, k: (i, 0))
    mod_spec = pl.BlockSpec((None, 6, d), lambda i, k: (mod_row(i * tm), 0, 0))
    vec = pl.BlockSpec((1, d), lambda i, k: (0, 0))
    in_specs = [pl.BlockSpec((tm, tk), lambda i, k: (i, k)),
                pl.BlockSpec((tk, d), lambda i, k: (k, 0)),
                pl.BlockSpec(memory_space=pl.ANY), mod_spec, vec]
    args = [a, w, x, mod, g]
    out_specs, out_shape = row, jax.ShapeDtypeStruct((t, d), F32)
    if nxt is not None:
        in_specs += [mod_spec, vec]
        args += [nxt[0], nxt[1]]
        out_specs = [row, row]
        out_shape = [out_shape, jax.ShapeDtypeStruct((t, d), BF16)]
    return pl.pallas_call(
        functools.partial(_out_kernel, tm=tm, gate_row=gate_row,
                          next_rows=None if nxt is None else nxt[2]),
        grid=(t // tm, kdim // tk),
        in_specs=in_specs,
        out_specs=out_specs,
        out_shape=out_shape,
        scratch_shapes=[pltpu.VMEM((tm, d), F32), pltpu.SemaphoreType.DMA(())],
        compiler_params=_cparams("arbitrary", "arbitrary", vmem_mib=VMEM_LIMIT_WIDE_MIB),
        name="proj_norm_residual",
    )(*args)


def _ffn_up_kernel(h_ref, wg_ref, wu_ref, o_ref):
    h = h_ref[...]
    y = _silu(jnp.dot(h, wg_ref[...], preferred_element_type=F32)) * jnp.dot(
        h, wu_ref[...], preferred_element_type=F32)
    o_ref[...] = y.astype(o_ref.dtype)


def _ffn_up_call(h, wg, wu, tm, tn):
    t, d = h.shape
    f = wg.shape[1]
    w_spec = pl.BlockSpec((d, tn), lambda i, j: (0, j))
    return pl.pallas_call(
        _ffn_up_kernel,
        grid=(t // tm, f // tn),
        in_specs=[pl.BlockSpec((tm, d), lambda i, j: (i, 0)), w_spec, w_spec],
        out_specs=pl.BlockSpec((tm, tn), lambda i, j: (i, j)),
        out_shape=jax.ShapeDtypeStruct((t, f), BF16),
        compiler_params=_cparams("parallel", "parallel"),
        name="ffn_up",
    )(h, wg, wu)


def _split_bf16(x):
    hi = x.astype(BF16)
    return hi, (x - hi.astype(F32)).astype(BF16)


def _pack_bf16_pairs(h):
    n = h.shape[1] // 2
    hb = h.astype(BF16).astype(F32)
    hi = pltpu.bitcast(hb[:, :n], jnp.int32)
    lo = pltpu.bitcast(hb[:, n:], jnp.int32)
    return hi | lax.shift_right_logical(lo, 16)


def _unpack_bf16_pairs(w):
    hi = pltpu.bitcast(w & jnp.int32(-65536), F32).astype(BF16)
    lo = pltpu.bitcast(w << 16, F32).astype(BF16)
    return hi, lo


def _router_kernel(x_ref, mod_ref, g_ref, w_ref, b_ref, o_ref, hp_ref, *, n_experts):
    h = _mod_norm(x_ref[...], g_ref[...], mod_ref, SCALE2, SHIFT2)
    hp_ref[...] = _pack_bf16_pairs(h)
    h_hi, h_lo = _split_bf16(h)
    w_hi, w_lo = _split_bf16(w_ref[...])
    logits = (jnp.dot(h_hi, w_hi, preferred_element_type=F32)
              + jnp.dot(h_lo, w_hi, preferred_element_type=F32)
              + jnp.dot(h_hi, w_lo, preferred_element_type=F32)) + b_ref[...]
    lane = lax.broadcasted_iota(jnp.int32, logits.shape, 1).astype(F32)
    neg = -jnp.inf
    no_lane = float(logits.shape[1])
    logits = jnp.where(lane < n_experts, logits, neg)
    m1 = jnp.max(logits, axis=-1, keepdims=True)
    i1 = jnp.min(jnp.where(logits == m1, lane, no_lane), axis=-1, keepdims=True)
    rest = jnp.where(lane == i1, neg, logits)
    m2 = jnp.max(rest, axis=-1, keepdims=True)
    i2 = jnp.min(jnp.where(rest == m2, lane, no_lane), axis=-1, keepdims=True)
    e2 = jnp.exp(m2 - m1)
    den = 1.0 + e2
    o_ref[...] = (jnp.where(lane == 0.0, i1, 0.0) + jnp.where(lane == 1.0, i2, 0.0)
                  + jnp.where(lane == 2.0, 1.0 / den, 0.0) + jnp.where(lane == 3.0, e2 / den, 0.0))


def _router_call(x, mod, g, w_router, b_router, mod_row, tm):
    t, d = x.shape
    ne = w_router.shape[1]
    lanes = 128
    w = jnp.zeros((d, lanes), F32).at[:, :ne].set(w_router)
    b = jnp.zeros((1, lanes), F32).at[0, :ne].set(b_router)
    return pl.pallas_call(
        functools.partial(_router_kernel, n_experts=ne),
        grid=(t // tm,),
        in_specs=[pl.BlockSpec((tm, d), lambda i: (i, 0)),
                  pl.BlockSpec((None, 6, d), lambda i: (mod_row(i * tm), 0, 0)),
                  pl.BlockSpec((1, d), lambda i: (0, 0)),
                  pl.BlockSpec((d, lanes), lambda i: (0, 0)),
                  pl.BlockSpec((1, lanes), lambda i: (0, 0))],
        out_specs=[pl.BlockSpec((tm, lanes), lambda i: (i, 0)),
                   pl.BlockSpec((tm, d // 2), lambda i: (i, 0))],
        out_shape=[jax.ShapeDtypeStruct((t, lanes), F32),
                   jax.ShapeDtypeStruct((t, d // 2), jnp.int32)],
        compiler_params=_cparams("parallel"),
        name="router",
    )(x, mod, g, w, b)


def _routing_tables(route, n_experts, tm):
    t = route.shape[0]
    na = TOP_K * t
    p_rows = na + n_experts * tm
    nt = p_rows // tm
    e_flat = route[:, :TOP_K].astype(jnp.int32).reshape(na)
    onehot = (e_flat[:, None] == jnp.arange(n_experts, dtype=jnp.int32)[None, :]).astype(jnp.int32)
    csum = jnp.cumsum(onehot, axis=0)
    rank = jnp.sum((csum - onehot) * onehot, axis=1)
    counts = csum[-1]
    padded = ((counts + tm - 1) // tm) * tm
    ends = jnp.cumsum(padded)
    pos = (ends - padded)[e_flat] + rank
    src = jnp.zeros((p_rows,), jnp.int32).at[pos].set(jnp.arange(na, dtype=jnp.int32) // TOP_K)
    tile_expert = jnp.minimum(
        jnp.searchsorted(ends, jnp.arange(nt, dtype=jnp.int32) * tm, side="right"), n_experts - 1)
    meta = jnp.concatenate([tile_expert.astype(jnp.int32), (ends[-1:] // tm).astype(jnp.int32)])
    return src, pos.astype(jnp.int32), meta


def _gather_rows_kernel(idx_ref, src_ref, o_ref, buf_ref, sem_ref, *, rows):
    i = pl.program_id(0)
    n = pl.num_programs(0)

    def issue(tile, slot):
        def body(r, carry):
            pltpu.make_async_copy(src_ref.at[pl.ds(idx_ref[tile * rows + r], 1)],
                                  buf_ref.at[slot, pl.ds(r, 1)], sem_ref.at[slot]).start()
            return carry
        lax.fori_loop(0, rows, body, 0, unroll=DMA_ISSUE_UNROLL)

    @pl.when(i == 0)
    def _():
        issue(0, 0)

    @pl.when(i + 1 < n)
    def _():
        issue(i + 1, (i + 1) % 2)

    slot = i % 2
    pltpu.make_async_copy(src_ref.at[pl.ds(0, rows)], buf_ref.at[slot], sem_ref.at[slot]).wait()
    half = buf_ref.shape[2]
    hi, lo = _unpack_bf16_pairs(buf_ref[slot])
    o_ref[:, :half] = hi
    o_ref[:, half:] = lo


def _gather_rows_call(src, idx, rows):
    p_rows = idx.shape[0]
    half = src.shape[1]
    grid_spec = pltpu.PrefetchScalarGridSpec(
        num_scalar_prefetch=1,
        grid=(p_rows // rows,),
        in_specs=[pl.BlockSpec(memory_space=pl.ANY)],
        out_specs=pl.BlockSpec((rows, 2 * half), lambda i, idx: (i, 0)),
        scratch_shapes=[pltpu.VMEM((2, rows, half), src.dtype), pltpu.SemaphoreType.DMA((2,))],
    )
    return pl.pallas_call(
        functools.partial(_gather_rows_kernel, rows=rows),
        grid_spec=grid_spec,
        out_shape=jax.ShapeDtypeStruct((p_rows, 2 * half), BF16),
        compiler_params=_cparams("arbitrary"),
        name="moe_gather",
    )(idx, src)


def _moe_up_kernel(meta_ref, xs_ref, wg_ref, wu_ref, o_ref, wgb_ref, wub_ref, *, n_tiles):
    i = pl.program_id(1)
    used = i < meta_ref[n_tiles]
    fresh = jnp.logical_or(i == 0, meta_ref[i] != meta_ref[jnp.maximum(i - 1, 0)])

    @pl.when(jnp.logical_and(used, fresh))
    def _():
        wgb_ref[...] = wg_ref[...].astype(BF16)
        wub_ref[...] = wu_ref[...].astype(BF16)

    @pl.when(used)
    def _():
        h = xs_ref[...]
        y = _silu(jnp.dot(h, wgb_ref[...], preferred_element_type=F32)) * jnp.dot(
            h, wub_ref[...], preferred_element_type=F32)
        o_ref[...] = y.astype(o_ref.dtype)

    @pl.when(jnp.logical_not(used))
    def _():
        o_ref[...] = jnp.zeros_like(o_ref)


def _moe_up_call(xs, meta, wg, wu, tm, tn):
    p_rows, d = xs.shape
    f = wg.shape[2]
    nt = p_rows // tm
    w_spec = pl.BlockSpec((None, d, tn), lambda j, i, meta: (meta[i], 0, j))
    grid_spec = pltpu.PrefetchScalarGridSpec(
        num_scalar_prefetch=1,
        grid=(f // tn, nt),
        in_specs=[pl.BlockSpec((tm, d), lambda j, i, meta: (i, 0)), w_spec, w_spec],
        out_specs=pl.BlockSpec((None, tm, tn), lambda j, i, meta: (j, i, 0)),
        scratch_shapes=[pltpu.VMEM((d, tn), BF16), pltpu.VMEM((d, tn), BF16)],
    )
    return pl.pallas_call(
        functools.partial(_moe_up_kernel, n_tiles=nt),
        grid_spec=grid_spec,
        out_shape=jax.ShapeDtypeStruct((f // tn, p_rows, tn), BF16),
        compiler_params=_cparams("arbitrary", "arbitrary", vmem_mib=VMEM_LIMIT_WIDE_MIB),
        name="moe_up",
    )(meta, xs, wg, wu)


def _moe_down_kernel(meta_ref, a_ref, w_ref, o_ref, *, n_tiles):
    i = pl.program_id(0)

    @pl.when(pl.program_id(1) == 0)
    def _():
        o_ref[...] = jnp.zeros_like(o_ref)

    @pl.when(i < meta_ref[n_tiles])
    def _():
        nsub, _, tk = a_ref.shape
        acc = o_ref[...]
        for s in range(nsub):
            acc = acc + jnp.dot(a_ref[s], w_ref[s * tk:(s + 1) * tk, :], preferred_element_type=F32)
        o_ref[...] = acc


def _moe_down_call(a, meta, wd, tm, nsub):
    nk, p_rows, tk = a.shape
    d = wd.shape[2]
    nt = p_rows // tm
    grid_spec = pltpu.PrefetchScalarGridSpec(
        num_scalar_prefetch=1,
        grid=(nt, nk // nsub),
        in_specs=[pl.BlockSpec((nsub, tm, tk), lambda i, k, meta: (k, i, 0)),
                  pl.BlockSpec((None, nsub * tk, d), lambda i, k, meta: (meta[i], k, 0))],
        out_specs=pl.BlockSpec((tm, d), lambda i, k, meta: (i, 0)),
    )
    return pl.pallas_call(
        functools.partial(_moe_down_kernel, n_tiles=nt),
        grid_spec=grid_spec,
        out_shape=jax.ShapeDtypeStruct((p_rows, d), F32),
        compiler_params=_cparams("parallel", "arbitrary"),
        name="moe_down",
    )(meta, a, wd)


def _moe_combine_kernel(pos_ref, ys_ref, route_ref, x_ref, mod_ref, g_ref, *rest, rows, n_lat_tiles):
    if n_lat_tiles is None:
        o_ref, buf_ref, sem_ref = rest
    else:
        lat_ref, ctx_ref, buf_ref, sem_ref = rest
    i = pl.program_id(0)
    n = pl.num_programs(0)

    def issue(tile, slot):
        def body(r, carry):
            for s in range(TOP_K):
                pltpu.make_async_copy(ys_ref.at[pl.ds(pos_ref[TOP_K * (tile * rows + r) + s], 1)],
                                      buf_ref.at[slot, s, pl.ds(r, 1)], sem_ref.at[slot]).start()
            return carry
        lax.fori_loop(0, rows, body, 0, unroll=DMA_ISSUE_UNROLL)

    @pl.when(i == 0)
    def _():
        issue(0, 0)

    @pl.when(i + 1 < n)
    def _():
        issue(i + 1, (i + 1) % 2)

    slot = i % 2
    for s in range(TOP_K):
        pltpu.make_async_copy(ys_ref.at[pl.ds(0, rows)], buf_ref.at[slot, s], sem_ref.at[slot]).wait()
    route = route_ref[...]
    y = None
    for s in range(TOP_K):
        term = route[:, TOP_K + s:TOP_K + s + 1] * buf_ref[slot, s]
        y = term if y is None else y + term
    res = x_ref[...] + mod_ref[GATE2:GATE2 + 1, :] * _rms(y, g_ref[...])
    if n_lat_tiles is None:
        o_ref[...] = res
    else:
        @pl.when(i < n_lat_tiles)
        def _():
            lat_ref[...] = res

        @pl.when(i >= n_lat_tiles)
        def _():
            ctx_ref[...] = res


def _moe_combine_call(ys, pos, route, x, mod, g, mod_row, rows, split_rows):
    t, d = x.shape
    row = pl.BlockSpec((rows, d), lambda i, pos: (i, 0))
    if split_rows is None:
        nl = None
        out_specs, out_shape = row, jax.ShapeDtypeStruct((t, d), F32)
    else:
        nl = split_rows // rows
        out_specs = [pl.BlockSpec((rows, d), lambda i, pos: (jnp.minimum(i, nl - 1), 0)),
                     pl.BlockSpec((rows, d), lambda i, pos: (jnp.maximum(i - nl, 0), 0))]
        out_shape = [jax.ShapeDtypeStruct((split_rows, d), F32),
                     jax.ShapeDtypeStruct((t - split_rows, d), F32)]
    grid_spec = pltpu.PrefetchScalarGridSpec(
        num_scalar_prefetch=1,
        grid=(t // rows,),
        in_specs=[pl.BlockSpec(memory_space=pl.ANY),
                  pl.BlockSpec((rows, route.shape[1]), lambda i, pos: (i, 0)),
                  row,
                  pl.BlockSpec((None, 6, d), lambda i, pos: (mod_row(i * rows), 0, 0)),
                  pl.BlockSpec((1, d), lambda i, pos: (0, 0))],
        out_specs=out_specs,
        scratch_shapes=[pltpu.VMEM((2, TOP_K, rows, d), F32), pltpu.SemaphoreType.DMA((2,))],
    )
    return pl.pallas_call(
        functools.partial(_moe_combine_kernel, rows=rows, n_lat_tiles=nl),
        grid_spec=grid_spec,
        out_shape=out_shape,
        compiler_params=_cparams("arbitrary"),
        name="moe_combine",
    )(pos, ys, route, x, mod, g)


def kernel(x_prompt, x_sample, cache_k, cache_v, c, c_ctx, w_ada, b_ada, norm_g, w_in, b_in, pool_w, pool_scale, rpb, conv_w, conv_b, conv_ln_g, conv_ln_b, w_branch, w_out, w_gate_d, w_up_d, w_down_d, w_router, b_router, w_gate_e, w_up_e, w_down_e):
    nbp, seq, d = x_prompt.shape
    nbs, dec_seq, _ = x_sample.shape
    depth = w_ada.shape[0]
    bw = d // N_BRANCH
    hd = bw // NA_HEADS
    ns_rows = nbs * dec_seq
    np_rows = nbp * seq
    assert dec_seq & (dec_seq - 1) == 0 and seq & (seq - 1) == 0
    assert nbs + 1 <= MOD_ROWS

    t_rows = ns_rows + np_rows
    tm_wide = max(tm for tm in (2048, 1024, 512) if dec_seq % tm == 0 and np_rows % tm == 0)
    tm_mid = min(tm_wide, 1024)
    tm_moe = 512
    tile_seq = min(256, seq)
    tn = 512
    assert t_rows % tm_wide == 0 and seq % tile_seq == 0

    def mod_row(row0):
        return jnp.where(row0 < ns_rows, row0 // dec_seq, nbs)

    cvec =jnp.zeros((MOD_ROWS, d), F32).at[:nbs].set(c).at[nbs].set(c_ctx)
    mods = _ada_call(cvec, w_ada, b_ada).reshape(depth, MOD_ROWS, 6, d)
    cc, sc = _channel_dft_mats(bw, FNET_GROUPS)

    pool_col, conv_col, main_cols = 1, 5, 7 * bw
    q_col = 2 * bw // hd

    def gain(l, n):
        return norm_g[l, n][None, :]

    x, h = _entry_call(x_sample.reshape(ns_rows, d), x_prompt.reshape(np_rows, d), mods[0], gain(0, 0),
                       mod_row, 512)
    w_in_bf16 = w_in.astype(BF16)
    new_k, new_v = [], []
    y_split = None
    for l in range(depth):
        mod = mods[l]
        last = l + 1 == depth
        proj, gates = _in_call(h, w_in_bf16, b_in[l][None, :], l, main_cols, tm_wide, tn)

        f_lat = _fourier_call(proj, 0, nbs, dec_seq, bw, cc, sc)
        f_ctx = _fourier_call(proj, ns_rows, nbp, seq, bw, cc, sc)
        o_p = _pool_call(proj, pool_w[l].astype(BF16), pool_scale[l][None, :], pool_col, bw,
                         tile_seq, ns_rows, dec_seq, seq)
        o_c = _conv_call(proj, conv_w[l], conv_b[l][None, :], conv_ln_g[l][None, :],
                         conv_ln_b[l][None, :], conv_col, bw, tile_seq, ns_rows, dec_seq, seq)
        a_lat = _na_attn_call(proj, cache_k, cache_v, l, rpb[l], nbs, dec_seq, q_col, hd)
        a_ctx, k_ctx, v_ctx = _ctx_attn_call(proj, ns_rows, nbp, seq, q_col, hd)
        new_k.append(k_ctx)
        new_v.append(v_ctx)

        merged = _merge_call(f_lat, f_ctx, o_p, a_lat, a_ctx, o_c, gates, w_branch[l].astype(BF16),
                             tm_mid, tn)
        nxt = None if last else (mods[l + 1], gain(l + 1, 0), (SCALE1, SHIFT1))
        i = l // 2
        if l % 2 == 1:
            x = _out_call(merged, w_out[l].astype(BF16), x, mod, gain(l, 1), GATE1, None,
                          mod_row, tm_mid)
            ne = w_router.shape[2]
            route, h_packed = _router_call(x, mod, gain(l, 2), w_router[i], b_router[i], mod_row, 512)
            src, pos, meta = _routing_tables(route, ne, tm_moe)
            xs = _gather_rows_call(h_packed, src, 256)
            hmid = _moe_up_call(xs, meta, w_gate_e[i], w_up_e[i], tm_moe, 2 * tn)
            ys = _moe_down_call(hmid, meta, w_down_e[i].astype(BF16), tm_moe, 1)
            out = _moe_combine_call(ys, pos, route, x, mod, gain(l, 3), mod_row, 256,
                                    ns_rows if last else None)
            if last:
                y_split = out
            else:
                x = out
                h = _norm_call(x, nxt[0], nxt[1], *nxt[2], mod_row, tm_mid)
        else:
            x, h2 = _out_call(merged, w_out[l].astype(BF16), x, mod, gain(l, 1), GATE1,
                              (mod, gain(l, 2), (SCALE2, SHIFT2)), mod_row, tm_mid)
            hmid = _ffn_up_call(h2, w_gate_d[i].astype(BF16), w_up_d[i].astype(BF16), tm_wide, tn)
            if last:
                x = _out_call(hmid, w_down_d[i].astype(BF16), x, mod, gain(l, 3), GATE2, None,
                              mod_row, tm_mid)
            else:
                x, h = _out_call(hmid, w_down_d[i].astype(BF16), x, mod, gain(l, 3), GATE2, nxt,
                                 mod_row, tm_mid)

    if y_split is None:
        y_split = x[:ns_rows], x[ns_rows:]
    y_sample = y_split[0].reshape(nbs, dec_seq, d)
    y_prompt = y_split[1].reshape(nbp, seq, d)
    return y_prompt, y_sample, jnp.stack(new_k, axis=1), jnp.stack(new_v, axis=1)
```

```python
import functools

import numpy as np
import jax
import jax.numpy as jnp
from jax import lax
from jax.experimental import pallas as pl
from jax.experimental.pallas import tpu as pltpu

F32 = jnp.float32
BF16 = jnp.bfloat16

N_BRANCH = 4
FNET_GROUPS = 4
POOL_WINDOWS = (2, 4, 8, 16)
NA_HEADS = 4
NA_WIN_ROWS = 8
NA_WIN_COLS = 16
GRID_W = 64
CONV_K = 31
TOP_K = 2
RMS_EPS = 1e-6
LN_EPS = 1e-5
MASK_VALUE = -1e30
MOD_ROWS = 8
POOL_HALO = 8
CONV_HALO = 16
NA_Q_ROWS = 8
NA_K_ROWS = 16
DMA_ISSUE_UNROLL = 8
VMEM_LIMIT_MIB = 48
VMEM_LIMIT_WIDE_MIB = 56

SHIFT1, SCALE1, GATE1, SHIFT2, SCALE2, GATE2 = range(6)


def _cparams(*sem, vmem_mib=VMEM_LIMIT_MIB):
    return pltpu.CompilerParams(dimension_semantics=sem, vmem_limit_bytes=vmem_mib * 1024 * 1024)


def _sigmoid(x):
    return 0.5 * jnp.tanh(0.5 * x) + 0.5


def _silu(x):
    return x * _sigmoid(x)


def _rms(x, g):
    return x * lax.rsqrt(jnp.mean(x * x, axis=-1, keepdims=True) + RMS_EPS) * g


def _mod_norm(x, g, mod_ref, scale_row, shift_row):
    return (_rms(x, g) * (1.0 + mod_ref[scale_row:scale_row + 1, :])
            + mod_ref[shift_row:shift_row + 1, :])


def _nt_dot(a, b):
    return lax.dot_general(a, b, (((1,), (1,)), ((), ())), preferred_element_type=F32)


def _ada_kernel(c_ref, w_ref, b_ref, o_ref):
    s = _silu(c_ref[...]).astype(BF16)
    o_ref[...] = jnp.dot(s, w_ref[...].astype(BF16), preferred_element_type=F32) + b_ref[...]


def _ada_call(cvec, w_ada, b_ada):
    depth, d, n = w_ada.shape
    tn = 512
    return pl.pallas_call(
        _ada_kernel,
        grid=(depth, n // tn),
        in_specs=[pl.BlockSpec((MOD_ROWS, d), lambda l, j: (0, 0)),
                  pl.BlockSpec((None, d, tn), lambda l, j: (l, 0, j)),
                  pl.BlockSpec((None, 1, tn), lambda l, j: (l, 0, j))],
        out_specs=pl.BlockSpec((None, MOD_ROWS, tn), lambda l, j: (l, 0, j)),
        out_shape=jax.ShapeDtypeStruct((depth, MOD_ROWS, n), F32),
        compiler_params=_cparams("parallel", "parallel"),
        name="adaln",
    )(cvec, w_ada, b_ada.reshape(depth, 1, n))


def _entry_kernel(lat_ref, ctx_ref, mod_ref, g_ref, x_ref, h_ref, *, n_lat_tiles):
    x = jnp.where(pl.program_id(0) < n_lat_tiles, lat_ref[...], ctx_ref[...])
    x_ref[...] = x
    h_ref[...] = _mod_norm(x, g_ref[...], mod_ref, SCALE1, SHIFT1).astype(h_ref.dtype)


def _entry_call(x_lat, x_ctx, mod, g, mod_row, tm):
    d = x_lat.shape[1]
    nl = x_lat.shape[0] // tm
    t = x_lat.shape[0] + x_ctx.shape[0]
    row = pl.BlockSpec((tm, d), lambda i: (i, 0))
    return pl.pallas_call(
        functools.partial(_entry_kernel, n_lat_tiles=nl),
        grid=(t // tm,),
        in_specs=[pl.BlockSpec((tm, d), lambda i: (jnp.minimum(i, nl - 1), 0)),
                  pl.BlockSpec((tm, d), lambda i: (jnp.maximum(i - nl, 0), 0)),
                  pl.BlockSpec((None, 6, d), lambda i: (mod_row(i * tm), 0, 0)),
                  pl.BlockSpec((1, d), lambda i: (0, 0))],
        out_specs=[row, row],
        out_shape=[jax.ShapeDtypeStruct((t, d), F32), jax.ShapeDtypeStruct((t, d), BF16)],
        compiler_params=_cparams("parallel"),
        name="entry_norm",
    )(x_lat, x_ctx, mod, g)


def _norm_kernel(x_ref, mod_ref, g_ref, o_ref, *, scale_row, shift_row):
    o_ref[...] = _mod_norm(x_ref[...], g_ref[...], mod_ref, scale_row, shift_row).astype(o_ref.dtype)


def _norm_call(x, mod, g, scale_row, shift_row, mod_row, tm):
    t, d = x.shape
    return pl.pallas_call(
        functools.partial(_norm_kernel, scale_row=scale_row, shift_row=shift_row),
        grid=(t // tm,),
        in_specs=[pl.BlockSpec((tm, d), lambda i: (i, 0)),
                  pl.BlockSpec((None, 6, d), lambda i: (mod_row(i * tm), 0, 0)),
                  pl.BlockSpec((1, d), lambda i: (0, 0))],
        out_specs=pl.BlockSpec((tm, d), lambda i: (i, 0)),
        out_shape=jax.ShapeDtypeStruct((t, d), BF16),
        compiler_params=_cparams("parallel"),
        name="mod_norm",
    )(x, mod, g)


def _in_kernel(h_ref, w_ref, b_ref, main_ref, gate_ref, *, n_main):
    j = pl.program_id(1)
    y = jnp.dot(h_ref[...], w_ref[...], preferred_element_type=F32) + b_ref[...]

    @pl.when(j < n_main)
    def _():
        main_ref[...] = y

    @pl.when(j >= n_main)
    def _():
        gate_ref[...] = _sigmoid(y).astype(gate_ref.dtype)


def _in_call(h, w, b, layer, main_cols, tm, tn):
    t, d = h.shape
    n = w.shape[2]
    n_main = main_cols // tn
    return pl.pallas_call(
        functools.partial(_in_kernel, n_main=n_main),
        grid=(t // tm, n // tn),
        in_specs=[pl.BlockSpec((tm, d), lambda i, j: (i, 0)),
                  pl.BlockSpec((None, d, tn), lambda i, j: (layer, 0, j)),
                  pl.BlockSpec((1, tn), lambda i, j: (0, j))],
        out_specs=[pl.BlockSpec((tm, tn), lambda i, j: (i, jnp.minimum(j, n_main - 1))),
                   pl.BlockSpec((tm, tn), lambda i, j: (i, jnp.maximum(j - n_main, 0)))],
        out_shape=[jax.ShapeDtypeStruct((t, main_cols), F32),
                   jax.ShapeDtypeStruct((t, n - main_cols), BF16)],
        compiler_params=_cparams("parallel", "arbitrary"),
        name="in_proj",
    )(h, w, b)


def _dft_mats(n):
    scale = 1.0 / np.sqrt(n)
    j = jnp.arange(n, dtype=jnp.int32)
    if n <= 1024:
        ang = ((j[:, None] * j[None, :]) % n).astype(F32) * (2.0 * np.pi / n)
        return (jnp.cos(ang) * scale).astype(BF16), (jnp.sin(ang) * scale).astype(BF16)
    base = 64
    hi = n // base
    k1 = jnp.arange(hi, dtype=jnp.int32)
    k0 = jnp.arange(base, dtype=jnp.int32)
    a = ((j[:, None] * k1[None, :]) % hi).astype(F32) * (2.0 * np.pi / hi)
    b = ((j[:, None] * k0[None, :]) % n).astype(F32) * (2.0 * np.pi / n)
    ca, sa = jnp.cos(a)[:, :, None], jnp.sin(a)[:, :, None]
    cb, sb = jnp.cos(b)[:, None, :] * scale, jnp.sin(b)[:, None, :] * scale
    c = (ca * cb - sa * sb).reshape(n, n)
    s = (sa * cb + ca * sb).reshape(n, n)
    return c.astype(BF16), s.astype(BF16)


def _channel_dft_mats(width, groups):
    gw = width // groups
    k = np.arange(gw)
    ang = 2.0 * np.pi * ((k[:, None] * k[None, :]) % gw) / gw
    c = np.zeros((width, width), np.float32)
    s = np.zeros((width, width), np.float32)
    for g in range(groups):
        sl = slice(g * gw, (g + 1) * gw)
        c[sl, sl] = np.cos(ang) / np.sqrt(gw)
        s[sl, sl] = np.sin(ang) / np.sqrt(gw)
    return jnp.asarray(c, BF16), jnp.asarray(s, BF16)


def _fnet1_kernel(u_ref, cc_ref, sc_ref, vc_ref, vs_ref):
    u = u_ref[...].astype(BF16)
    vc_ref[...] = jnp.dot(u, cc_ref[...], preferred_element_type=F32).astype(BF16)
    vs_ref[...] = jnp.dot(u, sc_ref[...], preferred_element_type=F32).astype(BF16)


def _fnet2_kernel(c_ref, s_ref, vc_ref, vs_ref, o_ref, acc_ref):
    k = pl.program_id(2)

    @pl.when(k == 0)
    def _():
        acc_ref[...] = jnp.zeros_like(acc_ref)

    acc_ref[...] += (jnp.dot(c_ref[...], vc_ref[...], preferred_element_type=F32)
                     - jnp.dot(s_ref[...], vs_ref[...], preferred_element_type=F32))

    @pl.when(k == pl.num_programs(2) - 1)
    def _():
        o_ref[...] = acc_ref[...].astype(o_ref.dtype)


def _fourier_call(proj, row_base, nb, seq, bw, cc, sc):
    c_l, s_l = _dft_mats(seq)
    tm1 = min(seq, 512)
    nt1 = seq // tm1
    base1 = row_base // tm1
    vc, vs = pl.pallas_call(
        _fnet1_kernel,
        grid=(nb, nt1),
        in_specs=[pl.BlockSpec((tm1, bw), lambda b, t: (base1 + b * nt1 + t, 0)),
                  pl.BlockSpec((bw, bw), lambda b, t: (0, 0)),
                  pl.BlockSpec((bw, bw), lambda b, t: (0, 0))],
        out_specs=[pl.BlockSpec((tm1, bw), lambda b, t: (t, b)),
                   pl.BlockSpec((tm1, bw), lambda b, t: (t, b))],
        out_shape=[jax.ShapeDtypeStruct((seq, nb * bw), BF16)] * 2,
        compiler_params=_cparams("parallel", "parallel"),
        name="fnet_channels",
    )(proj, cc, sc)
    tm2 = min(seq, 1024)
    tk = min(seq, 2048)
    nt2 = seq // tm2
    return pl.pallas_call(
        _fnet2_kernel,
        grid=(nt2, nb, seq // tk),
        in_specs=[pl.BlockSpec((tm2, tk), lambda i, j, k: (i, k)),
                  pl.BlockSpec((tm2, tk), lambda i, j, k: (i, k)),
                  pl.BlockSpec((tk, bw), lambda i, j, k: (k, j)),
                  pl.BlockSpec((tk, bw), lambda i, j, k: (k, j))],
        out_specs=pl.BlockSpec((tm2, bw), lambda i, j, k: (j * nt2 + i, 0)),
        out_shape=jax.ShapeDtypeStruct((nb * seq, bw), BF16),
        scratch_shapes=[pltpu.VMEM((tm2, bw), F32)],
        compiler_params=_cparams("parallel", "parallel", "arbitrary"),
        name="fnet_positions",
    )(c_l, s_l, vc, vs)


def _seq_position(row0, ns_rows, dec_seq, seq):
    is_lat = row0 < ns_rows
    seq_len = jnp.where(is_lat, dec_seq, seq)
    pos0 = jnp.where(is_lat, row0 & (dec_seq - 1), row0 & (seq - 1))
    return seq_len, pos0


def _pool_kernel(prev_ref, cur_ref, nxt_ref, w_ref, sc_ref, o_ref, ext_ref, *,
                 tm, ns_rows, dec_seq, seq):
    seq_len, pos0 = _seq_position(pl.program_id(0) * tm, ns_rows, dec_seq, seq)
    h = POOL_HALO
    ext_ref[0:h, :] = jnp.where(pos0 == 0, 0.0, prev_ref[...])
    ext_ref[h:h + tm, :] = cur_ref[...]
    ext_ref[h + tm:2 * h + tm, :] = jnp.where(pos0 + tm == seq_len, 0.0, nxt_ref[...])
    t = pos0 + lax.broadcasted_iota(jnp.int32, (tm, 1), 0)
    gw = cur_ref.shape[1] // len(POOL_WINDOWS)
    for gi, win in enumerate(POOL_WINDOWS):
        lo = win // 2
        hi = win - lo
        cols = slice(gi * gw, (gi + 1) * gw)
        s = ext_ref[h - lo:h - lo + tm, cols]
        for j in range(1 - lo, hi):
            s = s + ext_ref[h + j:h + j + tm, cols]
        cnt = (jnp.minimum(t + hi, seq_len) - jnp.maximum(t - lo, 0)).astype(F32)
        dlt = s / cnt - cur_ref[:, cols]
        y = jnp.dot(dlt.astype(BF16), w_ref[gi], preferred_element_type=F32) * sc_ref[:, cols]
        o_ref[:, cols] = y.astype(o_ref.dtype)


def _pool_call(proj, pool_w, pool_scale, col_blk, bw, tm, ns_rows, dec_seq, seq):
    t = proj.shape[0]
    h = POOL_HALO
    r = tm // h
    last = t // h - 1
    gw = bw // len(POOL_WINDOWS)
    return pl.pallas_call(
        functools.partial(_pool_kernel, tm=tm, ns_rows=ns_rows, dec_seq=dec_seq, seq=seq),
        grid=(t // tm,),
        in_specs=[pl.BlockSpec((h, bw), lambda i: (jnp.maximum(i * r - 1, 0), col_blk)),
                  pl.BlockSpec((tm, bw), lambda i: (i, col_blk)),
                  pl.BlockSpec((h, bw), lambda i: (jnp.minimum((i + 1) * r, last), col_blk)),
                  pl.BlockSpec((len(POOL_WINDOWS), gw, gw), lambda i: (0, 0, 0)),
                  pl.BlockSpec((1, bw), lambda i: (0, 0))],
        out_specs=pl.BlockSpec((tm, bw), lambda i: (i, 0)),
        out_shape=jax.ShapeDtypeStruct((t, bw), BF16),
        scratch_shapes=[pltpu.VMEM((tm + 2 * h, bw), F32)],
        compiler_params=_cparams("parallel"),
        name="pool_mix",
    )(proj, proj, proj, pool_w, pool_scale)


def _conv_kernel(ap_ref, ac_ref, an_ref, gp_ref, gc_ref, gn_ref, w_ref, b_ref, lg_ref, lb_ref,
                 o_ref, ext_ref, y_ref, *, tm, ns_rows, dec_seq, seq):
    seq_len, pos0 = _seq_position(pl.program_id(0) * tm, ns_rows, dec_seq, seq)
    h = CONV_HALO
    ext_ref[0:h, :] = jnp.where(pos0 == 0, 0.0, ap_ref[...] * _sigmoid(gp_ref[...]))
    ext_ref[h:h + tm, :] = ac_ref[...] * _sigmoid(gc_ref[...])
    ext_ref[h + tm:2 * h + tm, :] = jnp.where(pos0 + tm == seq_len, 0.0,
                                              an_ref[...] * _sigmoid(gn_ref[...]))
    bw = ac_ref.shape[1]
    lanes = 128
    for c in range(bw // lanes):
        cols = slice(c * lanes, (c + 1) * lanes)
        acc = jnp.zeros((tm, lanes), F32)
        for k in range(CONV_K):
            off = h + k - CONV_K // 2
            acc = acc + ext_ref[off:off + tm, cols] * w_ref[k:k + 1, cols]
        y_ref[:, cols] = acc + b_ref[:, cols]
    y = y_ref[...]
    mu = jnp.mean(y, axis=-1, keepdims=True)
    var = jnp.mean(jnp.square(y - mu), axis=-1, keepdims=True)
    z = (y - mu) * lax.rsqrt(var + LN_EPS) * lg_ref[...] + lb_ref[...]
    o_ref[...] = _silu(z).astype(o_ref.dtype)


def _conv_call(proj, conv_w, conv_b, ln_g, ln_b, col_blk, bw, tm, ns_rows, dec_seq, seq):
    t = proj.shape[0]
    h = CONV_HALO
    r = tm // h
    last = t // h - 1
    prev = lambda i: jnp.maximum(i * r - 1, 0)
    nxt = lambda i: jnp.minimum((i + 1) * r, last)
    vec = pl.BlockSpec((1, bw), lambda i: (0, 0))
    return pl.pallas_call(
        functools.partial(_conv_kernel, tm=tm, ns_rows=ns_rows, dec_seq=dec_seq, seq=seq),
        grid=(t // tm,),
        in_specs=[pl.BlockSpec((h, bw), lambda i: (prev(i), col_blk)),
                  pl.BlockSpec((tm, bw), lambda i: (i, col_blk)),
                  pl.BlockSpec((h, bw), lambda i: (nxt(i), col_blk)),
                  pl.BlockSpec((h, bw), lambda i: (prev(i), col_blk + 1)),
                  pl.BlockSpec((tm, bw), lambda i: (i, col_blk + 1)),
                  pl.BlockSpec((h, bw), lambda i: (nxt(i), col_blk + 1)),
                  pl.BlockSpec((CONV_K, bw), lambda i: (0, 0)),
                  vec, vec, vec],
        out_specs=pl.BlockSpec((tm, bw), lambda i: (i, 0)),
        out_shape=jax.ShapeDtypeStruct((t, bw), BF16),
        scratch_shapes=[pltpu.VMEM((tm + 2 * h, bw), F32), pltpu.VMEM((tm, bw), F32)],
        compiler_params=_cparams("parallel"),
        name="conv_module",
    )(proj, proj, proj, proj, proj, proj, conv_w, conv_b, ln_g, ln_b)


def _ctx_attn_kernel(q_ref, k_ref, v_ref, o_ref, ko_ref, vo_ref, *, scale):
    k = k_ref[...]
    v = v_ref[...]
    s = _nt_dot(q_ref[...].astype(BF16), k.astype(BF16)) * scale
    p = jnp.exp(s - jnp.max(s, axis=-1, keepdims=True))
    l = jnp.sum(p, axis=-1, keepdims=True)
    o = jnp.dot(p.astype(BF16), v.astype(BF16), preferred_element_type=F32)
    o_ref[...] = (o / l).astype(o_ref.dtype)
    ko_ref[...] = k
    vo_ref[...] = v


def _ctx_attn_call(proj, row_base, nb, seq, q_col, hd):
    base = row_base // seq
    nh = NA_HEADS
    kv_spec = pl.BlockSpec((None, None, seq, hd), lambda b, h: (b, h, 0, 0))
    return pl.pallas_call(
        functools.partial(_ctx_attn_kernel, scale=hd ** -0.5),
        grid=(nb, nh),
        in_specs=[pl.BlockSpec((seq, hd), lambda b, h: (base + b, q_col + h)),
                  pl.BlockSpec((seq, hd), lambda b, h: (base + b, q_col + nh + h)),
                  pl.BlockSpec((seq, hd), lambda b, h: (base + b, q_col + 2 * nh + h))],
        out_specs=[pl.BlockSpec((seq, hd), lambda b, h: (b, h)), kv_spec, kv_spec],
        out_shape=[jax.ShapeDtypeStruct((nb * seq, nh * hd), BF16),
                   jax.ShapeDtypeStruct((nb, nh, seq, hd), F32),
                   jax.ShapeDtypeStruct((nb, nh, seq, hd), F32)],
        compiler_params=_cparams("parallel", "parallel"),
        name="context_attention",
    )(proj, proj, proj)


def _na_key_row0(kb, rows):
    lo = kb * NA_Q_ROWS - NA_WIN_ROWS // 2
    if isinstance(kb, (int, np.integer)):
        return int(np.clip(lo, 0, rows - NA_K_ROWS))
    return jnp.clip(lo, 0, rows - NA_K_ROWS)


def _na_bias_tables(rpb, rows):
    nh = rpb.shape[0]
    w = GRID_W
    c = np.arange(w)
    cs = np.clip(c - NA_WIN_COLS // 2, 0, w - NA_WIN_COLS)
    col_ok = (c[None, :] >= cs[:, None]) & (c[None, :] < cs[:, None] + NA_WIN_COLS)
    rel_col = np.clip(c[None, :] - c[:, None] + NA_WIN_COLS - 1, 0, 2 * NA_WIN_COLS - 2)
    planes = jnp.where(col_ok[None, None], rpb[:, :, rel_col], MASK_VALUE)
    planes = jnp.concatenate([planes, jnp.full((nh, 1, w, w), MASK_VALUE, rpb.dtype)], axis=1)
    masked_plane = 2 * NA_WIN_ROWS - 1
    variants, var_ids = [], []
    for kb in range(rows // NA_Q_ROWS):
        r = kb * NA_Q_ROWS + np.arange(NA_Q_ROWS)
        rs = np.clip(r - NA_WIN_ROWS // 2, 0, rows - NA_WIN_ROWS)
        kr = _na_key_row0(kb, rows) + np.arange(NA_K_ROWS)
        ok = (kr[None, :] >= rs[:, None]) & (kr[None, :] < rs[:, None] + NA_WIN_ROWS)
        assert ok.sum() == NA_Q_ROWS * NA_WIN_ROWS
        plane = np.where(ok, kr[None, :] - r[:, None] + NA_WIN_ROWS - 1, masked_plane)
        for vi, known in enumerate(variants):
            if np.array_equal(known, plane):
                var_ids.append(vi)
                break
        else:
            var_ids.append(len(variants))
            variants.append(plane)
    tbl = pl.pallas_call(
        functools.partial(_na_table_kernel, variants=[v.tolist() for v in variants]),
        grid=(nh,),
        in_specs=[pl.BlockSpec((None, masked_plane + 1, w, w), lambda h: (h, 0, 0, 0))],
        out_specs=pl.BlockSpec((None, len(variants), NA_Q_ROWS * w, NA_K_ROWS * w),
                               lambda h: (h, 0, 0, 0)),
        out_shape=jax.ShapeDtypeStruct((nh, len(variants), NA_Q_ROWS * w, NA_K_ROWS * w), rpb.dtype),
        compiler_params=_cparams("parallel"),
        name="na_bias_table",
    )(planes)
    return tbl, jnp.asarray(np.array(var_ids, np.int32))


def _na_table_kernel(planes_ref, o_ref, *, variants):
    w = GRID_W
    for v, plane_of in enumerate(variants):
        for qr in range(NA_Q_ROWS):
            for kr in range(0, NA_K_ROWS, 2):
                pair = jnp.concatenate([planes_ref[plane_of[qr][kr]], planes_ref[plane_of[qr][kr + 1]]], axis=1)
                o_ref[v, qr * w:(qr + 1) * w, kr * w:(kr + 2) * w] = pair


def _na_attn_kernel(var_ref, q_ref, k_ref, v_ref, kc_ref, vc_ref, bias_ref, o_ref, *, rows, scale):
    del var_ref
    nk = NA_K_ROWS * GRID_W
    start = pl.multiple_of(_na_key_row0(pl.program_id(2), rows) * GRID_W, 256)
    kw = k_ref[pl.ds(start, nk), :].astype(BF16)
    vw = v_ref[pl.ds(start, nk), :].astype(BF16)
    q = q_ref[...].astype(BF16)
    s_loc = _nt_dot(q, kw) * scale + bias_ref[...]
    s_ctx = _nt_dot(q, kc_ref[...].astype(BF16)) * scale
    m = jnp.maximum(jnp.max(s_loc, axis=-1, keepdims=True), jnp.max(s_ctx, axis=-1, keepdims=True))
    p_loc = jnp.exp(s_loc - m)
    p_ctx = jnp.exp(s_ctx - m)
    l = jnp.sum(p_loc, axis=-1, keepdims=True) + jnp.sum(p_ctx, axis=-1, keepdims=True)
    o = (jnp.dot(p_loc.astype(BF16), vw, preferred_element_type=F32)
         + jnp.dot(p_ctx.astype(BF16), vc_ref[...].astype(BF16), preferred_element_type=F32))
    o_ref[...] = (o / l).astype(o_ref.dtype)


def _na_attn_call(proj, cache_k, cache_v, layer, rpb, nb, dec_seq, q_col, hd):
    rows = dec_seq // GRID_W
    assert rows >= NA_K_ROWS and rows % NA_Q_ROWS == 0
    nh = NA_HEADS
    nq = NA_Q_ROWS * GRID_W
    nkb = rows // NA_Q_ROWS
    past = cache_k.shape[3]
    tbl, var_ids = _na_bias_tables(rpb, rows)
    ctx_spec = pl.BlockSpec((None, None, None, past, hd), lambda b, h, k, var: (b, layer, h, 0, 0))
    grid_spec = pltpu.PrefetchScalarGridSpec(
        num_scalar_prefetch=1,
        grid=(nb, nh, nkb),
        in_specs=[pl.BlockSpec((nq, hd), lambda b, h, k, var: (b * nkb + k, q_col + h)),
                  pl.BlockSpec((dec_seq, hd), lambda b, h, k, var: (b, q_col + nh + h)),
                  pl.BlockSpec((dec_seq, hd), lambda b, h, k, var: (b, q_col + 2 * nh + h)),
                  ctx_spec, ctx_spec,
                  pl.BlockSpec((None, None, nq, NA_K_ROWS * GRID_W),
                               lambda b, h, k, var: (h, var[k], 0, 0))],
        out_specs=pl.BlockSpec((nq, hd), lambda b, h, k, var: (b * nkb + k, h)),
    )
    return pl.pallas_call(
        functools.partial(_na_attn_kernel, rows=rows, scale=hd ** -0.5),
        grid_spec=grid_spec,
        out_shape=jax.ShapeDtypeStruct((nb * dec_seq, nh * hd), BF16),
        compiler_params=_cparams("parallel", "parallel", "arbitrary"),
        name="neighbourhood_attention",
    )(var_ids, proj, proj, proj, cache_k, cache_v, tbl)


N_MERGE_REFS = 2 * 3 + N_BRANCH + 1


def _merged_block(fl_ref, fc_ref, p_ref, al_ref, ac_ref, c_ref, g0_ref, g1_ref, g2_ref, g3_ref, w_ref, *,
                  n_lat_tiles):
    is_lat = pl.program_id(0) < n_lat_tiles
    f = jnp.where(is_lat, fl_ref[...], fc_ref[...])
    a = jnp.where(is_lat, al_ref[...], ac_ref[...])
    acc = None
    for n, (br, gt) in enumerate(((f, g0_ref), (p_ref[...], g1_ref), (a, g2_ref), (c_ref[...], g3_ref))):
        y = gt[...].astype(F32) * jnp.dot(br, w_ref[n], preferred_element_type=F32)
        acc = y if acc is None else acc + y
    return acc.astype(BF16)


def _out_kernel(*refs, tm, gate_row, next_rows, n_lat_tiles):
    if n_lat_tiles is None:
        lhs_refs, refs = refs[:1], refs[1:]
    else:
        lhs_refs, refs = refs[:N_MERGE_REFS], refs[N_MERGE_REFS:]
    w_ref, x_hbm, mod_ref, g_ref = refs[:4]
    rest = refs[4:]
    if next_rows is None:
        o_ref, xbuf_ref, sem = rest
    else:
        nmod_ref, ng_ref, o_ref, h_ref, xbuf_ref, sem = rest
    k = pl.program_id(1)
    x_copy = pltpu.make_async_copy(x_hbm.at[pl.ds(pl.program_id(0) * tm, tm)], xbuf_ref, sem)

    @pl.when(k == 0)
    def _():
        x_copy.start()
        o_ref[...] = jnp.zeros_like(o_ref)

    if n_lat_tiles is None:
        a = lhs_refs[0][...]
    else:
        a = _merged_block(*lhs_refs, n_lat_tiles=n_lat_tiles)
    o_ref[...] += jnp.dot(a, w_ref[...], preferred_element_type=F32)

    @pl.when(k == pl.num_programs(1) - 1)
    def _():
        x_copy.wait()
        xn = xbuf_ref[...] + mod_ref[gate_row:gate_row + 1, :] * _rms(o_ref[...], g_ref[...])
        o_ref[...] = xn
        if next_rows is not None:
            h_ref[...] = _mod_norm(xn, ng_ref[...], nmod_ref, *next_rows).astype(h_ref.dtype)


def _k_tile(kdim, cap):
    lanes = 128
    return max(t for t in range(lanes, cap + 1, lanes) if kdim % t == 0)


def _out_call(a, w, x, mod, g, gate_row, nxt, mod_row, tm, merge_tk=None):
    t, d = x.shape
    row = pl.BlockSpec((tm, d), lambda i, k: (i, 0))
    mod_spec = pl.BlockSpec((None, 6, d), lambda i, k: (mod_row(i * tm), 0, 0))
    vec = pl.BlockSpec((1, d), lambda i, k: (0, 0))
    if merge_tk is None:
        kdim = a.shape[1]
        tk = _k_tile(kdim, 1024)
        nl = None
        in_specs = [pl.BlockSpec((tm, tk), lambda i, k: (i, k))]
        args = [a]
    else:
        f_lat, f_ctx, o_p, a_lat, a_ctx, o_c, gates, w_branch = a
        kdim, tk = w.shape[0], merge_tk
        bw = o_p.shape[1]
        nk = kdim // tk
        nl = f_lat.shape[0] // tm
        br_spec = pl.BlockSpec((tm, bw), lambda i, k: (i, 0))
        lat_spec = pl.BlockSpec((tm, bw), lambda i, k: (jnp.minimum(i, nl - 1), 0))
        ctx_spec = pl.BlockSpec((tm, bw), lambda i, k: (jnp.maximum(i - nl, 0), 0))
        gate_specs = [pl.BlockSpec((tm, tk), functools.partial(lambda i, k, n: (i, n * nk + k), n=n))
                      for n in range(N_BRANCH)]
        in_specs = ([lat_spec, ctx_spec, br_spec, lat_spec, ctx_spec, br_spec] + gate_specs
                    + [pl.BlockSpec((N_BRANCH, bw, tk), lambda i, k: (0, 0, k))])
        args = [f_lat, f_ctx, o_p, a_lat, a_ctx, o_c] + [gates] * N_BRANCH + [w_branch]
        assert len(args) == N_MERGE_REFS
    in_specs += [pl.BlockSpec((tk, d), lambda i, k: (k, 0)),
                 pl.BlockSpec(memory_space=pl.ANY), mod_spec, vec]
    args += [w, x, mod, g]
    out_specs, out_shape = row, jax.ShapeDtypeStruct((t, d), F32)
    if nxt is not None:
        in_specs += [mod_spec, vec]
        args += [nxt[0], nxt[1]]
        out_specs = [row, row]
        out_shape = [out_shape, jax.ShapeDtypeStruct((t, d), BF16)]
    return pl.pallas_call(
        functools.partial(_out_kernel, tm=tm, gate_row=gate_row,
                          next_rows=None if nxt is None else nxt[2], n_lat_tiles=nl),
        grid=(t // tm, kdim // tk),
        in_specs=in_specs,
        out_specs=out_specs,
        out_shape=out_shape,
        scratch_shapes=[pltpu.VMEM((tm, d), F32), pltpu.SemaphoreType.DMA(())],
        compiler_params=_cparams("arbitrary", "arbitrary", vmem_mib=VMEM_LIMIT_WIDE_MIB),
        name="proj_norm_residual",
    )(*args)


def _ffn_up_kernel(h_ref, wg_ref, wu_ref, o_ref):
    h = h_ref[...]
    y = _silu(jnp.dot(h, wg_ref[...], preferred_element_type=F32)) * jnp.dot(
        h, wu_ref[...], preferred_element_type=F32)
    o_ref[...] = y.astype(o_ref.dtype)


def _ffn_up_call(h, wg, wu, tm, tn):
    t, d = h.shape
    f = wg.shape[1]
    w_spec = pl.BlockSpec((d, tn), lambda i, j: (0, j))
    return pl.pallas_call(
        _ffn_up_kernel,
        grid=(t // tm, f // tn),
        in_specs=[pl.BlockSpec((tm, d), lambda i, j: (i, 0)), w_spec, w_spec],
        out_specs=pl.BlockSpec((tm, tn), lambda i, j: (i, j)),
        out_shape=jax.ShapeDtypeStruct((t, f), BF16),
        compiler_params=_cparams("parallel", "parallel"),
        name="ffn_up",
    )(h, wg, wu)


def _split_bf16(x):
    hi = x.astype(BF16)
    return hi, (x - hi.astype(F32)).astype(BF16)


def _pack_bf16_pairs(h):
    n = h.shape[1] // 2
    hb = h.astype(BF16).astype(F32)
    hi = pltpu.bitcast(hb[:, :n], jnp.int32)
    lo = pltpu.bitcast(hb[:, n:], jnp.int32)
    return hi | lax.shift_right_logical(lo, 16)


def _unpack_bf16_pairs(w):
    hi = pltpu.bitcast(w & jnp.int32(-65536), F32).astype(BF16)
    lo = pltpu.bitcast(w << 16, F32).astype(BF16)
    return hi, lo


def _router_kernel(x_ref, mod_ref, g_ref, w_ref, b_ref, o_ref, hp_ref, *, n_experts):
    h = _mod_norm(x_ref[...], g_ref[...], mod_ref, SCALE2, SHIFT2)
    hp_ref[...] = _pack_bf16_pairs(h)
    h_hi, h_lo = _split_bf16(h)
    w_hi, w_lo = _split_bf16(w_ref[...])
    logits = (jnp.dot(h_hi, w_hi, preferred_element_type=F32)
              + jnp.dot(h_lo, w_hi, preferred_element_type=F32)
              + jnp.dot(h_hi, w_lo, preferred_element_type=F32)) + b_ref[...]
    lane = lax.broadcasted_iota(jnp.int32, logits.shape, 1).astype(F32)
    neg = -jnp.inf
    no_lane = float(logits.shape[1])
    logits = jnp.where(lane < n_experts, logits, neg)
    m1 = jnp.max(logits, axis=-1, keepdims=True)
    i1 = jnp.min(jnp.where(logits == m1, lane, no_lane), axis=-1, keepdims=True)
    rest = jnp.where(lane == i1, neg, logits)
    m2 = jnp.max(rest, axis=-1, keepdims=True)
    i2 = jnp.min(jnp.where(rest == m2, lane, no_lane), axis=-1, keepdims=True)
    e2 = jnp.exp(m2 - m1)
    den = 1.0 + e2
    o_ref[...] = (jnp.where(lane == 0.0, i1, 0.0) + jnp.where(lane == 1.0, i2, 0.0)
                  + jnp.where(lane == 2.0, 1.0 / den, 0.0) + jnp.where(lane == 3.0, e2 / den, 0.0))


def _router_call(x, mod, g, w_router, b_router, mod_row, tm):
    t, d = x.shape
    ne = w_router.shape[1]
    lanes = 128
    w = jnp.zeros((d, lanes), F32).at[:, :ne].set(w_router)
    b = jnp.zeros((1, lanes), F32).at[0, :ne].set(b_router)
    return pl.pallas_call(
        functools.partial(_router_kernel, n_experts=ne),
        grid=(t // tm,),
        in_specs=[pl.BlockSpec((tm, d), lambda i: (i, 0)),
                  pl.BlockSpec((None, 6, d), lambda i: (mod_row(i * tm), 0, 0)),
                  pl.BlockSpec((1, d), lambda i: (0, 0)),
                  pl.BlockSpec((d, lanes), lambda i: (0, 0)),
                  pl.BlockSpec((1, lanes), lambda i: (0, 0))],
        out_specs=[pl.BlockSpec((tm, lanes), lambda i: (i, 0)),
                   pl.BlockSpec((tm, d // 2), lambda i: (i, 0))],
        out_shape=[jax.ShapeDtypeStruct((t, lanes), F32),
                   jax.ShapeDtypeStruct((t, d // 2), jnp.int32)],
        compiler_params=_cparams("parallel"),
        name="router",
    )(x, mod, g, w, b)


def _routing_tables(route, n_experts, tm):
    t = route.shape[0]
    na = TOP_K * t
    p_rows = na + n_experts * tm
    nt = p_rows // tm
    e_flat = route[:, :TOP_K].astype(jnp.int32).reshape(na)
    onehot = (e_flat[:, None] == jnp.arange(n_experts, dtype=jnp.int32)[None, :]).astype(jnp.int32)
    csum = jnp.cumsum(onehot, axis=0)
    rank = jnp.sum((csum - onehot) * onehot, axis=1)
    counts = csum[-1]
    padded = ((counts + tm - 1) // tm) * tm
    ends = jnp.cumsum(padded)
    pos = (ends - padded)[e_flat] + rank
    src = jnp.zeros((p_rows,), jnp.int32).at[pos].set(jnp.arange(na, dtype=jnp.int32) // TOP_K)
    tile_expert = jnp.minimum(
        jnp.searchsorted(ends, jnp.arange(nt, dtype=jnp.int32) * tm, side="right"), n_experts - 1)
    meta = jnp.concatenate([tile_expert.astype(jnp.int32), (ends[-1:] // tm).astype(jnp.int32)])
    return src, pos.astype(jnp.int32), meta


def _gather_rows_kernel(idx_ref, src_ref, o_ref, buf_ref, sem_ref, *, rows):
    i = pl.program_id(0)
    n = pl.num_programs(0)

    def issue(tile, slot):
        def body(r, carry):
            pltpu.make_async_copy(src_ref.at[pl.ds(idx_ref[tile * rows + r], 1)],
                                  buf_ref.at[slot, pl.ds(r, 1)], sem_ref.at[slot]).start()
            return carry
        lax.fori_loop(0, rows, body, 0, unroll=DMA_ISSUE_UNROLL)

    @pl.when(i == 0)
    def _():
        issue(0, 0)

    @pl.when(i + 1 < n)
    def _():
        issue(i + 1, (i + 1) % 2)

    slot = i % 2
    pltpu.make_async_copy(src_ref.at[pl.ds(0, rows)], buf_ref.at[slot], sem_ref.at[slot]).wait()
    half = buf_ref.shape[2]
    hi, lo = _unpack_bf16_pairs(buf_ref[slot])
    o_ref[:, :half] = hi
    o_ref[:, half:] = lo


def _gather_rows_call(src, idx, rows):
    p_rows = idx.shape[0]
    half = src.shape[1]
    grid_spec = pltpu.PrefetchScalarGridSpec(
        num_scalar_prefetch=1,
        grid=(p_rows // rows,),
        in_specs=[pl.BlockSpec(memory_space=pl.ANY)],
        out_specs=pl.BlockSpec((rows, 2 * half), lambda i, idx: (i, 0)),
        scratch_shapes=[pltpu.VMEM((2, rows, half), src.dtype), pltpu.SemaphoreType.DMA((2,))],
    )
    return pl.pallas_call(
        functools.partial(_gather_rows_kernel, rows=rows),
        grid_spec=grid_spec,
        out_shape=jax.ShapeDtypeStruct((p_rows, 2 * half), BF16),
        compiler_params=_cparams("arbitrary"),
        name="moe_gather",
    )(idx, src)


def _moe_up_kernel(meta_ref, xs_ref, wg_ref, wu_ref, o_ref, wgb_ref, wub_ref, *, n_tiles):
    i = pl.program_id(1)
    used = i < meta_ref[n_tiles]
    fresh = jnp.logical_or(i == 0, meta_ref[i] != meta_ref[jnp.maximum(i - 1, 0)])

    @pl.when(jnp.logical_and(used, fresh))
    def _():
        wgb_ref[...] = wg_ref[...].astype(BF16)
        wub_ref[...] = wu_ref[...].astype(BF16)

    @pl.when(used)
    def _():
        h = xs_ref[...]
        y = _silu(jnp.dot(h, wgb_ref[...], preferred_element_type=F32)) * jnp.dot(
            h, wub_ref[...], preferred_element_type=F32)
        o_ref[...] = y.astype(o_ref.dtype)

    @pl.when(jnp.logical_not(used))
    def _():
        o_ref[...] = jnp.zeros_like(o_ref)


def _moe_up_call(xs, meta, wg, wu, tm, tn):
    p_rows, d = xs.shape
    f = wg.shape[2]
    nt = p_rows // tm
    w_spec = pl.BlockSpec((None, d, tn), lambda j, i, meta: (meta[i], 0, j))
    grid_spec = pltpu.PrefetchScalarGridSpec(
        num_scalar_prefetch=1,
        grid=(f // tn, nt),
        in_specs=[pl.BlockSpec((tm, d), lambda j, i, meta: (i, 0)), w_spec, w_spec],
        out_specs=pl.BlockSpec((None, tm, tn), lambda j, i, meta: (j, i, 0)),
        scratch_shapes=[pltpu.VMEM((d, tn), BF16), pltpu.VMEM((d, tn), BF16)],
    )
    return pl.pallas_call(
        functools.partial(_moe_up_kernel, n_tiles=nt),
        grid_spec=grid_spec,
        out_shape=jax.ShapeDtypeStruct((f // tn, p_rows, tn), BF16),
        compiler_params=_cparams("arbitrary", "arbitrary", vmem_mib=VMEM_LIMIT_WIDE_MIB),
        name="moe_up",
    )(meta, xs, wg, wu)


def _moe_down_kernel(meta_ref, a_ref, w_ref, o_ref, *, n_tiles):
    i = pl.program_id(0)

    @pl.when(pl.program_id(1) == 0)
    def _():
        o_ref[...] = jnp.zeros_like(o_ref)

    @pl.when(i < meta_ref[n_tiles])
    def _():
        nsub, _, tk = a_ref.shape
        acc = o_ref[...]
        for s in range(nsub):
            acc = acc + jnp.dot(a_ref[s], w_ref[s * tk:(s + 1) * tk, :], preferred_element_type=F32)
        o_ref[...] = acc


def _moe_down_call(a, meta, wd, tm, nsub):
    nk, p_rows, tk = a.shape
    d = wd.shape[2]
    nt = p_rows // tm
    grid_spec = pltpu.PrefetchScalarGridSpec(
        num_scalar_prefetch=1,
        grid=(nt, nk // nsub),
        in_specs=[pl.BlockSpec((nsub, tm, tk), lambda i, k, meta: (k, i, 0)),
                  pl.BlockSpec((None, nsub * tk, d), lambda i, k, meta: (meta[i], k, 0))],
        out_specs=pl.BlockSpec((tm, d), lambda i, k, meta: (i, 0)),
    )
    return pl.pallas_call(
        functools.partial(_moe_down_kernel, n_tiles=nt),
        grid_spec=grid_spec,
        out_shape=jax.ShapeDtypeStruct((p_rows, d), F32),
        compiler_params=_cparams("parallel", "arbitrary"),
        name="moe_down",
    )(meta, a, wd)


def _moe_combine_kernel(pos_ref, ys_ref, route_ref, x_ref, mod_ref, g_ref, *rest, rows, n_lat_tiles):
    if n_lat_tiles is None:
        o_ref, buf_ref, sem_ref = rest
    else:
        lat_ref, ctx_ref, buf_ref, sem_ref = rest
    i = pl.program_id(0)
    n = pl.num_programs(0)

    def issue(tile, slot):
        def body(r, carry):
            for s in range(TOP_K):
                pltpu.make_async_copy(ys_ref.at[pl.ds(pos_ref[TOP_K * (tile * rows + r) + s], 1)],
                                      buf_ref.at[slot, s, pl.ds(r, 1)], sem_ref.at[slot]).start()
            return carry
        lax.fori_loop(0, rows, body, 0, unroll=DMA_ISSUE_UNROLL)

    @pl.when(i == 0)
    def _():
        issue(0, 0)

    @pl.when(i + 1 < n)
    def _():
        issue(i + 1, (i + 1) % 2)

    slot = i % 2
    for s in range(TOP_K):
        pltpu.make_async_copy(ys_ref.at[pl.ds(0, rows)], buf_ref.at[slot, s], sem_ref.at[slot]).wait()
    route = route_ref[...]
    y = None
    for s in range(TOP_K):
        term = route[:, TOP_K + s:TOP_K + s + 1] * buf_ref[slot, s]
        y = term if y is None else y + term
    res = x_ref[...] + mod_ref[GATE2:GATE2 + 1, :] * _rms(y, g_ref[...])
    if n_lat_tiles is None:
        o_ref[...] = res
    else:
        @pl.when(i < n_lat_tiles)
        def _():
            lat_ref[...] = res

        @pl.when(i >= n_lat_tiles)
        def _():
            ctx_ref[...] = res


def _moe_combine_call(ys, pos, route, x, mod, g, mod_row, rows, split_rows):
    t, d = x.shape
    row = pl.BlockSpec((rows, d), lambda i, pos: (i, 0))
    if split_rows is None:
        nl = None
        out_specs, out_shape = row, jax.ShapeDtypeStruct((t, d), F32)
    else:
        nl = split_rows // rows
        out_specs = [pl.BlockSpec((rows, d), lambda i, pos: (jnp.minimum(i, nl - 1), 0)),
                     pl.BlockSpec((rows, d), lambda i, pos: (jnp.maximum(i - nl, 0), 0))]
        out_shape = [jax.ShapeDtypeStruct((split_rows, d), F32),
                     jax.ShapeDtypeStruct((t - split_rows, d), F32)]
    grid_spec = pltpu.PrefetchScalarGridSpec(
        num_scalar_prefetch=1,
        grid=(t // rows,),
        in_specs=[pl.BlockSpec(memory_space=pl.ANY),
                  pl.BlockSpec((rows, route.shape[1]), lambda i, pos: (i, 0)),
                  row,
                  pl.BlockSpec((None, 6, d), lambda i, pos: (mod_row(i * rows), 0, 0)),
                  pl.BlockSpec((1, d), lambda i, pos: (0, 0))],
        out_specs=out_specs,
        scratch_shapes=[pltpu.VMEM((2, TOP_K, rows, d), F32), pltpu.SemaphoreType.DMA((2,))],
    )
    return pl.pallas_call(
        functools.partial(_moe_combine_kernel, rows=rows, n_lat_tiles=nl),
        grid_spec=grid_spec,
        out_shape=out_shape,
        compiler_params=_cparams("arbitrary"),
        name="moe_combine",
    )(pos, ys, route, x, mod, g)


def kernel(x_prompt, x_sample, cache_k, cache_v, c, c_ctx, w_ada, b_ada, norm_g, w_in, b_in, pool_w, pool_scale, rpb, conv_w, conv_b, conv_ln_g, conv_ln_b, w_branch, w_out, w_gate_d, w_up_d, w_down_d, w_router, b_router, w_gate_e, w_up_e, w_down_e):
    nbp, seq, d = x_prompt.shape
    nbs, dec_seq, _ = x_sample.shape
    depth = w_ada.shape[0]
    bw = d // N_BRANCH
    hd = bw // NA_HEADS
    ns_rows = nbs * dec_seq
    np_rows = nbp * seq
    assert dec_seq & (dec_seq - 1) == 0 and seq & (seq - 1) == 0
    assert nbs + 1 <= MOD_ROWS

    t_rows = ns_rows + np_rows
    tm_wide = max(tm for tm in (2048, 1024, 512) if dec_seq % tm == 0 and np_rows % tm == 0)
    tm_mid = min(tm_wide, 1024)
    tm_merge = 512
    tm_moe = 512
    tile_seq = min(256, seq)
    tn = 512
    assert t_rows % tm_wide == 0 and seq % tile_seq == 0

    def mod_row(row0):
        return jnp.where(row0 < ns_rows, row0 // dec_seq, nbs)

    cvec =jnp.zeros((MOD_ROWS, d), F32).at[:nbs].set(c).at[nbs].set(c_ctx)
    mods = _ada_call(cvec, w_ada, b_ada).reshape(depth, MOD_ROWS, 6, d)
    cc, sc = _channel_dft_mats(bw, FNET_GROUPS)

    pool_col, conv_col, main_cols = 1, 5, 7 * bw
    q_col = 2 * bw // hd

    def gain(l, n):
        return norm_g[l, n][None, :]

    x, h = _entry_call(x_sample.reshape(ns_rows, d), x_prompt.reshape(np_rows, d), mods[0], gain(0, 0),
                       mod_row, 512)
    w_in_bf16 = w_in.astype(BF16)
    new_k, new_v = [], []
    y_split = None
    for l in range(depth):
        mod = mods[l]
        last = l + 1 == depth
        proj, gates = _in_call(h, w_in_bf16, b_in[l][None, :], l, main_cols, tm_wide, tn)

        f_lat = _fourier_call(proj, 0, nbs, dec_seq, bw, cc, sc)
        f_ctx = _fourier_call(proj, ns_rows, nbp, seq, bw, cc, sc)
        o_p = _pool_call(proj, pool_w[l].astype(BF16), pool_scale[l][None, :], pool_col, bw,
                         tile_seq, ns_rows, dec_seq, seq)
        o_c = _conv_call(proj, conv_w[l], conv_b[l][None, :], conv_ln_g[l][None, :],
                         conv_ln_b[l][None, :], conv_col, bw, tile_seq, ns_rows, dec_seq, seq)
        a_lat = _na_attn_call(proj, cache_k, cache_v, l, rpb[l], nbs, dec_seq, q_col, hd)
        a_ctx, k_ctx, v_ctx = _ctx_attn_call(proj, ns_rows, nbp, seq, q_col, hd)
        new_k.append(k_ctx)
        new_v.append(v_ctx)

        merged = (f_lat, f_ctx, o_p, a_lat, a_ctx, o_c, gates, w_branch[l].astype(BF16))
        nxt = None if last else (mods[l + 1], gain(l + 1, 0), (SCALE1, SHIFT1))
        i = l // 2
        if l % 2 == 1:
            x = _out_call(merged, w_out[l].astype(BF16), x, mod, gain(l, 1), GATE1, None,
                          mod_row, tm_merge, tn)
            ne = w_router.shape[2]
            route, h_packed = _router_call(x, mod, gain(l, 2), w_router[i], b_router[i], mod_row, 512)
            src, pos, meta = _routing_tables(route, ne, tm_moe)
            xs = _gather_rows_call(h_packed, src, 256)
            hmid = _moe_up_call(xs, meta, w_gate_e[i], w_up_e[i], tm_moe, 2 * tn)
            ys = _moe_down_call(hmid, meta, w_down_e[i].astype(BF16), tm_moe, 1)
            out = _moe_combine_call(ys, pos, route, x, mod, gain(l, 3), mod_row, 256,
                                    ns_rows if last else None)
            if last:
                y_split = out
            else:
                x = out
                h = _norm_call(x, nxt[0], nxt[1], *nxt[2], mod_row, tm_mid)
        else:
            x, h2 = _out_call(merged, w_out[l].astype(BF16), x, mod, gain(l, 1), GATE1,
                              (mod, gain(l, 2), (SCALE2, SHIFT2)), mod_row, tm_merge, tn)
            hmid = _ffn_up_call(h2, w_gate_d[i].astype(BF16), w_up_d[i].astype(BF16), tm_wide, tn)
            if last:
                x = _out_call(hmid, w_down_d[i].astype(BF16), x, mod, gain(l, 3), GATE2, None,
                              mod_row, tm_mid)
            else:
                x, h = _out_call(hmid, w_down_d[i].astype(BF16), x, mod, gain(l, 3), GATE2, nxt,
                                 mod_row, tm_mid)

    if y_split is None:
        y_split = x[:ns_rows], x[ns_rows:]
    y_sample = y_split[0].reshape(nbs, dec_seq, d)
    y_prompt = y_split[1].reshape(nbp, seq, d)
    return y_prompt, y_sample, jnp.stack(new_k, axis=1), jnp.stack(new_v, axis=1)
```

```python
import functools

import numpy as np
import jax
import jax.numpy as jnp
from jax import lax
from jax.experimental import pallas as pl
from jax.experimental.pallas import tpu as pltpu

F32 = jnp.float32
BF16 = jnp.bfloat16

N_BRANCH = 4
FNET_GROUPS = 4
POOL_WINDOWS = (2, 4, 8, 16)
NA_HEADS = 4
NA_WIN_ROWS = 8
NA_WIN_COLS = 16
GRID_W = 64
CONV_K = 31
TOP_K = 2
RMS_EPS = 1e-6
LN_EPS = 1e-5
MASK_VALUE = -1e30
MOD_ROWS = 8
POOL_HALO = 8
CONV_HALO = 16
NA_Q_ROWS = 8
NA_K_ROWS = 16
DMA_ISSUE_UNROLL = 8
VMEM_LIMIT_MIB = 48
VMEM_LIMIT_WIDE_MIB = 56

SHIFT1, SCALE1, GATE1, SHIFT2, SCALE2, GATE2 = range(6)


def _cparams(*sem, vmem_mib=VMEM_LIMIT_MIB):
    return pltpu.CompilerParams(dimension_semantics=sem, vmem_limit_bytes=vmem_mib * 1024 * 1024)


def _sigmoid(x):
    return 0.5 * jnp.tanh(0.5 * x) + 0.5


def _silu(x):
    return x * _sigmoid(x)


def _rms(x, g):
    return x * lax.rsqrt(jnp.mean(x * x, axis=-1, keepdims=True) + RMS_EPS) * g


def _mod_norm(x, g, mod_ref, scale_row, shift_row):
    return (_rms(x, g) * (1.0 + mod_ref[scale_row:scale_row + 1, :])
            + mod_ref[shift_row:shift_row + 1, :])


def _nt_dot(a, b):
    return lax.dot_general(a, b, (((1,), (1,)), ((), ())), preferred_element_type=F32)


def _ada_kernel(c_ref, w_ref, b_ref, o_ref):
    s = _silu(c_ref[...]).astype(BF16)
    o_ref[...] = jnp.dot(s, w_ref[...].astype(BF16), preferred_element_type=F32) + b_ref[...]


def _ada_call(cvec, w_ada, b_ada):
    depth, d, n = w_ada.shape
    tn = 512
    return pl.pallas_call(
        _ada_kernel,
        grid=(depth, n // tn),
        in_specs=[pl.BlockSpec((MOD_ROWS, d), lambda l, j: (0, 0)),
                  pl.BlockSpec((None, d, tn), lambda l, j: (l, 0, j)),
                  pl.BlockSpec((None, 1, tn), lambda l, j: (l, 0, j))],
        out_specs=pl.BlockSpec((None, MOD_ROWS, tn), lambda l, j: (l, 0, j)),
        out_shape=jax.ShapeDtypeStruct((depth, MOD_ROWS, n), F32),
        compiler_params=_cparams("parallel", "parallel"),
        name="adaln",
    )(cvec, w_ada, b_ada.reshape(depth, 1, n))


def _entry_kernel(lat_ref, ctx_ref, mod_ref, g_ref, x_ref, h_ref, *, n_lat_tiles):
    x = jnp.where(pl.program_id(0) < n_lat_tiles, lat_ref[...], ctx_ref[...])
    x_ref[...] = x
    h_ref[...] = _mod_norm(x, g_ref[...], mod_ref, SCALE1, SHIFT1).astype(h_ref.dtype)


def _entry_call(x_lat, x_ctx, mod, g, mod_row, tm):
    d = x_lat.shape[1]
    nl = x_lat.shape[0] // tm
    t = x_lat.shape[0] + x_ctx.shape[0]
    row = pl.BlockSpec((tm, d), lambda i: (i, 0))
    return pl.pallas_call(
        functools.partial(_entry_kernel, n_lat_tiles=nl),
        grid=(t // tm,),
        in_specs=[pl.BlockSpec((tm, d), lambda i: (jnp.minimum(i, nl - 1), 0)),
                  pl.BlockSpec((tm, d), lambda i: (jnp.maximum(i - nl, 0), 0)),
                  pl.BlockSpec((None, 6, d), lambda i: (mod_row(i * tm), 0, 0)),
                  pl.BlockSpec((1, d), lambda i: (0, 0))],
        out_specs=[row, row],
        out_shape=[jax.ShapeDtypeStruct((t, d), F32), jax.ShapeDtypeStruct((t, d), BF16)],
        compiler_params=_cparams("parallel"),
        name="entry_norm",
    )(x_lat, x_ctx, mod, g)


def _norm_kernel(x_ref, mod_ref, g_ref, o_ref, *, scale_row, shift_row):
    o_ref[...] = _mod_norm(x_ref[...], g_ref[...], mod_ref, scale_row, shift_row).astype(o_ref.dtype)


def _norm_call(x, mod, g, scale_row, shift_row, mod_row, tm):
    t, d = x.shape
    return pl.pallas_call(
        functools.partial(_norm_kernel, scale_row=scale_row, shift_row=shift_row),
        grid=(t // tm,),
        in_specs=[pl.BlockSpec((tm, d), lambda i: (i, 0)),
                  pl.BlockSpec((None, 6, d), lambda i: (mod_row(i * tm), 0, 0)),
                  pl.BlockSpec((1, d), lambda i: (0, 0))],
        out_specs=pl.BlockSpec((tm, d), lambda i: (i, 0)),
        out_shape=jax.ShapeDtypeStruct((t, d), BF16),
        compiler_params=_cparams("parallel"),
        name="mod_norm",
    )(x, mod, g)


def _in_kernel(h_ref, w_ref, b_ref, *rest, n_main):
    if len(rest) == 2:
        main_ref, gate_ref = rest
    else:
        cast_src_ref, main_ref, gate_ref, cast_dst_ref = rest
        cast_dst_ref[...] = cast_src_ref[...].astype(cast_dst_ref.dtype)
    j = pl.program_id(1)
    y = jnp.dot(h_ref[...], w_ref[...], preferred_element_type=F32) + b_ref[...]

    @pl.when(j < n_main)
    def _():
        main_ref[...] = y

    @pl.when(j >= n_main)
    def _():
        gate_ref[...] = _sigmoid(y).astype(gate_ref.dtype)


def _in_call(h, w, b, layer, main_cols, tm, tn, cast_src=None):
    t, d = h.shape
    n = w.shape[2]
    n_main = main_cols // tn
    nj = n // tn
    steps = (t // tm) * nj
    in_specs = [pl.BlockSpec((tm, d), lambda i, j: (i, 0)),
                pl.BlockSpec((None, d, tn), lambda i, j: (layer, 0, j)),
                pl.BlockSpec((1, tn), lambda i, j: (0, j))]
    out_specs = [pl.BlockSpec((tm, tn), lambda i, j: (i, jnp.minimum(j, n_main - 1))),
                 pl.BlockSpec((tm, tn), lambda i, j: (i, jnp.maximum(j - n_main, 0)))]
    out_shape = [jax.ShapeDtypeStruct((t, main_cols), F32),
                 jax.ShapeDtypeStruct((t, n - main_cols), BF16)]
    args = [h, w, b]
    if cast_src is not None:
        rows, cols = cast_src.shape
        bf16_rows = 16
        chunk = min(c for c in range(bf16_rows, rows + 1, bf16_rows)
                    if rows % c == 0 and rows // c <= steps)
        chunk_spec = pl.BlockSpec((chunk, cols), lambda i, j: (jnp.minimum(i * nj + j, rows // chunk - 1), 0))
        in_specs.append(chunk_spec)
        out_specs.append(chunk_spec)
        out_shape.append(jax.ShapeDtypeStruct((rows, cols), BF16))
        args.append(cast_src)
    return pl.pallas_call(
        functools.partial(_in_kernel, n_main=n_main),
        grid=(t // tm, nj),
        in_specs=in_specs,
        out_specs=out_specs,
        out_shape=out_shape,
        compiler_params=_cparams("arbitrary", "arbitrary"),
        name="in_proj",
    )(*args)


def _dft_mats(n):
    scale = 1.0 / np.sqrt(n)
    j = jnp.arange(n, dtype=jnp.int32)
    if n <= 1024:
        ang = ((j[:, None] * j[None, :]) % n).astype(F32) * (2.0 * np.pi / n)
        return (jnp.cos(ang) * scale).astype(BF16), (jnp.sin(ang) * scale).astype(BF16)
    base = 64
    hi = n // base
    k1 = jnp.arange(hi, dtype=jnp.int32)
    k0 = jnp.arange(base, dtype=jnp.int32)
    a = ((j[:, None] * k1[None, :]) % hi).astype(F32) * (2.0 * np.pi / hi)
    b = ((j[:, None] * k0[None, :]) % n).astype(F32) * (2.0 * np.pi / n)
    ca, sa = jnp.cos(a)[:, :, None], jnp.sin(a)[:, :, None]
    cb, sb = jnp.cos(b)[:, None, :] * scale, jnp.sin(b)[:, None, :] * scale
    c = (ca * cb - sa * sb).reshape(n, n)
    s = (sa * cb + ca * sb).reshape(n, n)
    return c.astype(BF16), s.astype(BF16)


def _channel_dft_mats(width, groups):
    gw = width // groups
    k = np.arange(gw)
    ang = 2.0 * np.pi * ((k[:, None] * k[None, :]) % gw) / gw
    c = np.zeros((width, width), np.float32)
    s = np.zeros((width, width), np.float32)
    for g in range(groups):
        sl = slice(g * gw, (g + 1) * gw)
        c[sl, sl] = np.cos(ang) / np.sqrt(gw)
        s[sl, sl] = np.sin(ang) / np.sqrt(gw)
    return jnp.asarray(c, BF16), jnp.asarray(s, BF16)


def _fnet1_kernel(u_ref, cc_ref, sc_ref, vc_ref, vs_ref):
    u = u_ref[...].astype(BF16)
    vc_ref[...] = jnp.dot(u, cc_ref[...], preferred_element_type=F32).astype(BF16)
    vs_ref[...] = jnp.dot(u, sc_ref[...], preferred_element_type=F32).astype(BF16)


def _fnet2_kernel(c_ref, s_ref, vc_ref, vs_ref, o_ref, acc_ref):
    k = pl.program_id(2)

    @pl.when(k == 0)
    def _():
        acc_ref[...] = jnp.zeros_like(acc_ref)

    acc_ref[...] += (jnp.dot(c_ref[...], vc_ref[...], preferred_element_type=F32)
                     - jnp.dot(s_ref[...], vs_ref[...], preferred_element_type=F32))

    @pl.when(k == pl.num_programs(2) - 1)
    def _():
        o_ref[...] = acc_ref[...].astype(o_ref.dtype)


def _fourier_call(proj, row_base, nb, seq, bw, cc, sc):
    c_l, s_l = _dft_mats(seq)
    tm1 = min(seq, 512)
    nt1 = seq // tm1
    base1 = row_base // tm1
    vc, vs = pl.pallas_call(
        _fnet1_kernel,
        grid=(nb, nt1),
        in_specs=[pl.BlockSpec((tm1, bw), lambda b, t: (base1 + b * nt1 + t, 0)),
                  pl.BlockSpec((bw, bw), lambda b, t: (0, 0)),
                  pl.BlockSpec((bw, bw), lambda b, t: (0, 0))],
        out_specs=[pl.BlockSpec((tm1, bw), lambda b, t: (t, b)),
                   pl.BlockSpec((tm1, bw), lambda b, t: (t, b))],
        out_shape=[jax.ShapeDtypeStruct((seq, nb * bw), BF16)] * 2,
        compiler_params=_cparams("parallel", "parallel"),
        name="fnet_channels",
    )(proj, cc, sc)
    tm2 = min(seq, 1024)
    tk = min(seq, 2048)
    nt2 = seq // tm2
    return pl.pallas_call(
        _fnet2_kernel,
        grid=(nt2, nb, seq // tk),
        in_specs=[pl.BlockSpec((tm2, tk), lambda i, j, k: (i, k)),
                  pl.BlockSpec((tm2, tk), lambda i, j, k: (i, k)),
                  pl.BlockSpec((tk, bw), lambda i, j, k: (k, j)),
                  pl.BlockSpec((tk, bw), lambda i, j, k: (k, j))],
        out_specs=pl.BlockSpec((tm2, bw), lambda i, j, k: (j * nt2 + i, 0)),
        out_shape=jax.ShapeDtypeStruct((nb * seq, bw), BF16),
        scratch_shapes=[pltpu.VMEM((tm2, bw), F32)],
        compiler_params=_cparams("parallel", "parallel", "arbitrary"),
        name="fnet_positions",
    )(c_l, s_l, vc, vs)


def _seq_position(row0, ns_rows, dec_seq, seq):
    is_lat = row0 < ns_rows
    seq_len = jnp.where(is_lat, dec_seq, seq)
    pos0 = jnp.where(is_lat, row0 & (dec_seq - 1), row0 & (seq - 1))
    return seq_len, pos0


def _pool_kernel(prev_ref, cur_ref, nxt_ref, w_ref, sc_ref, o_ref, ext_ref, *,
                 tm, ns_rows, dec_seq, seq):
    seq_len, pos0 = _seq_position(pl.program_id(0) * tm, ns_rows, dec_seq, seq)
    h = POOL_HALO
    ext_ref[0:h, :] = jnp.where(pos0 == 0, 0.0, prev_ref[...])
    ext_ref[h:h + tm, :] = cur_ref[...]
    ext_ref[h + tm:2 * h + tm, :] = jnp.where(pos0 + tm == seq_len, 0.0, nxt_ref[...])
    t = pos0 + lax.broadcasted_iota(jnp.int32, (tm, 1), 0)
    gw = cur_ref.shape[1] // len(POOL_WINDOWS)
    for gi, win in enumerate(POOL_WINDOWS):
        lo = win // 2
        hi = win - lo
        cols = slice(gi * gw, (gi + 1) * gw)
        s = ext_ref[h - lo:h - lo + tm, cols]
        for j in range(1 - lo, hi):
            s = s + ext_ref[h + j:h + j + tm, cols]
        cnt = (jnp.minimum(t + hi, seq_len) - jnp.maximum(t - lo, 0)).astype(F32)
        dlt = s / cnt - cur_ref[:, cols]
        y = jnp.dot(dlt.astype(BF16), w_ref[gi], preferred_element_type=F32) * sc_ref[:, cols]
        o_ref[:, cols] = y.astype(o_ref.dtype)


def _pool_call(proj, pool_w, pool_scale, col_blk, bw, tm, ns_rows, dec_seq, seq):
    t = proj.shape[0]
    h = POOL_HALO
    r = tm // h
    last = t // h - 1
    gw = bw // len(POOL_WINDOWS)
    return pl.pallas_call(
        functools.partial(_pool_kernel, tm=tm, ns_rows=ns_rows, dec_seq=dec_seq, seq=seq),
        grid=(t // tm,),
        in_specs=[pl.BlockSpec((h, bw), lambda i: (jnp.maximum(i * r - 1, 0), col_blk)),
                  pl.BlockSpec((tm, bw), lambda i: (i, col_blk)),
                  pl.BlockSpec((h, bw), lambda i: (jnp.minimum((i + 1) * r, last), col_blk)),
                  pl.BlockSpec((len(POOL_WINDOWS), gw, gw), lambda i: (0, 0, 0)),
                  pl.BlockSpec((1, bw), lambda i: (0, 0))],
        out_specs=pl.BlockSpec((tm, bw), lambda i: (i, 0)),
        out_shape=jax.ShapeDtypeStruct((t, bw), BF16),
        scratch_shapes=[pltpu.VMEM((tm + 2 * h, bw), F32)],
        compiler_params=_cparams("parallel"),
        name="pool_mix",
    )(proj, proj, proj, pool_w, pool_scale)


def _conv_kernel(ap_ref, ac_ref, an_ref, gp_ref, gc_ref, gn_ref, w_ref, b_ref, lg_ref, lb_ref,
                 o_ref, ext_ref, y_ref, *, tm, ns_rows, dec_seq, seq):
    seq_len, pos0 = _seq_position(pl.program_id(0) * tm, ns_rows, dec_seq, seq)
    h = CONV_HALO
    ext_ref[0:h, :] = jnp.where(pos0 == 0, 0.0, ap_ref[...] * _sigmoid(gp_ref[...]))
    ext_ref[h:h + tm, :] = ac_ref[...] * _sigmoid(gc_ref[...])
    ext_ref[h + tm:2 * h + tm, :] = jnp.where(pos0 + tm == seq_len, 0.0,
                                              an_ref[...] * _sigmoid(gn_ref[...]))
    bw = ac_ref.shape[1]
    lanes = 128
    for c in range(bw // lanes):
        cols = slice(c * lanes, (c + 1) * lanes)
        acc = jnp.zeros((tm, lanes), F32)
        for k in range(CONV_K):
            off = h + k - CONV_K // 2
            acc = acc + ext_ref[off:off + tm, cols] * w_ref[k:k + 1, cols]
        y_ref[:, cols] = acc + b_ref[:, cols]
    y = y_ref[...]
    mu = jnp.mean(y, axis=-1, keepdims=True)
    var = jnp.mean(jnp.square(y - mu), axis=-1, keepdims=True)
    z = (y - mu) * lax.rsqrt(var + LN_EPS) * lg_ref[...] + lb_ref[...]
    o_ref[...] = _silu(z).astype(o_ref.dtype)


def _conv_call(proj, conv_w, conv_b, ln_g, ln_b, col_blk, bw, tm, ns_rows, dec_seq, seq):
    t = proj.shape[0]
    h = CONV_HALO
    r = tm // h
    last = t // h - 1
    prev = lambda i: jnp.maximum(i * r - 1, 0)
    nxt = lambda i: jnp.minimum((i + 1) * r, last)
    vec = pl.BlockSpec((1, bw), lambda i: (0, 0))
    return pl.pallas_call(
        functools.partial(_conv_kernel, tm=tm, ns_rows=ns_rows, dec_seq=dec_seq, seq=seq),
        grid=(t // tm,),
        in_specs=[pl.BlockSpec((h, bw), lambda i: (prev(i), col_blk)),
                  pl.BlockSpec((tm, bw), lambda i: (i, col_blk)),
                  pl.BlockSpec((h, bw), lambda i: (nxt(i), col_blk)),
                  pl.BlockSpec((h, bw), lambda i: (prev(i), col_blk + 1)),
                  pl.BlockSpec((tm, bw), lambda i: (i, col_blk + 1)),
                  pl.BlockSpec((h, bw), lambda i: (nxt(i), col_blk + 1)),
                  pl.BlockSpec((CONV_K, bw), lambda i: (0, 0)),
                  vec, vec, vec],
        out_specs=pl.BlockSpec((tm, bw), lambda i: (i, 0)),
        out_shape=jax.ShapeDtypeStruct((t, bw), BF16),
        scratch_shapes=[pltpu.VMEM((tm + 2 * h, bw), F32), pltpu.VMEM((tm, bw), F32)],
        compiler_params=_cparams("parallel"),
        name="conv_module",
    )(proj, proj, proj, proj, proj, proj, conv_w, conv_b, ln_g, ln_b)


def _ctx_attn_kernel(q_ref, k_ref, v_ref, o_ref, ko_ref, vo_ref, *, scale):
    k = k_ref[...]
    v = v_ref[...]
    s = _nt_dot(q_ref[...].astype(BF16), k.astype(BF16)) * scale
    p = jnp.exp(s - jnp.max(s, axis=-1, keepdims=True))
    l = jnp.sum(p, axis=-1, keepdims=True)
    o = jnp.dot(p.astype(BF16), v.astype(BF16), preferred_element_type=F32)
    o_ref[...] = (o / l).astype(o_ref.dtype)
    ko_ref[...] = k
    vo_ref[...] = v


def _ctx_attn_call(proj, row_base, nb, seq, q_col, hd):
    base = row_base // seq
    nh = NA_HEADS
    kv_spec = pl.BlockSpec((None, None, seq, hd), lambda b, h: (b, h, 0, 0))
    return pl.pallas_call(
        functools.partial(_ctx_attn_kernel, scale=hd ** -0.5),
        grid=(nb, nh),
        in_specs=[pl.BlockSpec((seq, hd), lambda b, h: (base + b, q_col + h)),
                  pl.BlockSpec((seq, hd), lambda b, h: (base + b, q_col + nh + h)),
                  pl.BlockSpec((seq, hd), lambda b, h: (base + b, q_col + 2 * nh + h))],
        out_specs=[pl.BlockSpec((seq, hd), lambda b, h: (b, h)), kv_spec, kv_spec],
        out_shape=[jax.ShapeDtypeStruct((nb * seq, nh * hd), BF16),
                   jax.ShapeDtypeStruct((nb, nh, seq, hd), F32),
                   jax.ShapeDtypeStruct((nb, nh, seq, hd), F32)],
        compiler_params=_cparams("parallel", "parallel"),
        name="context_attention",
    )(proj, proj, proj)


def _na_key_row0(kb, rows):
    lo = kb * NA_Q_ROWS - NA_WIN_ROWS // 2
    if isinstance(kb, (int, np.integer)):
        return int(np.clip(lo, 0, rows - NA_K_ROWS))
    return jnp.clip(lo, 0, rows - NA_K_ROWS)


def _na_bias_tables(rpb, rows):
    nh = rpb.shape[0]
    w = GRID_W
    c = np.arange(w)
    cs = np.clip(c - NA_WIN_COLS // 2, 0, w - NA_WIN_COLS)
    col_ok = (c[None, :] >= cs[:, None]) & (c[None, :] < cs[:, None] + NA_WIN_COLS)
    rel_col = np.clip(c[None, :] - c[:, None] + NA_WIN_COLS - 1, 0, 2 * NA_WIN_COLS - 2)
    planes = jnp.where(col_ok[None, None], rpb[:, :, rel_col], MASK_VALUE)
    planes = jnp.concatenate([planes, jnp.full((nh, 1, w, w), MASK_VALUE, rpb.dtype)], axis=1)
    masked_plane = 2 * NA_WIN_ROWS - 1
    variants, var_ids = [], []
    for kb in range(rows // NA_Q_ROWS):
        r = kb * NA_Q_ROWS + np.arange(NA_Q_ROWS)
        rs = np.clip(r - NA_WIN_ROWS // 2, 0, rows - NA_WIN_ROWS)
        kr = _na_key_row0(kb, rows) + np.arange(NA_K_ROWS)
        ok = (kr[None, :] >= rs[:, None]) & (kr[None, :] < rs[:, None] + NA_WIN_ROWS)
        assert ok.sum() == NA_Q_ROWS * NA_WIN_ROWS
        plane = np.where(ok, kr[None, :] - r[:, None] + NA_WIN_ROWS - 1, masked_plane)
        for vi, known in enumerate(variants):
            if np.array_equal(known, plane):
                var_ids.append(vi)
                break
        else:
            var_ids.append(len(variants))
            variants.append(plane)
    tbl = pl.pallas_call(
        functools.partial(_na_table_kernel, variants=[v.tolist() for v in variants]),
        grid=(nh,),
        in_specs=[pl.BlockSpec((None, masked_plane + 1, w, w), lambda h: (h, 0, 0, 0))],
        out_specs=pl.BlockSpec((None, len(variants), NA_Q_ROWS * w, NA_K_ROWS * w),
                               lambda h: (h, 0, 0, 0)),
        out_shape=jax.ShapeDtypeStruct((nh, len(variants), NA_Q_ROWS * w, NA_K_ROWS * w), rpb.dtype),
        compiler_params=_cparams("parallel"),
        name="na_bias_table",
    )(planes)
    return tbl, jnp.asarray(np.array(var_ids, np.int32))


def _na_table_kernel(planes_ref, o_ref, *, variants):
    w = GRID_W
    for v, plane_of in enumerate(variants):
        for qr in range(NA_Q_ROWS):
            for kr in range(0, NA_K_ROWS, 2):
                pair = jnp.concatenate([planes_ref[plane_of[qr][kr]], planes_ref[plane_of[qr][kr + 1]]], axis=1)
                o_ref[v, qr * w:(qr + 1) * w, kr * w:(kr + 2) * w] = pair


def _na_attn_kernel(var_ref, q_ref, k_ref, v_ref, kc_ref, vc_ref, bias_ref, o_ref, *, rows, scale):
    del var_ref
    nk = NA_K_ROWS * GRID_W
    start = pl.multiple_of(_na_key_row0(pl.program_id(2), rows) * GRID_W, 256)
    kw = k_ref[pl.ds(start, nk), :].astype(BF16)
    vw = v_ref[pl.ds(start, nk), :].astype(BF16)
    q = q_ref[...].astype(BF16)
    s_loc = _nt_dot(q, kw) * scale + bias_ref[...]
    s_ctx = _nt_dot(q, kc_ref[...].astype(BF16)) * scale
    m = jnp.maximum(jnp.max(s_loc, axis=-1, keepdims=True), jnp.max(s_ctx, axis=-1, keepdims=True))
    p_loc = jnp.exp(s_loc - m)
    p_ctx = jnp.exp(s_ctx - m)
    l = jnp.sum(p_loc, axis=-1, keepdims=True) + jnp.sum(p_ctx, axis=-1, keepdims=True)
    o = (jnp.dot(p_loc.astype(BF16), vw, preferred_element_type=F32)
         + jnp.dot(p_ctx.astype(BF16), vc_ref[...].astype(BF16), preferred_element_type=F32))
    o_ref[...] = (o / l).astype(o_ref.dtype)


def _na_attn_call(proj, cache_k, cache_v, layer, rpb, nb, dec_seq, q_col, hd):
    rows = dec_seq // GRID_W
    assert rows >= NA_K_ROWS and rows % NA_Q_ROWS == 0
    nh = NA_HEADS
    nq = NA_Q_ROWS * GRID_W
    nkb = rows // NA_Q_ROWS
    past = cache_k.shape[3]
    tbl, var_ids = _na_bias_tables(rpb, rows)
    ctx_spec = pl.BlockSpec((None, None, None, past, hd), lambda b, h, k, var: (b, layer, h, 0, 0))
    grid_spec = pltpu.PrefetchScalarGridSpec(
        num_scalar_prefetch=1,
        grid=(nb, nh, nkb),
        in_specs=[pl.BlockSpec((nq, hd), lambda b, h, k, var: (b * nkb + k, q_col + h)),
                  pl.BlockSpec((dec_seq, hd), lambda b, h, k, var: (b, q_col + nh + h)),
                  pl.BlockSpec((dec_seq, hd), lambda b, h, k, var: (b, q_col + 2 * nh + h)),
                  ctx_spec, ctx_spec,
                  pl.BlockSpec((None, None, nq, NA_K_ROWS * GRID_W),
                               lambda b, h, k, var: (h, var[k], 0, 0))],
        out_specs=pl.BlockSpec((nq, hd), lambda b, h, k, var: (b * nkb + k, h)),
    )
    return pl.pallas_call(
        functools.partial(_na_attn_kernel, rows=rows, scale=hd ** -0.5),
        grid_spec=grid_spec,
        out_shape=jax.ShapeDtypeStruct((nb * dec_seq, nh * hd), BF16),
        compiler_params=_cparams("parallel", "parallel", "arbitrary"),
        name="neighbourhood_attention",
    )(var_ids, proj, proj, proj, cache_k, cache_v, tbl)


def _merge_kernel(fl_ref, fc_ref, p_ref, al_ref, ac_ref, c_ref, g0_ref, g1_ref, g2_ref, g3_ref, w_ref,
                  o_ref, *, n_lat_tiles):
    is_lat = pl.program_id(0) < n_lat_tiles
    f = jnp.where(is_lat, fl_ref[...], fc_ref[...])
    a = jnp.where(is_lat, al_ref[...], ac_ref[...])
    acc = None
    for n, (br, gt) in enumerate(((f, g0_ref), (p_ref[...], g1_ref), (a, g2_ref), (c_ref[...], g3_ref))):
        y = gt[...].astype(F32) * jnp.dot(br, w_ref[n], preferred_element_type=F32)
        acc = y if acc is None else acc + y
    o_ref[...] = acc.astype(o_ref.dtype)


def _merge_call(f_lat, f_ctx, o_p, a_lat, a_ctx, o_c, gates, w_branch, tm, tn):
    t, bw = o_p.shape
    d = w_branch.shape[2]
    nj = d // tn
    nl = f_lat.shape[0] // tm
    br_spec = pl.BlockSpec((tm, bw), lambda i, j: (i, 0))
    lat_spec = pl.BlockSpec((tm, bw), lambda i, j: (jnp.minimum(i, nl - 1), 0))
    ctx_spec = pl.BlockSpec((tm, bw), lambda i, j: (jnp.maximum(i - nl, 0), 0))
    gate_specs = [pl.BlockSpec((tm, tn), functools.partial(lambda i, j, n: (i, n * nj + j), n=n))
                  for n in range(N_BRANCH)]
    return pl.pallas_call(
        functools.partial(_merge_kernel, n_lat_tiles=nl),
        grid=(t // tm, nj),
        in_specs=[lat_spec, ctx_spec, br_spec, lat_spec, ctx_spec, br_spec] + gate_specs
                 + [pl.BlockSpec((N_BRANCH, bw, tn), lambda i, j: (0, 0, j))],
        out_specs=pl.BlockSpec((tm, tn), lambda i, j: (i, j)),
        out_shape=jax.ShapeDtypeStruct((t, d), BF16),
        compiler_params=_cparams("parallel", "parallel"),
        name="branch_merge",
    )(f_lat, f_ctx, o_p, a_lat, a_ctx, o_c, gates, gates, gates, gates, w_branch)


def _out_kernel(a_ref, w_ref, x_hbm, mod_ref, g_ref, *rest, tm, gate_row, next_rows):
    if next_rows is None:
        o_ref, xbuf_ref, sem = rest
    else:
        nmod_ref, ng_ref, o_ref, h_ref, xbuf_ref, sem = rest
    k = pl.program_id(1)
    x_copy = pltpu.make_async_copy(x_hbm.at[pl.ds(pl.program_id(0) * tm, tm)], xbuf_ref, sem)

    @pl.when(k == 0)
    def _():
        x_copy.start()
        o_ref[...] = jnp.zeros_like(o_ref)

    o_ref[...] += jnp.dot(a_ref[...], w_ref[...], preferred_element_type=F32)

    @pl.when(k == pl.num_programs(1) - 1)
    def _():
        x_copy.wait()
        xn = xbuf_ref[...] + mod_ref[gate_row:gate_row + 1, :] * _rms(o_ref[...], g_ref[...])
        o_ref[...] = xn
        if next_rows is not None:
            h_ref[...] = _mod_norm(xn, ng_ref[...], nmod_ref, *next_rows).astype(h_ref.dtype)


def _k_tile(kdim, cap):
    lanes = 128
    return max(t for t in range(lanes, cap + 1, lanes) if kdim % t == 0)


def _out_call(a, w, x, mod, g, gate_row, nxt, mod_row, tm):
    t, d = x.shape
    kdim = a.shape[1]
    tk = _k_tile(kdim, 1024)
    row = pl.BlockSpec((tm, d), lambda i, k: (i, 0))
    mod_spec = pl.BlockSpec((None, 6, d), lambda i, k: (mod_row(i * tm), 0, 0))
    vec = pl.BlockSpec((1, d), lambda i, k: (0, 0))
    in_specs = [pl.BlockSpec((tm, tk), lambda i, k: (i, k)),
                pl.BlockSpec((tk, d), lambda i, k: (k, 0)),
                pl.BlockSpec(memory_space=pl.ANY), mod_spec, vec]
    args = [a, w, x, mod, g]
    out_specs, out_shape = row, jax.ShapeDtypeStruct((t, d), F32)
    if nxt is not None:
        in_specs += [mod_spec, vec]
        args += [nxt[0], nxt[1]]
        out_specs = [row, row]
        out_shape = [out_shape, jax.ShapeDtypeStruct((t, d), BF16)]
    return pl.pallas_call(
        functools.partial(_out_kernel, tm=tm, gate_row=gate_row,
                          next_rows=None if nxt is None else nxt[2]),
        grid=(t // tm, kdim // tk),
        in_specs=in_specs,
        out_specs=out_specs,
        out_shape=out_shape,
        scratch_shapes=[pltpu.VMEM((tm, d), F32), pltpu.SemaphoreType.DMA(())],
        compiler_params=_cparams("arbitrary", "arbitrary", vmem_mib=VMEM_LIMIT_WIDE_MIB),
        name="proj_norm_residual",
    )(*args)


def _ffn_up_kernel(h_ref, wg_ref, wu_ref, o_ref):
    h = h_ref[...]
    y = _silu(jnp.dot(h, wg_ref[...], preferred_element_type=F32)) * jnp.dot(
        h, wu_ref[...], preferred_element_type=F32)
    o_ref[...] = y.astype(o_ref.dtype)


def _ffn_up_call(h, wg, wu, tm, tn):
    t, d = h.shape
    f = wg.shape[1]
    w_spec = pl.BlockSpec((d, tn), lambda i, j: (0, j))
    return pl.pallas_call(
        _ffn_up_kernel,
        grid=(t // tm, f // tn),
        in_specs=[pl.BlockSpec((tm, d), lambda i, j: (i, 0)), w_spec, w_spec],
        out_specs=pl.BlockSpec((tm, tn), lambda i, j: (i, j)),
        out_shape=jax.ShapeDtypeStruct((t, f), BF16),
        compiler_params=_cparams("parallel", "parallel"),
        name="ffn_up",
    )(h, wg, wu)


def _split_bf16(x):
    hi = x.astype(BF16)
    return hi, (x - hi.astype(F32)).astype(BF16)


def _pack_bf16_pairs(h):
    n = h.shape[1] // 2
    hb = h.astype(BF16).astype(F32)
    hi = pltpu.bitcast(hb[:, :n], jnp.int32)
    lo = pltpu.bitcast(hb[:, n:], jnp.int32)
    return hi | lax.shift_right_logical(lo, 16)


def _unpack_bf16_pairs(w):
    hi = pltpu.bitcast(w & jnp.int32(-65536), F32).astype(BF16)
    lo = pltpu.bitcast(w << 16, F32).astype(BF16)
    return hi, lo


def _router_kernel(x_ref, mod_ref, g_ref, w_ref, b_ref, o_ref, hp_ref, *, n_experts):
    h = _mod_norm(x_ref[...], g_ref[...], mod_ref, SCALE2, SHIFT2)
    hp_ref[...] = _pack_bf16_pairs(h)
    h_hi, h_lo = _split_bf16(h)
    w_hi, w_lo = _split_bf16(w_ref[...])
    logits = (jnp.dot(h_hi, w_hi, preferred_element_type=F32)
              + jnp.dot(h_lo, w_hi, preferred_element_type=F32)
              + jnp.dot(h_hi, w_lo, preferred_element_type=F32)) + b_ref[...]
    lane = lax.broadcasted_iota(jnp.int32, logits.shape, 1).astype(F32)
    neg = -jnp.inf
    no_lane = float(logits.shape[1])
    logits = jnp.where(lane < n_experts, logits, neg)
    m1 = jnp.max(logits, axis=-1, keepdims=True)
    i1 = jnp.min(jnp.where(logits == m1, lane, no_lane), axis=-1, keepdims=True)
    rest = jnp.where(lane == i1, neg, logits)
    m2 = jnp.max(rest, axis=-1, keepdims=True)
    i2 = jnp.min(jnp.where(rest == m2, lane, no_lane), axis=-1, keepdims=True)
    e2 = jnp.exp(m2 - m1)
    den = 1.0 + e2
    o_ref[...] = (jnp.where(lane == 0.0, i1, 0.0) + jnp.where(lane == 1.0, i2, 0.0)
                  + jnp.where(lane == 2.0, 1.0 / den, 0.0) + jnp.where(lane == 3.0, e2 / den, 0.0))


def _router_call(x, mod, g, w_router, b_router, mod_row, tm):
    t, d = x.shape
    ne = w_router.shape[1]
    lanes = 128
    w = jnp.zeros((d, lanes), F32).at[:, :ne].set(w_router)
    b = jnp.zeros((1, lanes), F32).at[0, :ne].set(b_router)
    return pl.pallas_call(
        functools.partial(_router_kernel, n_experts=ne),
        grid=(t // tm,),
        in_specs=[pl.BlockSpec((tm, d), lambda i: (i, 0)),
                  pl.BlockSpec((None, 6, d), lambda i: (mod_row(i * tm), 0, 0)),
                  pl.BlockSpec((1, d), lambda i: (0, 0)),
                  pl.BlockSpec((d, lanes), lambda i: (0, 0)),
                  pl.BlockSpec((1, lanes), lambda i: (0, 0))],
        out_specs=[pl.BlockSpec((tm, lanes), lambda i: (i, 0)),
                   pl.BlockSpec((tm, d // 2), lambda i: (i, 0))],
        out_shape=[jax.ShapeDtypeStruct((t, lanes), F32),
                   jax.ShapeDtypeStruct((t, d // 2), jnp.int32)],
        compiler_params=_cparams("parallel"),
        name="router",
    )(x, mod, g, w, b)


def _routing_tables(route, n_experts, tm):
    t = route.shape[0]
    na = TOP_K * t
    p_rows = na + n_experts * tm
    nt = p_rows // tm
    e_flat = route[:, :TOP_K].astype(jnp.int32).reshape(na)
    onehot = (e_flat[:, None] == jnp.arange(n_experts, dtype=jnp.int32)[None, :]).astype(jnp.int32)
    csum = jnp.cumsum(onehot, axis=0)
    rank = jnp.sum((csum - onehot) * onehot, axis=1)
    counts = csum[-1]
    padded = ((counts + tm - 1) // tm) * tm
    ends = jnp.cumsum(padded)
    pos = (ends - padded)[e_flat] + rank
    src = jnp.zeros((p_rows,), jnp.int32).at[pos].set(
        jnp.arange(na, dtype=jnp.int32) // TOP_K, unique_indices=True, mode="promise_in_bounds")
    tile_expert = jnp.minimum(
        jnp.searchsorted(ends, jnp.arange(nt, dtype=jnp.int32) * tm, side="right"), n_experts - 1)
    meta = jnp.concatenate([tile_expert.astype(jnp.int32), (ends[-1:] // tm).astype(jnp.int32)])
    return src, pos.astype(jnp.int32), meta


def _gather_rows_kernel(idx_ref, src_ref, o_ref, buf_ref, sem_ref, *, rows):
    i = pl.program_id(0)
    n = pl.num_programs(0)

    def issue(tile, slot):
        def body(r, carry):
            pltpu.make_async_copy(src_ref.at[pl.ds(idx_ref[tile * rows + r], 1)],
                                  buf_ref.at[slot, pl.ds(r, 1)], sem_ref.at[slot]).start()
            return carry
        lax.fori_loop(0, rows, body, 0, unroll=DMA_ISSUE_UNROLL)

    @pl.when(i == 0)
    def _():
        issue(0, 0)

    @pl.when(i + 1 < n)
    def _():
        issue(i + 1, (i + 1) % 2)

    slot = i % 2
    pltpu.make_async_copy(src_ref.at[pl.ds(0, rows)], buf_ref.at[slot], sem_ref.at[slot]).wait()
    half = buf_ref.shape[2]
    hi, lo = _unpack_bf16_pairs(buf_ref[slot])
    o_ref[:, :half] = hi
    o_ref[:, half:] = lo


def _gather_rows_call(src, idx, rows):
    p_rows = idx.shape[0]
    half = src.shape[1]
    grid_spec = pltpu.PrefetchScalarGridSpec(
        num_scalar_prefetch=1,
        grid=(p_rows // rows,),
        in_specs=[pl.BlockSpec(memory_space=pl.ANY)],
        out_specs=pl.BlockSpec((rows, 2 * half), lambda i, idx: (i, 0)),
        scratch_shapes=[pltpu.VMEM((2, rows, half), src.dtype), pltpu.SemaphoreType.DMA((2,))],
    )
    return pl.pallas_call(
        functools.partial(_gather_rows_kernel, rows=rows),
        grid_spec=grid_spec,
        out_shape=jax.ShapeDtypeStruct((p_rows, 2 * half), BF16),
        compiler_params=_cparams("arbitrary"),
        name="moe_gather",
    )(idx, src)


def _moe_up_kernel(meta_ref, xs_ref, wg_ref, wu_ref, o_ref, wgb_ref, wub_ref, *, n_tiles):
    i = pl.program_id(1)
    used = i < meta_ref[n_tiles]
    fresh = jnp.logical_or(i == 0, meta_ref[i] != meta_ref[jnp.maximum(i - 1, 0)])

    @pl.when(jnp.logical_and(used, fresh))
    def _():
        wgb_ref[...] = wg_ref[...].astype(BF16)
        wub_ref[...] = wu_ref[...].astype(BF16)

    @pl.when(used)
    def _():
        h = xs_ref[...]
        y = _silu(jnp.dot(h, wgb_ref[...], preferred_element_type=F32)) * jnp.dot(
            h, wub_ref[...], preferred_element_type=F32)
        o_ref[...] = y.astype(o_ref.dtype)

    @pl.when(jnp.logical_not(used))
    def _():
        o_ref[...] = jnp.zeros_like(o_ref)


def _moe_up_call(xs, meta, wg, wu, tm, tn):
    p_rows, d = xs.shape
    f = wg.shape[2]
    nt = p_rows // tm
    w_spec = pl.BlockSpec((None, d, tn), lambda j, i, meta: (meta[i], 0, j))
    grid_spec = pltpu.PrefetchScalarGridSpec(
        num_scalar_prefetch=1,
        grid=(f // tn, nt),
        in_specs=[pl.BlockSpec((tm, d), lambda j, i, meta: (i, 0)), w_spec, w_spec],
        out_specs=pl.BlockSpec((None, tm, tn), lambda j, i, meta: (j, i, 0)),
        scratch_shapes=[pltpu.VMEM((d, tn), BF16), pltpu.VMEM((d, tn), BF16)],
    )
    return pl.pallas_call(
        functools.partial(_moe_up_kernel, n_tiles=nt),
        grid_spec=grid_spec,
        out_shape=jax.ShapeDtypeStruct((f // tn, p_rows, tn), BF16),
        compiler_params=_cparams("arbitrary", "arbitrary", vmem_mib=VMEM_LIMIT_WIDE_MIB),
        name="moe_up",
    )(meta, xs, wg, wu)


def _moe_down_kernel(meta_ref, a_ref, w_ref, o_ref, *, n_tiles):
    i = pl.program_id(0)

    @pl.when(pl.program_id(1) == 0)
    def _():
        o_ref[...] = jnp.zeros_like(o_ref)

    @pl.when(i < meta_ref[n_tiles])
    def _():
        nsub, _, tk = a_ref.shape
        acc = o_ref[...]
        for s in range(nsub):
            acc = acc + jnp.dot(a_ref[s], w_ref[s * tk:(s + 1) * tk, :], preferred_element_type=F32)
        o_ref[...] = acc


def _moe_down_call(a, meta, wd, tm, nsub):
    nk, p_rows, tk = a.shape
    d = wd.shape[2]
    nt = p_rows // tm
    grid_spec = pltpu.PrefetchScalarGridSpec(
        num_scalar_prefetch=1,
        grid=(nt, nk // nsub),
        in_specs=[pl.BlockSpec((nsub, tm, tk), lambda i, k, meta: (k, i, 0)),
                  pl.BlockSpec((None, nsub * tk, d), lambda i, k, meta: (meta[i], k, 0))],
        out_specs=pl.BlockSpec((tm, d), lambda i, k, meta: (i, 0)),
    )
    return pl.pallas_call(
        functools.partial(_moe_down_kernel, n_tiles=nt),
        grid_spec=grid_spec,
        out_shape=jax.ShapeDtypeStruct((p_rows, d), F32),
        compiler_params=_cparams("parallel", "arbitrary"),
        name="moe_down",
    )(meta, a, wd)


def _moe_combine_kernel(pos_ref, ys_ref, route_ref, x_ref, mod_ref, g_ref, *rest, rows, n_lat_tiles):
    if n_lat_tiles is None:
        o_ref, buf_ref, sem_ref = rest
    else:
        lat_ref, ctx_ref, buf_ref, sem_ref = rest
    i = pl.program_id(0)
    n = pl.num_programs(0)

    def issue(tile, slot):
        def body(r, carry):
            for s in range(TOP_K):
                pltpu.make_async_copy(ys_ref.at[pl.ds(pos_ref[TOP_K * (tile * rows + r) + s], 1)],
                                      buf_ref.at[slot, s, pl.ds(r, 1)], sem_ref.at[slot]).start()
            return carry
        lax.fori_loop(0, rows, body, 0, unroll=DMA_ISSUE_UNROLL)

    @pl.when(i == 0)
    def _():
        issue(0, 0)

    @pl.when(i + 1 < n)
    def _():
        issue(i + 1, (i + 1) % 2)

    slot = i % 2
    for s in range(TOP_K):
        pltpu.make_async_copy(ys_ref.at[pl.ds(0, rows)], buf_ref.at[slot, s], sem_ref.at[slot]).wait()
    route = route_ref[...]
    y = None
    for s in range(TOP_K):
        term = route[:, TOP_K + s:TOP_K + s + 1] * buf_ref[slot, s]
        y = term if y is None else y + term
    res = x_ref[...] + mod_ref[GATE2:GATE2 + 1, :] * _rms(y, g_ref[...])
    if n_lat_tiles is None:
        o_ref[...] = res
    else:
        @pl.when(i < n_lat_tiles)
        def _():
            lat_ref[...] = res

        @pl.when(i >= n_lat_tiles)
        def _():
            ctx_ref[...] = res


def _moe_combine_call(ys, pos, route, x, mod, g, mod_row, rows, split_rows):
    t, d = x.shape
    row = pl.BlockSpec((rows, d), lambda i, pos: (i, 0))
    if split_rows is None:
        nl = None
        out_specs, out_shape = row, jax.ShapeDtypeStruct((t, d), F32)
    else:
        nl = split_rows // rows
        out_specs = [pl.BlockSpec((rows, d), lambda i, pos: (jnp.minimum(i, nl - 1), 0)),
                     pl.BlockSpec((rows, d), lambda i, pos: (jnp.maximum(i - nl, 0), 0))]
        out_shape = [jax.ShapeDtypeStruct((split_rows, d), F32),
                     jax.ShapeDtypeStruct((t - split_rows, d), F32)]
    grid_spec = pltpu.PrefetchScalarGridSpec(
        num_scalar_prefetch=1,
        grid=(t // rows,),
        in_specs=[pl.BlockSpec(memory_space=pl.ANY),
                  pl.BlockSpec((rows, route.shape[1]), lambda i, pos: (i, 0)),
                  row,
                  pl.BlockSpec((None, 6, d), lambda i, pos: (mod_row(i * rows), 0, 0)),
                  pl.BlockSpec((1, d), lambda i, pos: (0, 0))],
        out_specs=out_specs,
        scratch_shapes=[pltpu.VMEM((2, TOP_K, rows, d), F32), pltpu.SemaphoreType.DMA((2,))],
    )
    return pl.pallas_call(
        functools.partial(_moe_combine_kernel, rows=rows, n_lat_tiles=nl),
        grid_spec=grid_spec,
        out_shape=out_shape,
        compiler_params=_cparams("arbitrary"),
        name="moe_combine",
    )(pos, ys, route, x, mod, g)


def kernel(x_prompt, x_sample, cache_k, cache_v, c, c_ctx, w_ada, b_ada, norm_g, w_in, b_in, pool_w, pool_scale, rpb, conv_w, conv_b, conv_ln_g, conv_ln_b, w_branch, w_out, w_gate_d, w_up_d, w_down_d, w_router, b_router, w_gate_e, w_up_e, w_down_e):
    nbp, seq, d = x_prompt.shape
    nbs, dec_seq, _ = x_sample.shape
    depth = w_ada.shape[0]
    bw = d // N_BRANCH
    hd = bw // NA_HEADS
    ns_rows = nbs * dec_seq
    np_rows = nbp * seq
    assert dec_seq & (dec_seq - 1) == 0 and seq & (seq - 1) == 0
    assert nbs + 1 <= MOD_ROWS

    t_rows = ns_rows + np_rows
    tm_wide = max(tm for tm in (2048, 1024, 512) if dec_seq % tm == 0 and np_rows % tm == 0)
    tm_mid = min(tm_wide, 1024)
    tm_moe = 512
    tile_seq = min(256, seq)
    tn = 512
    assert t_rows % tm_wide == 0 and seq % tile_seq == 0

    def mod_row(row0):
        return jnp.where(row0 < ns_rows, row0 // dec_seq, nbs)

    cvec =jnp.zeros((MOD_ROWS, d), F32).at[:nbs].set(c).at[nbs].set(c_ctx)
    mods = _ada_call(cvec, w_ada, b_ada).reshape(depth, MOD_ROWS, 6, d)
    cc, sc = _channel_dft_mats(bw, FNET_GROUPS)

    pool_col, conv_col, main_cols = 1, 5, 7 * bw
    q_col = 2 * bw // hd

    def gain(l, n):
        return norm_g[l, n][None, :]

    x, h = _entry_call(x_sample.reshape(ns_rows, d), x_prompt.reshape(np_rows, d), mods[0], gain(0, 0),
                       mod_row, 512)
    w_in_bf16 = w_in.astype(BF16)
    new_k, new_v = [], []
    y_split = None
    for l in range(depth):
        mod = mods[l]
        last = l + 1 == depth
        i = l // 2
        if l % 2 == 1:
            proj, gates, wd_e = _in_call(h, w_in_bf16, b_in[l][None, :], l, main_cols, tm_wide, tn,
                                         w_down_e[i].reshape(-1, d))
            wd_e = wd_e.reshape(w_down_e.shape[1:])
        else:
            proj, gates = _in_call(h, w_in_bf16, b_in[l][None, :], l, main_cols, tm_wide, tn)

        f_lat = _fourier_call(proj, 0, nbs, dec_seq, bw, cc, sc)
        f_ctx = _fourier_call(proj, ns_rows, nbp, seq, bw, cc, sc)
        o_p = _pool_call(proj, pool_w[l].astype(BF16), pool_scale[l][None, :], pool_col, bw,
                         tile_seq, ns_rows, dec_seq, seq)
        o_c = _conv_call(proj, conv_w[l], conv_b[l][None, :], conv_ln_g[l][None, :],
                         conv_ln_b[l][None, :], conv_col, bw, tile_seq, ns_rows, dec_seq, seq)
        a_lat = _na_attn_call(proj, cache_k, cache_v, l, rpb[l], nbs, dec_seq, q_col, hd)
        a_ctx, k_ctx, v_ctx = _ctx_attn_call(proj, ns_rows, nbp, seq, q_col, hd)
        new_k.append(k_ctx)
        new_v.append(v_ctx)

        merged = _merge_call(f_lat, f_ctx, o_p, a_lat, a_ctx, o_c, gates, w_branch[l].astype(BF16),
                             tm_mid, tn)
        nxt = None if last else (mods[l + 1], gain(l + 1, 0), (SCALE1, SHIFT1))
        if l % 2 == 1:
            x = _out_call(merged, w_out[l].astype(BF16), x, mod, gain(l, 1), GATE1, None,
                          mod_row, tm_mid)
            ne = w_router.shape[2]
            route, h_packed = _router_call(x, mod, gain(l, 2), w_router[i], b_router[i], mod_row, 512)
            src, pos, meta = _routing_tables(route, ne, tm_moe)
            xs = _gather_rows_call(h_packed, src, 256)
            hmid = _moe_up_call(xs, meta, w_gate_e[i], w_up_e[i], tm_moe, 2 * tn)
            ys = _moe_down_call(hmid, meta, wd_e, tm_moe, 1)
            out = _moe_combine_call(ys, pos, route, x, mod, gain(l, 3), mod_row, 256,
                                    ns_rows if last else None)
            if last:
                y_split = out
            else:
                x = out
                h = _norm_call(x, nxt[0], nxt[1], *nxt[2], mod_row, tm_mid)
        else:
            x, h2 = _out_call(merged, w_out[l].astype(BF16), x, mod, gain(l, 1), GATE1,
                              (mod, gain(l, 2), (SCALE2, SHIFT2)), mod_row, tm_mid)
            hmid = _ffn_up_call(h2, w_gate_d[i].astype(BF16), w_up_d[i].astype(BF16), tm_wide, tn)
            if last:
                x = _out_call(hmid, w_down_d[i].astype(BF16), x, mod, gain(l, 3), GATE2, None,
                              mod_row, tm_mid)
            else:
                x, h = _out_call(hmid, w_down_d[i].astype(BF16), x, mod, gain(l, 3), GATE2, nxt,
                                 mod_row, tm_mid)

    if y_split is None:
        y_split = x[:ns_rows], x[ns_rows:]
    y_sample = y_split[0].reshape(nbs, dec_seq, d)
    y_prompt = y_split[1].reshape(nbp, seq, d)
    return y_prompt, y_sample, jnp.stack(new_k, axis=1), jnp.stack(new_v, axis=1)
```

```python
import functools

import numpy as np
import jax
import jax.numpy as jnp
from jax import lax
from jax.experimental import pallas as pl
from jax.experimental.pallas import tpu as pltpu

F32 = jnp.float32
BF16 = jnp.bfloat16

N_BRANCH = 4
FNET_GROUPS = 4
POOL_WINDOWS = (2, 4, 8, 16)
NA_HEADS = 4
NA_WIN_ROWS = 8
NA_WIN_COLS = 16
GRID_W = 64
CONV_K = 31
TOP_K = 2
RMS_EPS = 1e-6
LN_EPS = 1e-5
MASK_VALUE = -1e30
MOD_ROWS = 8
POOL_HALO = 8
CONV_HALO = 16
NA_Q_ROWS = 8
NA_K_ROWS = 16
DMA_ISSUE_UNROLL = 8
VMEM_LIMIT_MIB = 48
VMEM_LIMIT_WIDE_MIB = 56

SHIFT1, SCALE1, GATE1, SHIFT2, SCALE2, GATE2 = range(6)


def _cparams(*sem, vmem_mib=VMEM_LIMIT_MIB):
    return pltpu.CompilerParams(dimension_semantics=sem, vmem_limit_bytes=vmem_mib * 1024 * 1024)


def _sigmoid(x):
    return 0.5 * jnp.tanh(0.5 * x) + 0.5


def _silu(x):
    return x * _sigmoid(x)


def _rms(x, g):
    return x * lax.rsqrt(jnp.mean(x * x, axis=-1, keepdims=True) + RMS_EPS) * g


def _mod_norm(x, g, mod_ref, scale_row, shift_row):
    return (_rms(x, g) * (1.0 + mod_ref[scale_row:scale_row + 1, :])
            + mod_ref[shift_row:shift_row + 1, :])


def _nt_dot(a, b):
    return lax.dot_general(a, b, (((1,), (1,)), ((), ())), preferred_element_type=F32)


def _ada_kernel(c_ref, w_ref, b_ref, o_ref):
    s = _silu(c_ref[...]).astype(BF16)
    o_ref[...] = jnp.dot(s, w_ref[...].astype(BF16), preferred_element_type=F32) + b_ref[...]


def _ada_call(cvec, w_ada, b_ada):
    depth, d, n = w_ada.shape
    tn = 512
    return pl.pallas_call(
        _ada_kernel,
        grid=(depth, n // tn),
        in_specs=[pl.BlockSpec((MOD_ROWS, d), lambda l, j: (0, 0)),
                  pl.BlockSpec((None, d, tn), lambda l, j: (l, 0, j)),
                  pl.BlockSpec((None, 1, tn), lambda l, j: (l, 0, j))],
        out_specs=pl.BlockSpec((None, MOD_ROWS, tn), lambda l, j: (l, 0, j)),
        out_shape=jax.ShapeDtypeStruct((depth, MOD_ROWS, n), F32),
        compiler_params=_cparams("parallel", "parallel"),
        name="adaln",
    )(cvec, w_ada, b_ada.reshape(depth, 1, n))


def _entry_kernel(lat_ref, ctx_ref, mod_ref, g_ref, x_ref, h_ref, *, n_lat_tiles):
    x = jnp.where(pl.program_id(0) < n_lat_tiles, lat_ref[...], ctx_ref[...])
    x_ref[...] = x
    h_ref[...] = _mod_norm(x, g_ref[...], mod_ref, SCALE1, SHIFT1).astype(h_ref.dtype)


def _entry_call(x_lat, x_ctx, mod, g, mod_row, tm):
    d = x_lat.shape[1]
    nl = x_lat.shape[0] // tm
    t = x_lat.shape[0] + x_ctx.shape[0]
    row = pl.BlockSpec((tm, d), lambda i: (i, 0))
    return pl.pallas_call(
        functools.partial(_entry_kernel, n_lat_tiles=nl),
        grid=(t // tm,),
        in_specs=[pl.BlockSpec((tm, d), lambda i: (jnp.minimum(i, nl - 1), 0)),
                  pl.BlockSpec((tm, d), lambda i: (jnp.maximum(i - nl, 0), 0)),
                  pl.BlockSpec((None, 6, d), lambda i: (mod_row(i * tm), 0, 0)),
                  pl.BlockSpec((1, d), lambda i: (0, 0))],
        out_specs=[row, row],
        out_shape=[jax.ShapeDtypeStruct((t, d), F32), jax.ShapeDtypeStruct((t, d), BF16)],
        compiler_params=_cparams("parallel"),
        name="entry_norm",
    )(x_lat, x_ctx, mod, g)


def _norm_kernel(x_ref, mod_ref, g_ref, o_ref, *, scale_row, shift_row):
    o_ref[...] = _mod_norm(x_ref[...], g_ref[...], mod_ref, scale_row, shift_row).astype(o_ref.dtype)


def _norm_call(x, mod, g, scale_row, shift_row, mod_row, tm):
    t, d = x.shape
    return pl.pallas_call(
        functools.partial(_norm_kernel, scale_row=scale_row, shift_row=shift_row),
        grid=(t // tm,),
        in_specs=[pl.BlockSpec((tm, d), lambda i: (i, 0)),
                  pl.BlockSpec((None, 6, d), lambda i: (mod_row(i * tm), 0, 0)),
                  pl.BlockSpec((1, d), lambda i: (0, 0))],
        out_specs=pl.BlockSpec((tm, d), lambda i: (i, 0)),
        out_shape=jax.ShapeDtypeStruct((t, d), BF16),
        compiler_params=_cparams("parallel"),
        name="mod_norm",
    )(x, mod, g)


def _cast_stream_specs(srcs, steps, nj):
    bf16_rows = 16
    specs, shapes = [], []
    for a in srcs:
        rows, cols = a.shape
        chunk = min(c for c in range(bf16_rows, rows + 1, bf16_rows)
                    if rows % c == 0 and rows // c <= steps)
        specs.append(pl.BlockSpec(
            (chunk, cols), functools.partial(lambda i, j, last: (jnp.minimum(i * nj + j, last), 0),
                                             last=rows // chunk - 1)))
        shapes.append(jax.ShapeDtypeStruct((rows, cols), BF16))
    return specs, shapes


def _cast_stream_step(src_refs, dst_refs):
    for s, d in zip(src_refs, dst_refs):
        d[...] = s[...].astype(d.dtype)


def _in_kernel(h_ref, w_ref, b_ref, *rest, n_main, n_cast):
    main_ref, gate_ref = rest[n_cast:n_cast + 2]
    _cast_stream_step(rest[:n_cast], rest[n_cast + 2:])
    j = pl.program_id(1)
    y = jnp.dot(h_ref[...], w_ref[...], preferred_element_type=F32) + b_ref[...]

    @pl.when(j < n_main)
    def _():
        main_ref[...] = y

    @pl.when(j >= n_main)
    def _():
        gate_ref[...] = _sigmoid(y).astype(gate_ref.dtype)


def _in_call(h, w, b, layer, main_cols, tm, tn, cast_srcs):
    t, d = h.shape
    n = w.shape[2]
    n_main = main_cols // tn
    nj = n // tn
    steps = (t // tm) * nj
    in_specs = [pl.BlockSpec((tm, d), lambda i, j: (i, 0)),
                pl.BlockSpec((None, d, tn), lambda i, j: (layer, 0, j)),
                pl.BlockSpec((1, tn), lambda i, j: (0, j))]
    out_specs = [pl.BlockSpec((tm, tn), lambda i, j: (i, jnp.minimum(j, n_main - 1))),
                 pl.BlockSpec((tm, tn), lambda i, j: (i, jnp.maximum(j - n_main, 0)))]
    out_shape = [jax.ShapeDtypeStruct((t, main_cols), F32),
                 jax.ShapeDtypeStruct((t, n - main_cols), BF16)]
    cast_specs, cast_shapes = _cast_stream_specs(cast_srcs, steps, nj)
    return pl.pallas_call(
        functools.partial(_in_kernel, n_main=n_main, n_cast=len(cast_srcs)),
        grid=(t // tm, nj),
        in_specs=in_specs + cast_specs,
        out_specs=out_specs + cast_specs,
        out_shape=out_shape + cast_shapes,
        compiler_params=_cparams("arbitrary", "arbitrary"),
        name="in_proj",
    )(h, w, b, *cast_srcs)


def _dft_mats(n):
    scale = 1.0 / np.sqrt(n)
    j = jnp.arange(n, dtype=jnp.int32)
    if n <= 1024:
        ang = ((j[:, None] * j[None, :]) % n).astype(F32) * (2.0 * np.pi / n)
        return (jnp.cos(ang) * scale).astype(BF16), (jnp.sin(ang) * scale).astype(BF16)
    base = 64
    hi = n // base
    k1 = jnp.arange(hi, dtype=jnp.int32)
    k0 = jnp.arange(base, dtype=jnp.int32)
    a = ((j[:, None] * k1[None, :]) % hi).astype(F32) * (2.0 * np.pi / hi)
    b = ((j[:, None] * k0[None, :]) % n).astype(F32) * (2.0 * np.pi / n)
    ca, sa = jnp.cos(a)[:, :, None], jnp.sin(a)[:, :, None]
    cb, sb = jnp.cos(b)[:, None, :] * scale, jnp.sin(b)[:, None, :] * scale
    c = (ca * cb - sa * sb).reshape(n, n)
    s = (sa * cb + ca * sb).reshape(n, n)
    return c.astype(BF16), s.astype(BF16)


def _channel_dft_mats(width, groups):
    gw = width // groups
    k = np.arange(gw)
    ang = 2.0 * np.pi * ((k[:, None] * k[None, :]) % gw) / gw
    c = np.zeros((width, width), np.float32)
    s = np.zeros((width, width), np.float32)
    for g in range(groups):
        sl = slice(g * gw, (g + 1) * gw)
        c[sl, sl] = np.cos(ang) / np.sqrt(gw)
        s[sl, sl] = np.sin(ang) / np.sqrt(gw)
    return jnp.asarray(c, BF16), jnp.asarray(s, BF16)


def _fnet1_kernel(u_ref, cc_ref, sc_ref, vc_ref, vs_ref):
    u = u_ref[...].astype(BF16)
    vc_ref[...] = jnp.dot(u, cc_ref[...], preferred_element_type=F32).astype(BF16)
    vs_ref[...] = jnp.dot(u, sc_ref[...], preferred_element_type=F32).astype(BF16)


def _fnet2_kernel(c_ref, s_ref, vc_ref, vs_ref, o_ref, acc_ref):
    k = pl.program_id(2)

    @pl.when(k == 0)
    def _():
        acc_ref[...] = jnp.zeros_like(acc_ref)

    acc_ref[...] += (jnp.dot(c_ref[...], vc_ref[...], preferred_element_type=F32)
                     - jnp.dot(s_ref[...], vs_ref[...], preferred_element_type=F32))

    @pl.when(k == pl.num_programs(2) - 1)
    def _():
        o_ref[...] = acc_ref[...].astype(o_ref.dtype)


def _fourier_call(proj, row_base, nb, seq, bw, cc, sc):
    c_l, s_l = _dft_mats(seq)
    tm1 = min(seq, 512)
    nt1 = seq // tm1
    base1 = row_base // tm1
    vc, vs = pl.pallas_call(
        _fnet1_kernel,
        grid=(nb, nt1),
        in_specs=[pl.BlockSpec((tm1, bw), lambda b, t: (base1 + b * nt1 + t, 0)),
                  pl.BlockSpec((bw, bw), lambda b, t: (0, 0)),
                  pl.BlockSpec((bw, bw), lambda b, t: (0, 0))],
        out_specs=[pl.BlockSpec((tm1, bw), lambda b, t: (t, b)),
                   pl.BlockSpec((tm1, bw), lambda b, t: (t, b))],
        out_shape=[jax.ShapeDtypeStruct((seq, nb * bw), BF16)] * 2,
        compiler_params=_cparams("parallel", "parallel"),
        name="fnet_channels",
    )(proj, cc, sc)
    tm2 = min(seq, 1024)
    tk = min(seq, 2048)
    nt2 = seq // tm2
    return pl.pallas_call(
        _fnet2_kernel,
        grid=(nt2, nb, seq // tk),
        in_specs=[pl.BlockSpec((tm2, tk), lambda i, j, k: (i, k)),
                  pl.BlockSpec((tm2, tk), lambda i, j, k: (i, k)),
                  pl.BlockSpec((tk, bw), lambda i, j, k: (k, j)),
                  pl.BlockSpec((tk, bw), lambda i, j, k: (k, j))],
        out_specs=pl.BlockSpec((tm2, bw), lambda i, j, k: (j * nt2 + i, 0)),
        out_shape=jax.ShapeDtypeStruct((nb * seq, bw), BF16),
        scratch_shapes=[pltpu.VMEM((tm2, bw), F32)],
        compiler_params=_cparams("parallel", "parallel", "arbitrary"),
        name="fnet_positions",
    )(c_l, s_l, vc, vs)


def _seq_position(row0, ns_rows, dec_seq, seq):
    is_lat = row0 < ns_rows
    seq_len = jnp.where(is_lat, dec_seq, seq)
    pos0 = jnp.where(is_lat, row0 & (dec_seq - 1), row0 & (seq - 1))
    return seq_len, pos0


def _pool_kernel(prev_ref, cur_ref, nxt_ref, w_ref, sc_ref, o_ref, ext_ref, *,
                 tm, ns_rows, dec_seq, seq):
    seq_len, pos0 = _seq_position(pl.program_id(0) * tm, ns_rows, dec_seq, seq)
    h = POOL_HALO
    ext_ref[0:h, :] = jnp.where(pos0 == 0, 0.0, prev_ref[...])
    ext_ref[h:h + tm, :] = cur_ref[...]
    ext_ref[h + tm:2 * h + tm, :] = jnp.where(pos0 + tm == seq_len, 0.0, nxt_ref[...])
    t = pos0 + lax.broadcasted_iota(jnp.int32, (tm, 1), 0)
    gw = cur_ref.shape[1] // len(POOL_WINDOWS)
    for gi, win in enumerate(POOL_WINDOWS):
        lo = win // 2
        hi = win - lo
        cols = slice(gi * gw, (gi + 1) * gw)
        s = ext_ref[h - lo:h - lo + tm, cols]
        for j in range(1 - lo, hi):
            s = s + ext_ref[h + j:h + j + tm, cols]
        cnt = (jnp.minimum(t + hi, seq_len) - jnp.maximum(t - lo, 0)).astype(F32)
        dlt = s / cnt - cur_ref[:, cols]
        y = jnp.dot(dlt.astype(BF16), w_ref[gi], preferred_element_type=F32) * sc_ref[:, cols]
        o_ref[:, cols] = y.astype(o_ref.dtype)


def _pool_call(proj, pool_w, pool_scale, col_blk, bw, tm, ns_rows, dec_seq, seq):
    t = proj.shape[0]
    h = POOL_HALO
    r = tm // h
    last = t // h - 1
    gw = bw // len(POOL_WINDOWS)
    return pl.pallas_call(
        functools.partial(_pool_kernel, tm=tm, ns_rows=ns_rows, dec_seq=dec_seq, seq=seq),
        grid=(t // tm,),
        in_specs=[pl.BlockSpec((h, bw), lambda i: (jnp.maximum(i * r - 1, 0), col_blk)),
                  pl.BlockSpec((tm, bw), lambda i: (i, col_blk)),
                  pl.BlockSpec((h, bw), lambda i: (jnp.minimum((i + 1) * r, last), col_blk)),
                  pl.BlockSpec((len(POOL_WINDOWS), gw, gw), lambda i: (0, 0, 0)),
                  pl.BlockSpec((1, bw), lambda i: (0, 0))],
        out_specs=pl.BlockSpec((tm, bw), lambda i: (i, 0)),
        out_shape=jax.ShapeDtypeStruct((t, bw), BF16),
        scratch_shapes=[pltpu.VMEM((tm + 2 * h, bw), F32)],
        compiler_params=_cparams("parallel"),
        name="pool_mix",
    )(proj, proj, proj, pool_w, pool_scale)


def _conv_kernel(ap_ref, ac_ref, an_ref, gp_ref, gc_ref, gn_ref, w_ref, b_ref, lg_ref, lb_ref,
                 o_ref, ext_ref, y_ref, *, tm, ns_rows, dec_seq, seq):
    seq_len, pos0 = _seq_position(pl.program_id(0) * tm, ns_rows, dec_seq, seq)
    h = CONV_HALO
    ext_ref[0:h, :] = jnp.where(pos0 == 0, 0.0, ap_ref[...] * _sigmoid(gp_ref[...]))
    ext_ref[h:h + tm, :] = ac_ref[...] * _sigmoid(gc_ref[...])
    ext_ref[h + tm:2 * h + tm, :] = jnp.where(pos0 + tm == seq_len, 0.0,
                                              an_ref[...] * _sigmoid(gn_ref[...]))
    bw = ac_ref.shape[1]
    lanes = 128
    for c in range(bw // lanes):
        cols = slice(c * lanes, (c + 1) * lanes)
        acc = jnp.zeros((tm, lanes), F32)
        for k in range(CONV_K):
            off = h + k - CONV_K // 2
            acc = acc + ext_ref[off:off + tm, cols] * w_ref[k:k + 1, cols]
        y_ref[:, cols] = acc + b_ref[:, cols]
    y = y_ref[...]
    mu = jnp.mean(y, axis=-1, keepdims=True)
    var = jnp.mean(jnp.square(y - mu), axis=-1, keepdims=True)
    z = (y - mu) * lax.rsqrt(var + LN_EPS) * lg_ref[...] + lb_ref[...]
    o_ref[...] = _silu(z).astype(o_ref.dtype)


def _conv_call(proj, conv_w, conv_b, ln_g, ln_b, col_blk, bw, tm, ns_rows, dec_seq, seq):
    t = proj.shape[0]
    h = CONV_HALO
    r = tm // h
    last = t // h - 1
    prev = lambda i: jnp.maximum(i * r - 1, 0)
    nxt = lambda i: jnp.minimum((i + 1) * r, last)
    vec = pl.BlockSpec((1, bw), lambda i: (0, 0))
    return pl.pallas_call(
        functools.partial(_conv_kernel, tm=tm, ns_rows=ns_rows, dec_seq=dec_seq, seq=seq),
        grid=(t // tm,),
        in_specs=[pl.BlockSpec((h, bw), lambda i: (prev(i), col_blk)),
                  pl.BlockSpec((tm, bw), lambda i: (i, col_blk)),
                  pl.BlockSpec((h, bw), lambda i: (nxt(i), col_blk)),
                  pl.BlockSpec((h, bw), lambda i: (prev(i), col_blk + 1)),
                  pl.BlockSpec((tm, bw), lambda i: (i, col_blk + 1)),
                  pl.BlockSpec((h, bw), lambda i: (nxt(i), col_blk + 1)),
                  pl.BlockSpec((CONV_K, bw), lambda i: (0, 0)),
                  vec, vec, vec],
        out_specs=pl.BlockSpec((tm, bw), lambda i: (i, 0)),
        out_shape=jax.ShapeDtypeStruct((t, bw), BF16),
        scratch_shapes=[pltpu.VMEM((tm + 2 * h, bw), F32), pltpu.VMEM((tm, bw), F32)],
        compiler_params=_cparams("parallel"),
        name="conv_module",
    )(proj, proj, proj, proj, proj, proj, conv_w, conv_b, ln_g, ln_b)


def _ctx_attn_kernel(q_ref, k_ref, v_ref, o_ref, ko_ref, vo_ref, *, scale):
    k = k_ref[...]
    v = v_ref[...]
    s = _nt_dot(q_ref[...].astype(BF16), k.astype(BF16)) * scale
    p = jnp.exp(s - jnp.max(s, axis=-1, keepdims=True))
    l = jnp.sum(p, axis=-1, keepdims=True)
    o = jnp.dot(p.astype(BF16), v.astype(BF16), preferred_element_type=F32)
    o_ref[...] = (o / l).astype(o_ref.dtype)
    ko_ref[...] = k
    vo_ref[...] = v


def _ctx_attn_call(proj, row_base, nb, seq, q_col, hd):
    base = row_base // seq
    nh = NA_HEADS
    kv_spec = pl.BlockSpec((None, None, seq, hd), lambda b, h: (b, h, 0, 0))
    return pl.pallas_call(
        functools.partial(_ctx_attn_kernel, scale=hd ** -0.5),
        grid=(nb, nh),
        in_specs=[pl.BlockSpec((seq, hd), lambda b, h: (base + b, q_col + h)),
                  pl.BlockSpec((seq, hd), lambda b, h: (base + b, q_col + nh + h)),
                  pl.BlockSpec((seq, hd), lambda b, h: (base + b, q_col + 2 * nh + h))],
        out_specs=[pl.BlockSpec((seq, hd), lambda b, h: (b, h)), kv_spec, kv_spec],
        out_shape=[jax.ShapeDtypeStruct((nb * seq, nh * hd), BF16),
                   jax.ShapeDtypeStruct((nb, nh, seq, hd), F32),
                   jax.ShapeDtypeStruct((nb, nh, seq, hd), F32)],
        compiler_params=_cparams("parallel", "parallel"),
        name="context_attention",
    )(proj, proj, proj)


def _na_key_row0(kb, rows):
    lo = kb * NA_Q_ROWS - NA_WIN_ROWS // 2
    if isinstance(kb, (int, np.integer)):
        return int(np.clip(lo, 0, rows - NA_K_ROWS))
    return jnp.clip(lo, 0, rows - NA_K_ROWS)


def _na_bias_tables(rpb, rows):
    nh = rpb.shape[0]
    w = GRID_W
    c = np.arange(w)
    cs = np.clip(c - NA_WIN_COLS // 2, 0, w - NA_WIN_COLS)
    col_ok = (c[None, :] >= cs[:, None]) & (c[None, :] < cs[:, None] + NA_WIN_COLS)
    rel_col = np.clip(c[None, :] - c[:, None] + NA_WIN_COLS - 1, 0, 2 * NA_WIN_COLS - 2)
    planes = jnp.where(col_ok[None, None], rpb[:, :, rel_col], MASK_VALUE)
    planes = jnp.concatenate([planes, jnp.full((nh, 1, w, w), MASK_VALUE, rpb.dtype)], axis=1)
    masked_plane = 2 * NA_WIN_ROWS - 1
    variants, var_ids = [], []
    for kb in range(rows // NA_Q_ROWS):
        r = kb * NA_Q_ROWS + np.arange(NA_Q_ROWS)
        rs = np.clip(r - NA_WIN_ROWS // 2, 0, rows - NA_WIN_ROWS)
        kr = _na_key_row0(kb, rows) + np.arange(NA_K_ROWS)
        ok = (kr[None, :] >= rs[:, None]) & (kr[None, :] < rs[:, None] + NA_WIN_ROWS)
        assert ok.sum() == NA_Q_ROWS * NA_WIN_ROWS
        plane = np.where(ok, kr[None, :] - r[:, None] + NA_WIN_ROWS - 1, masked_plane)
        for vi, known in enumerate(variants):
            if np.array_equal(known, plane):
                var_ids.append(vi)
                break
        else:
            var_ids.append(len(variants))
            variants.append(plane)
    tbl = pl.pallas_call(
        functools.partial(_na_table_kernel, variants=[v.tolist() for v in variants]),
        grid=(nh,),
        in_specs=[pl.BlockSpec((None, masked_plane + 1, w, w), lambda h: (h, 0, 0, 0))],
        out_specs=pl.BlockSpec((None, len(variants), NA_Q_ROWS * w, NA_K_ROWS * w),
                               lambda h: (h, 0, 0, 0)),
        out_shape=jax.ShapeDtypeStruct((nh, len(variants), NA_Q_ROWS * w, NA_K_ROWS * w), rpb.dtype),
        compiler_params=_cparams("parallel"),
        name="na_bias_table",
    )(planes)
    return tbl, jnp.asarray(np.array(var_ids, np.int32))


def _na_table_kernel(planes_ref, o_ref, *, variants):
    w = GRID_W
    for v, plane_of in enumerate(variants):
        for qr in range(NA_Q_ROWS):
            for kr in range(0, NA_K_ROWS, 2):
                pair = jnp.concatenate([planes_ref[plane_of[qr][kr]], planes_ref[plane_of[qr][kr + 1]]], axis=1)
                o_ref[v, qr * w:(qr + 1) * w, kr * w:(kr + 2) * w] = pair


def _na_attn_kernel(var_ref, q_ref, k_ref, v_ref, kc_ref, vc_ref, bias_ref, o_ref, *, rows, scale):
    del var_ref
    nk = NA_K_ROWS * GRID_W
    start = pl.multiple_of(_na_key_row0(pl.program_id(2), rows) * GRID_W, 256)
    kw = k_ref[pl.ds(start, nk), :].astype(BF16)
    vw = v_ref[pl.ds(start, nk), :].astype(BF16)
    q = q_ref[...].astype(BF16)
    s_loc = _nt_dot(q, kw) * scale + bias_ref[...]
    s_ctx = _nt_dot(q, kc_ref[...].astype(BF16)) * scale
    m = jnp.maximum(jnp.max(s_loc, axis=-1, keepdims=True), jnp.max(s_ctx, axis=-1, keepdims=True))
    p_loc = jnp.exp(s_loc - m)
    p_ctx = jnp.exp(s_ctx - m)
    l = jnp.sum(p_loc, axis=-1, keepdims=True) + jnp.sum(p_ctx, axis=-1, keepdims=True)
    o = (jnp.dot(p_loc.astype(BF16), vw, preferred_element_type=F32)
         + jnp.dot(p_ctx.astype(BF16), vc_ref[...].astype(BF16), preferred_element_type=F32))
    o_ref[...] = (o / l).astype(o_ref.dtype)


def _na_attn_call(proj, cache_k, cache_v, layer, rpb, nb, dec_seq, q_col, hd):
    rows = dec_seq // GRID_W
    assert rows >= NA_K_ROWS and rows % NA_Q_ROWS == 0
    nh = NA_HEADS
    nq = NA_Q_ROWS * GRID_W
    nkb = rows // NA_Q_ROWS
    past = cache_k.shape[3]
    tbl, var_ids = _na_bias_tables(rpb, rows)
    ctx_spec = pl.BlockSpec((None, None, None, past, hd), lambda b, h, k, var: (b, layer, h, 0, 0))
    grid_spec = pltpu.PrefetchScalarGridSpec(
        num_scalar_prefetch=1,
        grid=(nb, nh, nkb),
        in_specs=[pl.BlockSpec((nq, hd), lambda b, h, k, var: (b * nkb + k, q_col + h)),
                  pl.BlockSpec((dec_seq, hd), lambda b, h, k, var: (b, q_col + nh + h)),
                  pl.BlockSpec((dec_seq, hd), lambda b, h, k, var: (b, q_col + 2 * nh + h)),
                  ctx_spec, ctx_spec,
                  pl.BlockSpec((None, None, nq, NA_K_ROWS * GRID_W),
                               lambda b, h, k, var: (h, var[k], 0, 0))],
        out_specs=pl.BlockSpec((nq, hd), lambda b, h, k, var: (b * nkb + k, h)),
    )
    return pl.pallas_call(
        functools.partial(_na_attn_kernel, rows=rows, scale=hd ** -0.5),
        grid_spec=grid_spec,
        out_shape=jax.ShapeDtypeStruct((nb * dec_seq, nh * hd), BF16),
        compiler_params=_cparams("parallel", "parallel", "arbitrary"),
        name="neighbourhood_attention",
    )(var_ids, proj, proj, proj, cache_k, cache_v, tbl)


def _merge_kernel(fl_ref, fc_ref, p_ref, al_ref, ac_ref, c_ref, g0_ref, g1_ref, g2_ref, g3_ref, w_ref,
                  o_ref, *, n_lat_tiles):
    is_lat = pl.program_id(0) < n_lat_tiles
    f = jnp.where(is_lat, fl_ref[...], fc_ref[...])
    a = jnp.where(is_lat, al_ref[...], ac_ref[...])
    acc = None
    for n, (br, gt) in enumerate(((f, g0_ref), (p_ref[...], g1_ref), (a, g2_ref), (c_ref[...], g3_ref))):
        y = gt[...].astype(F32) * jnp.dot(br, w_ref[n], preferred_element_type=F32)
        acc = y if acc is None else acc + y
    o_ref[...] = acc.astype(o_ref.dtype)


def _merge_call(f_lat, f_ctx, o_p, a_lat, a_ctx, o_c, gates, w_branch, tm, tn):
    t, bw = o_p.shape
    d = w_branch.shape[2]
    nj = d // tn
    nl = f_lat.shape[0] // tm
    br_spec = pl.BlockSpec((tm, bw), lambda i, j: (i, 0))
    lat_spec = pl.BlockSpec((tm, bw), lambda i, j: (jnp.minimum(i, nl - 1), 0))
    ctx_spec = pl.BlockSpec((tm, bw), lambda i, j: (jnp.maximum(i - nl, 0), 0))
    gate_specs = [pl.BlockSpec((tm, tn), functools.partial(lambda i, j, n: (i, n * nj + j), n=n))
                  for n in range(N_BRANCH)]
    return pl.pallas_call(
        functools.partial(_merge_kernel, n_lat_tiles=nl),
        grid=(t // tm, nj),
        in_specs=[lat_spec, ctx_spec, br_spec, lat_spec, ctx_spec, br_spec] + gate_specs
                 + [pl.BlockSpec((N_BRANCH, bw, tn), lambda i, j: (0, 0, j))],
        out_specs=pl.BlockSpec((tm, tn), lambda i, j: (i, j)),
        out_shape=jax.ShapeDtypeStruct((t, d), BF16),
        compiler_params=_cparams("parallel", "parallel"),
        name="branch_merge",
    )(f_lat, f_ctx, o_p, a_lat, a_ctx, o_c, gates, gates, gates, gates, w_branch)


def _out_kernel(a_ref, w_ref, x_hbm, mod_ref, g_ref, *rest, tm, gate_row, next_rows):
    if next_rows is None:
        o_ref, xbuf_ref, sem = rest
    else:
        nmod_ref, ng_ref, o_ref, h_ref, xbuf_ref, sem = rest
    k = pl.program_id(1)
    x_copy = pltpu.make_async_copy(x_hbm.at[pl.ds(pl.program_id(0) * tm, tm)], xbuf_ref, sem)

    @pl.when(k == 0)
    def _():
        x_copy.start()
        o_ref[...] = jnp.zeros_like(o_ref)

    o_ref[...] += jnp.dot(a_ref[...], w_ref[...], preferred_element_type=F32)

    @pl.when(k == pl.num_programs(1) - 1)
    def _():
        x_copy.wait()
        xn = xbuf_ref[...] + mod_ref[gate_row:gate_row + 1, :] * _rms(o_ref[...], g_ref[...])
        o_ref[...] = xn
        if next_rows is not None:
            h_ref[...] = _mod_norm(xn, ng_ref[...], nmod_ref, *next_rows).astype(h_ref.dtype)


def _k_tile(kdim, cap):
    lanes = 128
    return max(t for t in range(lanes, cap + 1, lanes) if kdim % t == 0)


def _out_call(a, w, x, mod, g, gate_row, nxt, mod_row, tm):
    t, d = x.shape
    kdim = a.shape[1]
    tk = _k_tile(kdim, 1024)
    row = pl.BlockSpec((tm, d), lambda i, k: (i, 0))
    mod_spec = pl.BlockSpec((None, 6, d), lambda i, k: (mod_row(i * tm), 0, 0))
    vec = pl.BlockSpec((1, d), lambda i, k: (0, 0))
    in_specs = [pl.BlockSpec((tm, tk), lambda i, k: (i, k)),
                pl.BlockSpec((tk, d), lambda i, k: (k, 0)),
                pl.BlockSpec(memory_space=pl.ANY), mod_spec, vec]
    args = [a, w, x, mod, g]
    out_specs, out_shape = row, jax.ShapeDtypeStruct((t, d), F32)
    if nxt is not None:
        in_specs += [mod_spec, vec]
        args += [nxt[0], nxt[1]]
        out_specs = [row, row]
        out_shape = [out_shape, jax.ShapeDtypeStruct((t, d), BF16)]
    return pl.pallas_call(
        functools.partial(_out_kernel, tm=tm, gate_row=gate_row,
                          next_rows=None if nxt is None else nxt[2]),
        grid=(t // tm, kdim // tk),
        in_specs=in_specs,
        out_specs=out_specs,
        out_shape=out_shape,
        scratch_shapes=[pltpu.VMEM((tm, d), F32), pltpu.SemaphoreType.DMA(())],
        compiler_params=_cparams("arbitrary", "arbitrary", vmem_mib=VMEM_LIMIT_WIDE_MIB),
        name="proj_norm_residual",
    )(*args)


def _ffn_up_kernel(h_ref, wg_ref, wu_ref, *rest, n_cast):
    o_ref = rest[n_cast]
    _cast_stream_step(rest[:n_cast], rest[n_cast + 1:])
    h = h_ref[...]
    y = _silu(jnp.dot(h, wg_ref[...], preferred_element_type=F32)) * jnp.dot(
        h, wu_ref[...], preferred_element_type=F32)
    o_ref[...] = y.astype(o_ref.dtype)


def _ffn_up_call(h, wg, wu, tm, tn, cast_srcs):
    t, d = h.shape
    f = wg.shape[1]
    nj = f // tn
    w_spec = pl.BlockSpec((d, tn), lambda i, j: (0, j))
    cast_specs, cast_shapes = _cast_stream_specs(cast_srcs, (t // tm) * nj, nj)
    return pl.pallas_call(
        functools.partial(_ffn_up_kernel, n_cast=len(cast_srcs)),
        grid=(t // tm, nj),
        in_specs=[pl.BlockSpec((tm, d), lambda i, j: (i, 0)), w_spec, w_spec] + cast_specs,
        out_specs=[pl.BlockSpec((tm, tn), lambda i, j: (i, j))] + cast_specs,
        out_shape=[jax.ShapeDtypeStruct((t, f), BF16)] + cast_shapes,
        compiler_params=_cparams("arbitrary", "arbitrary"),
        name="ffn_up",
    )(h, wg, wu, *cast_srcs)


def _split_bf16(x):
    hi = x.astype(BF16)
    return hi, (x - hi.astype(F32)).astype(BF16)


def _pack_bf16_pairs(h):
    n = h.shape[1] // 2
    hb = h.astype(BF16).astype(F32)
    hi = pltpu.bitcast(hb[:, :n], jnp.int32)
    lo = pltpu.bitcast(hb[:, n:], jnp.int32)
    return hi | lax.shift_right_logical(lo, 16)


def _unpack_bf16_pairs(w):
    hi = pltpu.bitcast(w & jnp.int32(-65536), F32).astype(BF16)
    lo = pltpu.bitcast(w << 16, F32).astype(BF16)
    return hi, lo


def _router_kernel(x_ref, mod_ref, g_ref, w_ref, b_ref, o_ref, hp_ref, *, n_experts):
    h = _mod_norm(x_ref[...], g_ref[...], mod_ref, SCALE2, SHIFT2)
    hp_ref[...] = _pack_bf16_pairs(h)
    h_hi, h_lo = _split_bf16(h)
    w_hi, w_lo = _split_bf16(w_ref[...])
    logits = (jnp.dot(h_hi, w_hi, preferred_element_type=F32)
              + jnp.dot(h_lo, w_hi, preferred_element_type=F32)
              + jnp.dot(h_hi, w_lo, preferred_element_type=F32)) + b_ref[...]
    lane = lax.broadcasted_iota(jnp.int32, logits.shape, 1).astype(F32)
    neg = -jnp.inf
    no_lane = float(logits.shape[1])
    logits = jnp.where(lane < n_experts, logits, neg)
    m1 = jnp.max(logits, axis=-1, keepdims=True)
    i1 = jnp.min(jnp.where(logits == m1, lane, no_lane), axis=-1, keepdims=True)
    rest = jnp.where(lane == i1, neg, logits)
    m2 = jnp.max(rest, axis=-1, keepdims=True)
    i2 = jnp.min(jnp.where(rest == m2, lane, no_lane), axis=-1, keepdims=True)
    e2 = jnp.exp(m2 - m1)
    den = 1.0 + e2
    o_ref[...] = (jnp.where(lane == 0.0, i1, 0.0) + jnp.where(lane == 1.0, i2, 0.0)
                  + jnp.where(lane == 2.0, 1.0 / den, 0.0) + jnp.where(lane == 3.0, e2 / den, 0.0))


def _router_call(x, mod, g, w_router, b_router, mod_row, tm):
    t, d = x.shape
    ne = w_router.shape[1]
    lanes = 128
    w = jnp.zeros((d, lanes), F32).at[:, :ne].set(w_router)
    b = jnp.zeros((1, lanes), F32).at[0, :ne].set(b_router)
    return pl.pallas_call(
        functools.partial(_router_kernel, n_experts=ne),
        grid=(t // tm,),
        in_specs=[pl.BlockSpec((tm, d), lambda i: (i, 0)),
                  pl.BlockSpec((None, 6, d), lambda i: (mod_row(i * tm), 0, 0)),
                  pl.BlockSpec((1, d), lambda i: (0, 0)),
                  pl.BlockSpec((d, lanes), lambda i: (0, 0)),
                  pl.BlockSpec((1, lanes), lambda i: (0, 0))],
        out_specs=[pl.BlockSpec((tm, lanes), lambda i: (i, 0)),
                   pl.BlockSpec((tm, d // 2), lambda i: (i, 0))],
        out_shape=[jax.ShapeDtypeStruct((t, lanes), F32),
                   jax.ShapeDtypeStruct((t, d // 2), jnp.int32)],
        compiler_params=_cparams("parallel"),
        name="router",
    )(x, mod, g, w, b)


def _routing_tables(route, n_experts, tm):
    t = route.shape[0]
    na = TOP_K * t
    p_rows = na + n_experts * tm
    nt = p_rows // tm
    e_flat = route[:, :TOP_K].astype(jnp.int32).reshape(na)
    onehot = (e_flat[:, None] == jnp.arange(n_experts, dtype=jnp.int32)[None, :]).astype(jnp.int32)
    csum = jnp.cumsum(onehot, axis=0)
    rank = jnp.sum((csum - onehot) * onehot, axis=1)
    counts = csum[-1]
    padded = ((counts + tm - 1) // tm) * tm
    ends = jnp.cumsum(padded)
    pos = (ends - padded)[e_flat] + rank
    src = jnp.zeros((p_rows,), jnp.int32).at[pos].set(
        jnp.arange(na, dtype=jnp.int32) // TOP_K, unique_indices=True, mode="promise_in_bounds")
    tile_expert = jnp.minimum(
        jnp.searchsorted(ends, jnp.arange(nt, dtype=jnp.int32) * tm, side="right"), n_experts - 1)
    meta = jnp.concatenate([tile_expert.astype(jnp.int32), (ends[-1:] // tm).astype(jnp.int32)])
    return src, pos.astype(jnp.int32), meta


def _gather_rows_kernel(idx_ref, src_ref, o_ref, buf_ref, sem_ref, *, rows):
    i = pl.program_id(0)
    n = pl.num_programs(0)

    def issue(tile, slot):
        def body(r, carry):
            pltpu.make_async_copy(src_ref.at[pl.ds(idx_ref[tile * rows + r], 1)],
                                  buf_ref.at[slot, pl.ds(r, 1)], sem_ref.at[slot]).start()
            return carry
        lax.fori_loop(0, rows, body, 0, unroll=DMA_ISSUE_UNROLL)

    @pl.when(i == 0)
    def _():
        issue(0, 0)

    @pl.when(i + 1 < n)
    def _():
        issue(i + 1, (i + 1) % 2)

    slot = i % 2
    pltpu.make_async_copy(src_ref.at[pl.ds(0, rows)], buf_ref.at[slot], sem_ref.at[slot]).wait()
    half = buf_ref.shape[2]
    hi, lo = _unpack_bf16_pairs(buf_ref[slot])
    o_ref[:, :half] = hi
    o_ref[:, half:] = lo


def _gather_rows_call(src, idx, rows):
    p_rows = idx.shape[0]
    half = src.shape[1]
    grid_spec = pltpu.PrefetchScalarGridSpec(
        num_scalar_prefetch=1,
        grid=(p_rows // rows,),
        in_specs=[pl.BlockSpec(memory_space=pl.ANY)],
        out_specs=pl.BlockSpec((rows, 2 * half), lambda i, idx: (i, 0)),
        scratch_shapes=[pltpu.VMEM((2, rows, half), src.dtype), pltpu.SemaphoreType.DMA((2,))],
    )
    return pl.pallas_call(
        functools.partial(_gather_rows_kernel, rows=rows),
        grid_spec=grid_spec,
        out_shape=jax.ShapeDtypeStruct((p_rows, 2 * half), BF16),
        compiler_params=_cparams("arbitrary"),
        name="moe_gather",
    )(idx, src)


def _moe_up_kernel(meta_ref, xs_ref, wg_ref, wu_ref, o_ref, wgb_ref, wub_ref, *, n_tiles):
    i = pl.program_id(1)
    used = i < meta_ref[n_tiles]
    fresh = jnp.logical_or(i == 0, meta_ref[i] != meta_ref[jnp.maximum(i - 1, 0)])

    @pl.when(jnp.logical_and(used, fresh))
    def _():
        wgb_ref[...] = wg_ref[...].astype(BF16)
        wub_ref[...] = wu_ref[...].astype(BF16)

    @pl.when(used)
    def _():
        h = xs_ref[...]
        y = _silu(jnp.dot(h, wgb_ref[...], preferred_element_type=F32)) * jnp.dot(
            h, wub_ref[...], preferred_element_type=F32)
        o_ref[...] = y.astype(o_ref.dtype)

    @pl.when(jnp.logical_not(used))
    def _():
        o_ref[...] = jnp.zeros_like(o_ref)


def _moe_up_call(xs, meta, wg, wu, tm, tn):
    p_rows, d = xs.shape
    f = wg.shape[2]
    nt = p_rows // tm
    w_spec = pl.BlockSpec((None, d, tn), lambda j, i, meta: (meta[i], 0, j))
    grid_spec = pltpu.PrefetchScalarGridSpec(
        num_scalar_prefetch=1,
        grid=(f // tn, nt),
        in_specs=[pl.BlockSpec((tm, d), lambda j, i, meta: (i, 0)), w_spec, w_spec],
        out_specs=pl.BlockSpec((None, tm, tn), lambda j, i, meta: (j, i, 0)),
        scratch_shapes=[pltpu.VMEM((d, tn), BF16), pltpu.VMEM((d, tn), BF16)],
    )
    return pl.pallas_call(
        functools.partial(_moe_up_kernel, n_tiles=nt),
        grid_spec=grid_spec,
        out_shape=jax.ShapeDtypeStruct((f // tn, p_rows, tn), BF16),
        compiler_params=_cparams("arbitrary", "arbitrary", vmem_mib=VMEM_LIMIT_WIDE_MIB),
        name="moe_up",
    )(meta, xs, wg, wu)


def _moe_down_kernel(meta_ref, a_ref, w_ref, o_ref, *, n_tiles):
    i = pl.program_id(0)

    @pl.when(pl.program_id(1) == 0)
    def _():
        o_ref[...] = jnp.zeros_like(o_ref)

    @pl.when(i < meta_ref[n_tiles])
    def _():
        nsub, _, tk = a_ref.shape
        acc = o_ref[...]
        for s in range(nsub):
            acc = acc + jnp.dot(a_ref[s], w_ref[s * tk:(s + 1) * tk, :], preferred_element_type=F32)
        o_ref[...] = acc


def _moe_down_call(a, meta, wd, tm, nsub):
    nk, p_rows, tk = a.shape
    d = wd.shape[2]
    nt = p_rows // tm
    grid_spec = pltpu.PrefetchScalarGridSpec(
        num_scalar_prefetch=1,
        grid=(nt, nk // nsub),
        in_specs=[pl.BlockSpec((nsub, tm, tk), lambda i, k, meta: (k, i, 0)),
                  pl.BlockSpec((None, nsub * tk, d), lambda i, k, meta: (meta[i], k, 0))],
        out_specs=pl.BlockSpec((tm, d), lambda i, k, meta: (i, 0)),
    )
    return pl.pallas_call(
        functools.partial(_moe_down_kernel, n_tiles=nt),
        grid_spec=grid_spec,
        out_shape=jax.ShapeDtypeStruct((p_rows, d), F32),
        compiler_params=_cparams("parallel", "arbitrary"),
        name="moe_down",
    )(meta, a, wd)


def _moe_combine_kernel(pos_ref, ys_ref, route_ref, x_ref, mod_ref, g_ref, *rest, rows, n_lat_tiles):
    if n_lat_tiles is None:
        o_ref, buf_ref, sem_ref = rest
    else:
        lat_ref, ctx_ref, buf_ref, sem_ref = rest
    i = pl.program_id(0)
    n = pl.num_programs(0)

    def issue(tile, slot):
        def body(r, carry):
            for s in range(TOP_K):
                pltpu.make_async_copy(ys_ref.at[pl.ds(pos_ref[TOP_K * (tile * rows + r) + s], 1)],
                                      buf_ref.at[slot, s, pl.ds(r, 1)], sem_ref.at[slot]).start()
            return carry
        lax.fori_loop(0, rows, body, 0, unroll=DMA_ISSUE_UNROLL)

    @pl.when(i == 0)
    def _():
        issue(0, 0)

    @pl.when(i + 1 < n)
    def _():
        issue(i + 1, (i + 1) % 2)

    slot = i % 2
    for s in range(TOP_K):
        pltpu.make_async_copy(ys_ref.at[pl.ds(0, rows)], buf_ref.at[slot, s], sem_ref.at[slot]).wait()
    route = route_ref[...]
    y = None
    for s in range(TOP_K):
        term = route[:, TOP_K + s:TOP_K + s + 1] * buf_ref[slot, s]
        y = term if y is None else y + term
    res = x_ref[...] + mod_ref[GATE2:GATE2 + 1, :] * _rms(y, g_ref[...])
    if n_lat_tiles is None:
        o_ref[...] = res
    else:
        @pl.when(i < n_lat_tiles)
        def _():
            lat_ref[...] = res

        @pl.when(i >= n_lat_tiles)
        def _():
            ctx_ref[...] = res


def _moe_combine_call(ys, pos, route, x, mod, g, mod_row, rows, split_rows):
    t, d = x.shape
    row = pl.BlockSpec((rows, d), lambda i, pos: (i, 0))
    if split_rows is None:
        nl = None
        out_specs, out_shape = row, jax.ShapeDtypeStruct((t, d), F32)
    else:
        nl = split_rows // rows
        out_specs = [pl.BlockSpec((rows, d), lambda i, pos: (jnp.minimum(i, nl - 1), 0)),
                     pl.BlockSpec((rows, d), lambda i, pos: (jnp.maximum(i - nl, 0), 0))]
        out_shape = [jax.ShapeDtypeStruct((split_rows, d), F32),
                     jax.ShapeDtypeStruct((t - split_rows, d), F32)]
    grid_spec = pltpu.PrefetchScalarGridSpec(
        num_scalar_prefetch=1,
        grid=(t // rows,),
        in_specs=[pl.BlockSpec(memory_space=pl.ANY),
                  pl.BlockSpec((rows, route.shape[1]), lambda i, pos: (i, 0)),
                  row,
                  pl.BlockSpec((None, 6, d), lambda i, pos: (mod_row(i * rows), 0, 0)),
                  pl.BlockSpec((1, d), lambda i, pos: (0, 0))],
        out_specs=out_specs,
        scratch_shapes=[pltpu.VMEM((2, TOP_K, rows, d), F32), pltpu.SemaphoreType.DMA((2,))],
    )
    return pl.pallas_call(
        functools.partial(_moe_combine_kernel, rows=rows, n_lat_tiles=nl),
        grid_spec=grid_spec,
        out_shape=out_shape,
        compiler_params=_cparams("arbitrary"),
        name="moe_combine",
    )(pos, ys, route, x, mod, g)


def kernel(x_prompt, x_sample, cache_k, cache_v, c, c_ctx, w_ada, b_ada, norm_g, w_in, b_in, pool_w, pool_scale, rpb, conv_w, conv_b, conv_ln_g, conv_ln_b, w_branch, w_out, w_gate_d, w_up_d, w_down_d, w_router, b_router, w_gate_e, w_up_e, w_down_e):
    nbp, seq, d = x_prompt.shape
    nbs, dec_seq, _ = x_sample.shape
    depth = w_ada.shape[0]
    bw = d // N_BRANCH
    hd = bw // NA_HEADS
    ns_rows = nbs * dec_seq
    np_rows = nbp * seq
    assert dec_seq & (dec_seq - 1) == 0 and seq & (seq - 1) == 0
    assert nbs + 1 <= MOD_ROWS

    t_rows = ns_rows + np_rows
    tm_wide = max(tm for tm in (2048, 1024, 512) if dec_seq % tm == 0 and np_rows % tm == 0)
    tm_mid = min(tm_wide, 1024)
    tm_moe = 512
    tile_seq = min(256, seq)
    tn = 512
    assert t_rows % tm_wide == 0 and seq % tile_seq == 0

    def mod_row(row0):
        return jnp.where(row0 < ns_rows, row0 // dec_seq, nbs)

    cvec =jnp.zeros((MOD_ROWS, d), F32).at[:nbs].set(c).at[nbs].set(c_ctx)
    mods = _ada_call(cvec, w_ada, b_ada).reshape(depth, MOD_ROWS, 6, d)
    cc, sc = _channel_dft_mats(bw, FNET_GROUPS)

    pool_col, conv_col, main_cols = 1, 5, 7 * bw
    q_col = 2 * bw // hd

    def gain(l, n):
        return norm_g[l, n][None, :]

    x, h = _entry_call(x_sample.reshape(ns_rows, d), x_prompt.reshape(np_rows, d), mods[0], gain(0, 0),
                       mod_row, 512)
    w_in_cur = w_in[0].astype(BF16)[None]
    new_k, new_v = [], []
    y_split = None
    for l in range(depth):
        mod = mods[l]
        last = l + 1 == depth
        i = l // 2
        if l % 2 == 1:
            mixer_w = [w_down_e[i].reshape(-1, d)]
        else:
            mixer_w = [w_gate_d[i], w_up_d[i], w_down_d[i]]
        proj, gates, *mixer_w = _in_call(h, w_in_cur, b_in[l][None, :], 0, main_cols, tm_wide, tn, mixer_w)

        f_lat = _fourier_call(proj, 0, nbs, dec_seq, bw, cc, sc)
        f_ctx = _fourier_call(proj, ns_rows, nbp, seq, bw, cc, sc)
        o_p = _pool_call(proj, pool_w[l].astype(BF16), pool_scale[l][None, :], pool_col, bw,
                         tile_seq, ns_rows, dec_seq, seq)
        o_c = _conv_call(proj, conv_w[l], conv_b[l][None, :], conv_ln_g[l][None, :],
                         conv_ln_b[l][None, :], conv_col, bw, tile_seq, ns_rows, dec_seq, seq)
        a_lat = _na_attn_call(proj, cache_k, cache_v, l, rpb[l], nbs, dec_seq, q_col, hd)
        a_ctx, k_ctx, v_ctx = _ctx_attn_call(proj, ns_rows, nbp, seq, q_col, hd)
        new_k.append(k_ctx)
        new_v.append(v_ctx)

        merged = _merge_call(f_lat, f_ctx, o_p, a_lat, a_ctx, o_c, gates, w_branch[l].astype(BF16),
                             tm_mid, tn)
        nxt = None if last else (mods[l + 1], gain(l + 1, 0), (SCALE1, SHIFT1))
        if l % 2 == 1:
            x = _out_call(merged, w_out[l].astype(BF16), x, mod, gain(l, 1), GATE1, None,
                          mod_row, tm_mid)
            ne = w_router.shape[2]
            route, h_packed = _router_call(x, mod, gain(l, 2), w_router[i], b_router[i], mod_row, 512)
            src, pos, meta = _routing_tables(route, ne, tm_moe)
            xs = _gather_rows_call(h_packed, src, 256)
            hmid = _moe_up_call(xs, meta, w_gate_e[i], w_up_e[i], tm_moe, 2 * tn)
            ys = _moe_down_call(hmid, meta, mixer_w[0].reshape(w_down_e.shape[1:]), tm_moe, 1)
            out = _moe_combine_call(ys, pos, route, x, mod, gain(l, 3), mod_row, 256,
                                    ns_rows if last else None)
            if last:
                y_split = out
            else:
                x = out
                h = _norm_call(x, nxt[0], nxt[1], *nxt[2], mod_row, tm_mid)
                w_in_cur = w_in[l + 1].astype(BF16)[None]
        else:
            wg, wu, wd = mixer_w
            x, h2 = _out_call(merged, w_out[l].astype(BF16), x, mod, gain(l, 1), GATE1,
                              (mod, gain(l, 2), (SCALE2, SHIFT2)), mod_row, tm_mid)
            if last:
                (hmid,) = _ffn_up_call(h2, wg, wu, tm_wide, tn, [])
                x = _out_call(hmid, wd, x, mod, gain(l, 3), GATE2, None, mod_row, tm_mid)
            else:
                hmid, w_in_next = _ffn_up_call(h2, wg, wu, tm_wide, tn, [w_in[l + 1]])
                w_in_cur = w_in_next[None]
                x, h = _out_call(hmid, wd, x, mod, gain(l, 3), GATE2, nxt, mod_row, tm_mid)

    if y_split is None:
        y_split = x[:ns_rows], x[ns_rows:]
    y_sample = y_split[0].reshape(nbs, dec_seq, d)
    y_prompt = y_split[1].reshape(nbp, seq, d)
    return y_prompt, y_sample, jnp.stack(new_k, axis=1), jnp.stack(new_v, axis=1)
```

```python
import functools

import numpy as np
import jax
import jax.numpy as jnp
from jax import lax
from jax.experimental import pallas as pl
from jax.experimental.pallas import tpu as pltpu

F32 = jnp.float32
BF16 = jnp.bfloat16

N_BRANCH = 4
FNET_GROUPS = 4
POOL_WINDOWS = (2, 4, 8, 16)
NA_HEADS = 4
NA_WIN_ROWS = 8
NA_WIN_COLS = 16
GRID_W = 64
CONV_K = 31
TOP_K = 2
RMS_EPS = 1e-6
LN_EPS = 1e-5
MASK_VALUE = -1e30
MOD_ROWS = 8
POOL_HALO = 8
CONV_HALO = 16
NA_Q_ROWS = 8
NA_K_ROWS = 16
DMA_ISSUE_UNROLL = 8
VMEM_LIMIT_MIB = 48
VMEM_LIMIT_WIDE_MIB = 56

SHIFT1, SCALE1, GATE1, SHIFT2, SCALE2, GATE2 = range(6)


def _cparams(*sem, vmem_mib=VMEM_LIMIT_MIB):
    return pltpu.CompilerParams(dimension_semantics=sem, vmem_limit_bytes=vmem_mib * 1024 * 1024)


def _sigmoid(x):
    return 0.5 * jnp.tanh(0.5 * x) + 0.5


def _silu(x):
    return x * _sigmoid(x)


def _rms(x, g):
    return x * lax.rsqrt(jnp.mean(x * x, axis=-1, keepdims=True) + RMS_EPS) * g


def _mod_norm(x, g, mod_ref, scale_row, shift_row):
    return (_rms(x, g) * (1.0 + mod_ref[scale_row:scale_row + 1, :])
            + mod_ref[shift_row:shift_row + 1, :])


def _nt_dot(a, b):
    return lax.dot_general(a, b, (((1,), (1,)), ((), ())), preferred_element_type=F32)


def _ada_kernel(c_ref, w_ref, b_ref, o_ref):
    s = _silu(c_ref[...]).astype(BF16)
    o_ref[...] = jnp.dot(s, w_ref[...].astype(BF16), preferred_element_type=F32) + b_ref[...]


def _ada_call(cvec, w_ada, b_ada):
    depth, d, n = w_ada.shape
    tn = 512
    return pl.pallas_call(
        _ada_kernel,
        grid=(depth, n // tn),
        in_specs=[pl.BlockSpec((MOD_ROWS, d), lambda l, j: (0, 0)),
                  pl.BlockSpec((None, d, tn), lambda l, j: (l, 0, j)),
                  pl.BlockSpec((None, 1, tn), lambda l, j: (l, 0, j))],
        out_specs=pl.BlockSpec((None, MOD_ROWS, tn), lambda l, j: (l, 0, j)),
        out_shape=jax.ShapeDtypeStruct((depth, MOD_ROWS, n), F32),
        compiler_params=_cparams("parallel", "parallel"),
        name="adaln",
    )(cvec, w_ada, b_ada.reshape(depth, 1, n))


def _entry_kernel(lat_ref, ctx_ref, mod_ref, g_ref, x_ref, h_ref, *, n_lat_tiles):
    x = jnp.where(pl.program_id(0) < n_lat_tiles, lat_ref[...], ctx_ref[...])
    x_ref[...] = x
    h_ref[...] = _mod_norm(x, g_ref[...], mod_ref, SCALE1, SHIFT1).astype(h_ref.dtype)


def _entry_call(x_lat, x_ctx, mod, g, mod_row, tm):
    d = x_lat.shape[1]
    nl = x_lat.shape[0] // tm
    t = x_lat.shape[0] + x_ctx.shape[0]
    row = pl.BlockSpec((tm, d), lambda i: (i, 0))
    return pl.pallas_call(
        functools.partial(_entry_kernel, n_lat_tiles=nl),
        grid=(t // tm,),
        in_specs=[pl.BlockSpec((tm, d), lambda i: (jnp.minimum(i, nl - 1), 0)),
                  pl.BlockSpec((tm, d), lambda i: (jnp.maximum(i - nl, 0), 0)),
                  pl.BlockSpec((None, 6, d), lambda i: (mod_row(i * tm), 0, 0)),
                  pl.BlockSpec((1, d), lambda i: (0, 0))],
        out_specs=[row, row],
        out_shape=[jax.ShapeDtypeStruct((t, d), F32), jax.ShapeDtypeStruct((t, d), BF16)],
        compiler_params=_cparams("parallel"),
        name="entry_norm",
    )(x_lat, x_ctx, mod, g)


def _norm_kernel(x_ref, mod_ref, g_ref, o_ref, *, scale_row, shift_row):
    o_ref[...] = _mod_norm(x_ref[...], g_ref[...], mod_ref, scale_row, shift_row).astype(o_ref.dtype)


def _norm_call(x, mod, g, scale_row, shift_row, mod_row, tm):
    t, d = x.shape
    return pl.pallas_call(
        functools.partial(_norm_kernel, scale_row=scale_row, shift_row=shift_row),
        grid=(t // tm,),
        in_specs=[pl.BlockSpec((tm, d), lambda i: (i, 0)),
                  pl.BlockSpec((None, 6, d), lambda i: (mod_row(i * tm), 0, 0)),
                  pl.BlockSpec((1, d), lambda i: (0, 0))],
        out_specs=pl.BlockSpec((tm, d), lambda i: (i, 0)),
        out_shape=jax.ShapeDtypeStruct((t, d), BF16),
        compiler_params=_cparams("parallel"),
        name="mod_norm",
    )(x, mod, g)


def _cast_stream_specs(srcs, steps, nj):
    bf16_rows = 16
    in_specs, out_specs, shapes = [], [], []
    for a, row0, rows in srcs:
        cols = a.shape[1]
        chunk = min(c for c in range(bf16_rows, rows + 1, bf16_rows)
                    if rows % c == 0 and row0 % c == 0 and rows // c <= steps)
        walk = functools.partial(lambda i, j, first, last: (first + jnp.minimum(i * nj + j, last), 0),
                                 last=rows // chunk - 1)
        in_specs.append(pl.BlockSpec((chunk, cols), functools.partial(walk, first=row0 // chunk)))
        out_specs.append(pl.BlockSpec((chunk, cols), functools.partial(walk, first=0)))
        shapes.append(jax.ShapeDtypeStruct((rows, cols), BF16))
    return in_specs, out_specs, shapes


def _cast_stream_step(src_refs, dst_refs):
    for s, d in zip(src_refs, dst_refs):
        d[...] = s[...].astype(d.dtype)


def _in_kernel(h_ref, w_ref, b_ref, *rest, n_main, n_cast):
    main_ref, gate_ref = rest[n_cast:n_cast + 2]
    _cast_stream_step(rest[:n_cast], rest[n_cast + 2:])
    j = pl.program_id(1)
    y = jnp.dot(h_ref[...], w_ref[...], preferred_element_type=F32) + b_ref[...]

    @pl.when(j < n_main)
    def _():
        main_ref[...] = y

    @pl.when(j >= n_main)
    def _():
        gate_ref[...] = _sigmoid(y).astype(gate_ref.dtype)


def _in_call(h, w, b, layer, main_cols, tm, tn, cast_srcs):
    t, d = h.shape
    n = w.shape[2]
    n_main = main_cols // tn
    nj = n // tn
    steps = (t // tm) * nj
    in_specs = [pl.BlockSpec((tm, d), lambda i, j: (i, 0)),
                pl.BlockSpec((None, d, tn), lambda i, j: (layer, 0, j)),
                pl.BlockSpec((1, tn), lambda i, j: (0, j))]
    out_specs = [pl.BlockSpec((tm, tn), lambda i, j: (i, jnp.minimum(j, n_main - 1))),
                 pl.BlockSpec((tm, tn), lambda i, j: (i, jnp.maximum(j - n_main, 0)))]
    out_shape = [jax.ShapeDtypeStruct((t, main_cols), F32),
                 jax.ShapeDtypeStruct((t, n - main_cols), BF16)]
    cast_in, cast_out, cast_shapes = _cast_stream_specs(cast_srcs, steps, nj)
    return pl.pallas_call(
        functools.partial(_in_kernel, n_main=n_main, n_cast=len(cast_srcs)),
        grid=(t // tm, nj),
        in_specs=in_specs + cast_in,
        out_specs=out_specs + cast_out,
        out_shape=out_shape + cast_shapes,
        compiler_params=_cparams("arbitrary", "arbitrary"),
        name="in_proj",
    )(h, w, b, *[src[0] for src in cast_srcs])


def _dft_mats(n):
    scale = 1.0 / np.sqrt(n)
    j = jnp.arange(n, dtype=jnp.int32)
    if n <= 1024:
        ang = ((j[:, None] * j[None, :]) % n).astype(F32) * (2.0 * np.pi / n)
        return (jnp.cos(ang) * scale).astype(BF16), (jnp.sin(ang) * scale).astype(BF16)
    base = 64
    hi = n // base
    k1 = jnp.arange(hi, dtype=jnp.int32)
    k0 = jnp.arange(base, dtype=jnp.int32)
    a = ((j[:, None] * k1[None, :]) % hi).astype(F32) * (2.0 * np.pi / hi)
    b = ((j[:, None] * k0[None, :]) % n).astype(F32) * (2.0 * np.pi / n)
    ca, sa = jnp.cos(a)[:, :, None], jnp.sin(a)[:, :, None]
    cb, sb = jnp.cos(b)[:, None, :] * scale, jnp.sin(b)[:, None, :] * scale
    c = (ca * cb - sa * sb).reshape(n, n)
    s = (sa * cb + ca * sb).reshape(n, n)
    return c.astype(BF16), s.astype(BF16)


def _channel_dft_mats(width, groups):
    gw = width // groups
    k = np.arange(gw)
    ang = 2.0 * np.pi * ((k[:, None] * k[None, :]) % gw) / gw
    c = np.zeros((width, width), np.float32)
    s = np.zeros((width, width), np.float32)
    for g in range(groups):
        sl = slice(g * gw, (g + 1) * gw)
        c[sl, sl] = np.cos(ang) / np.sqrt(gw)
        s[sl, sl] = np.sin(ang) / np.sqrt(gw)
    return jnp.asarray(c, BF16), jnp.asarray(s, BF16)


def _fnet1_kernel(u_ref, cc_ref, sc_ref, vc_ref, vs_ref):
    u = u_ref[...].astype(BF16)
    vc_ref[...] = jnp.dot(u, cc_ref[...], preferred_element_type=F32).astype(BF16)
    vs_ref[...] = jnp.dot(u, sc_ref[...], preferred_element_type=F32).astype(BF16)


def _fnet2_kernel(c_ref, s_ref, vc_ref, vs_ref, o_ref, acc_ref):
    k = pl.program_id(2)

    @pl.when(k == 0)
    def _():
        acc_ref[...] = jnp.zeros_like(acc_ref)

    acc_ref[...] += (jnp.dot(c_ref[...], vc_ref[...], preferred_element_type=F32)
                     - jnp.dot(s_ref[...], vs_ref[...], preferred_element_type=F32))

    @pl.when(k == pl.num_programs(2) - 1)
    def _():
        o_ref[...] = acc_ref[...].astype(o_ref.dtype)


def _fourier_call(proj, row_base, nb, seq, bw, cc, sc):
    c_l, s_l = _dft_mats(seq)
    tm1 = min(seq, 512)
    nt1 = seq // tm1
    base1 = row_base // tm1
    vc, vs = pl.pallas_call(
        _fnet1_kernel,
        grid=(nb, nt1),
        in_specs=[pl.BlockSpec((tm1, bw), lambda b, t: (base1 + b * nt1 + t, 0)),
                  pl.BlockSpec((bw, bw), lambda b, t: (0, 0)),
                  pl.BlockSpec((bw, bw), lambda b, t: (0, 0))],
        out_specs=[pl.BlockSpec((tm1, bw), lambda b, t: (t, b)),
                   pl.BlockSpec((tm1, bw), lambda b, t: (t, b))],
        out_shape=[jax.ShapeDtypeStruct((seq, nb * bw), BF16)] * 2,
        compiler_params=_cparams("parallel", "parallel"),
        name="fnet_channels",
    )(proj, cc, sc)
    tm2 = min(seq, 1024)
    tk = min(seq, 2048)
    nt2 = seq // tm2
    return pl.pallas_call(
        _fnet2_kernel,
        grid=(nt2, nb, seq // tk),
        in_specs=[pl.BlockSpec((tm2, tk), lambda i, j, k: (i, k)),
                  pl.BlockSpec((tm2, tk), lambda i, j, k: (i, k)),
                  pl.BlockSpec((tk, bw), lambda i, j, k: (k, j)),
                  pl.BlockSpec((tk, bw), lambda i, j, k: (k, j))],
        out_specs=pl.BlockSpec((tm2, bw), lambda i, j, k: (j * nt2 + i, 0)),
        out_shape=jax.ShapeDtypeStruct((nb * seq, bw), BF16),
        scratch_shapes=[pltpu.VMEM((tm2, bw), F32)],
        compiler_params=_cparams("parallel", "parallel", "arbitrary"),
        name="fnet_positions",
    )(c_l, s_l, vc, vs)


def _seq_position(row0, ns_rows, dec_seq, seq):
    is_lat = row0 < ns_rows
    seq_len = jnp.where(is_lat, dec_seq, seq)
    pos0 = jnp.where(is_lat, row0 & (dec_seq - 1), row0 & (seq - 1))
    return seq_len, pos0


def _pool_kernel(prev_ref, cur_ref, nxt_ref, w_ref, sc_ref, o_ref, ext_ref, *,
                 tm, ns_rows, dec_seq, seq):
    seq_len, pos0 = _seq_position(pl.program_id(0) * tm, ns_rows, dec_seq, seq)
    h = POOL_HALO
    ext_ref[0:h, :] = jnp.where(pos0 == 0, 0.0, prev_ref[...])
    ext_ref[h:h + tm, :] = cur_ref[...]
    ext_ref[h + tm:2 * h + tm, :] = jnp.where(pos0 + tm == seq_len, 0.0, nxt_ref[...])
    t = pos0 + lax.broadcasted_iota(jnp.int32, (tm, 1), 0)
    gw = cur_ref.shape[1] // len(POOL_WINDOWS)
    for gi, win in enumerate(POOL_WINDOWS):
        lo = win // 2
        hi = win - lo
        cols = slice(gi * gw, (gi + 1) * gw)
        s = ext_ref[h - lo:h - lo + tm, cols]
        for j in range(1 - lo, hi):
            s = s + ext_ref[h + j:h + j + tm, cols]
        cnt = (jnp.minimum(t + hi, seq_len) - jnp.maximum(t - lo, 0)).astype(F32)
        dlt = s / cnt - cur_ref[:, cols]
        y = jnp.dot(dlt.astype(BF16), w_ref[gi], preferred_element_type=F32) * sc_ref[:, cols]
        o_ref[:, cols] = y.astype(o_ref.dtype)


def _pool_call(proj, pool_w, pool_scale, col_blk, bw, tm, ns_rows, dec_seq, seq):
    t = proj.shape[0]
    h = POOL_HALO
    r = tm // h
    last = t // h - 1
    gw = bw // len(POOL_WINDOWS)
    return pl.pallas_call(
        functools.partial(_pool_kernel, tm=tm, ns_rows=ns_rows, dec_seq=dec_seq, seq=seq),
        grid=(t // tm,),
        in_specs=[pl.BlockSpec((h, bw), lambda i: (jnp.maximum(i * r - 1, 0), col_blk)),
                  pl.BlockSpec((tm, bw), lambda i: (i, col_blk)),
                  pl.BlockSpec((h, bw), lambda i: (jnp.minimum((i + 1) * r, last), col_blk)),
                  pl.BlockSpec((len(POOL_WINDOWS), gw, gw), lambda i: (0, 0, 0)),
                  pl.BlockSpec((1, bw), lambda i: (0, 0))],
        out_specs=pl.BlockSpec((tm, bw), lambda i: (i, 0)),
        out_shape=jax.ShapeDtypeStruct((t, bw), BF16),
        scratch_shapes=[pltpu.VMEM((tm + 2 * h, bw), F32)],
        compiler_params=_cparams("parallel"),
        name="pool_mix",
    )(proj, proj, proj, pool_w, pool_scale)


def _conv_kernel(ap_ref, ac_ref, an_ref, gp_ref, gc_ref, gn_ref, w_ref, b_ref, lg_ref, lb_ref,
                 o_ref, ext_ref, y_ref, *, tm, ns_rows, dec_seq, seq):
    seq_len, pos0 = _seq_position(pl.program_id(0) * tm, ns_rows, dec_seq, seq)
    h = CONV_HALO
    ext_ref[0:h, :] = jnp.where(pos0 == 0, 0.0, ap_ref[...] * _sigmoid(gp_ref[...]))
    ext_ref[h:h + tm, :] = ac_ref[...] * _sigmoid(gc_ref[...])
    ext_ref[h + tm:2 * h + tm, :] = jnp.where(pos0 + tm == seq_len, 0.0,
                                              an_ref[...] * _sigmoid(gn_ref[...]))
    bw = ac_ref.shape[1]
    lanes = 128
    for c in range(bw // lanes):
        cols = slice(c * lanes, (c + 1) * lanes)
        acc = jnp.zeros((tm, lanes), F32)
        for k in range(CONV_K):
            off = h + k - CONV_K // 2
            acc = acc + ext_ref[off:off + tm, cols] * w_ref[k:k + 1, cols]
        y_ref[:, cols] = acc + b_ref[:, cols]
    y = y_ref[...]
    mu = jnp.mean(y, axis=-1, keepdims=True)
    var = jnp.mean(jnp.square(y - mu), axis=-1, keepdims=True)
    z = (y - mu) * lax.rsqrt(var + LN_EPS) * lg_ref[...] + lb_ref[...]
    o_ref[...] = _silu(z).astype(o_ref.dtype)


def _conv_call(proj, conv_w, conv_b, ln_g, ln_b, col_blk, bw, tm, ns_rows, dec_seq, seq):
    t = proj.shape[0]
    h = CONV_HALO
    r = tm // h
    last = t // h - 1
    prev = lambda i: jnp.maximum(i * r - 1, 0)
    nxt = lambda i: jnp.minimum((i + 1) * r, last)
    vec = pl.BlockSpec((1, bw), lambda i: (0, 0))
    return pl.pallas_call(
        functools.partial(_conv_kernel, tm=tm, ns_rows=ns_rows, dec_seq=dec_seq, seq=seq),
        grid=(t // tm,),
        in_specs=[pl.BlockSpec((h, bw), lambda i: (prev(i), col_blk)),
                  pl.BlockSpec((tm, bw), lambda i: (i, col_blk)),
                  pl.BlockSpec((h, bw), lambda i: (nxt(i), col_blk)),
                  pl.BlockSpec((h, bw), lambda i: (prev(i), col_blk + 1)),
                  pl.BlockSpec((tm, bw), lambda i: (i, col_blk + 1)),
                  pl.BlockSpec((h, bw), lambda i: (nxt(i), col_blk + 1)),
                  pl.BlockSpec((CONV_K, bw), lambda i: (0, 0)),
                  vec, vec, vec],
        out_specs=pl.BlockSpec((tm, bw), lambda i: (i, 0)),
        out_shape=jax.ShapeDtypeStruct((t, bw), BF16),
        scratch_shapes=[pltpu.VMEM((tm + 2 * h, bw), F32), pltpu.VMEM((tm, bw), F32)],
        compiler_params=_cparams("parallel"),
        name="conv_module",
    )(proj, proj, proj, proj, proj, proj, conv_w, conv_b, ln_g, ln_b)


def _ctx_attn_kernel(q_ref, k_ref, v_ref, o_ref, ko_ref, vo_ref, *, scale):
    k = k_ref[...]
    v = v_ref[...]
    s = _nt_dot(q_ref[...].astype(BF16), k.astype(BF16)) * scale
    p = jnp.exp(s - jnp.max(s, axis=-1, keepdims=True))
    l = jnp.sum(p, axis=-1, keepdims=True)
    o = jnp.dot(p.astype(BF16), v.astype(BF16), preferred_element_type=F32)
    o_ref[...] = (o / l).astype(o_ref.dtype)
    ko_ref[...] = k
    vo_ref[...] = v


def _ctx_attn_call(proj, row_base, nb, seq, q_col, hd):
    base = row_base // seq
    nh = NA_HEADS
    kv_spec = pl.BlockSpec((None, None, seq, hd), lambda b, h: (b, h, 0, 0))
    return pl.pallas_call(
        functools.partial(_ctx_attn_kernel, scale=hd ** -0.5),
        grid=(nb, nh),
        in_specs=[pl.BlockSpec((seq, hd), lambda b, h: (base + b, q_col + h)),
                  pl.BlockSpec((seq, hd), lambda b, h: (base + b, q_col + nh + h)),
                  pl.BlockSpec((seq, hd), lambda b, h: (base + b, q_col + 2 * nh + h))],
        out_specs=[pl.BlockSpec((seq, hd), lambda b, h: (b, h)), kv_spec, kv_spec],
        out_shape=[jax.ShapeDtypeStruct((nb * seq, nh * hd), BF16),
                   jax.ShapeDtypeStruct((nb, nh, seq, hd), F32),
                   jax.ShapeDtypeStruct((nb, nh, seq, hd), F32)],
        compiler_params=_cparams("parallel", "parallel"),
        name="context_attention",
    )(proj, proj, proj)


def _na_key_row0(kb, rows):
    lo = kb * NA_Q_ROWS - NA_WIN_ROWS // 2
    if isinstance(kb, (int, np.integer)):
        return int(np.clip(lo, 0, rows - NA_K_ROWS))
    return jnp.clip(lo, 0, rows - NA_K_ROWS)


def _na_bias_tables(rpb, rows):
    nh = rpb.shape[0]
    w = GRID_W
    c = np.arange(w)
    cs = np.clip(c - NA_WIN_COLS // 2, 0, w - NA_WIN_COLS)
    col_ok = (c[None, :] >= cs[:, None]) & (c[None, :] < cs[:, None] + NA_WIN_COLS)
    rel_col = np.clip(c[None, :] - c[:, None] + NA_WIN_COLS - 1, 0, 2 * NA_WIN_COLS - 2)
    planes = jnp.where(col_ok[None, None], rpb[:, :, rel_col], MASK_VALUE)
    planes = jnp.concatenate([planes, jnp.full((nh, 1, w, w), MASK_VALUE, rpb.dtype)], axis=1)
    masked_plane = 2 * NA_WIN_ROWS - 1
    variants, var_ids = [], []
    for kb in range(rows // NA_Q_ROWS):
        r = kb * NA_Q_ROWS + np.arange(NA_Q_ROWS)
        rs = np.clip(r - NA_WIN_ROWS // 2, 0, rows - NA_WIN_ROWS)
        kr = _na_key_row0(kb, rows) + np.arange(NA_K_ROWS)
        ok = (kr[None, :] >= rs[:, None]) & (kr[None, :] < rs[:, None] + NA_WIN_ROWS)
        assert ok.sum() == NA_Q_ROWS * NA_WIN_ROWS
        plane = np.where(ok, kr[None, :] - r[:, None] + NA_WIN_ROWS - 1, masked_plane)
        for vi, known in enumerate(variants):
            if np.array_equal(known, plane):
                var_ids.append(vi)
                break
        else:
            var_ids.append(len(variants))
            variants.append(plane)
    tbl = pl.pallas_call(
        functools.partial(_na_table_kernel, variants=[v.tolist() for v in variants]),
        grid=(nh,),
        in_specs=[pl.BlockSpec((None, masked_plane + 1, w, w), lambda h: (h, 0, 0, 0))],
        out_specs=pl.BlockSpec((None, len(variants), NA_Q_ROWS * w, NA_K_ROWS * w),
                               lambda h: (h, 0, 0, 0)),
        out_shape=jax.ShapeDtypeStruct((nh, len(variants), NA_Q_ROWS * w, NA_K_ROWS * w), rpb.dtype),
        compiler_params=_cparams("parallel"),
        name="na_bias_table",
    )(planes)
    return tbl, jnp.asarray(np.array(var_ids, np.int32))


def _na_table_kernel(planes_ref, o_ref, *, variants):
    w = GRID_W
    for v, plane_of in enumerate(variants):
        for qr in range(NA_Q_ROWS):
            for kr in range(0, NA_K_ROWS, 2):
                pair = jnp.concatenate([planes_ref[plane_of[qr][kr]], planes_ref[plane_of[qr][kr + 1]]], axis=1)
                o_ref[v, qr * w:(qr + 1) * w, kr * w:(kr + 2) * w] = pair


def _na_attn_kernel(var_ref, q_ref, k_ref, v_ref, kc_ref, vc_ref, bias_ref, o_ref, *, rows, scale):
    del var_ref
    nk = NA_K_ROWS * GRID_W
    start = pl.multiple_of(_na_key_row0(pl.program_id(2), rows) * GRID_W, 256)
    kw = k_ref[pl.ds(start, nk), :].astype(BF16)
    vw = v_ref[pl.ds(start, nk), :].astype(BF16)
    q = q_ref[...].astype(BF16)
    s_loc = _nt_dot(q, kw) * scale + bias_ref[...]
    s_ctx = _nt_dot(q, kc_ref[...].astype(BF16)) * scale
    m = jnp.maximum(jnp.max(s_loc, axis=-1, keepdims=True), jnp.max(s_ctx, axis=-1, keepdims=True))
    p_loc = jnp.exp(s_loc - m)
    p_ctx = jnp.exp(s_ctx - m)
    l = jnp.sum(p_loc, axis=-1, keepdims=True) + jnp.sum(p_ctx, axis=-1, keepdims=True)
    o = (jnp.dot(p_loc.astype(BF16), vw, preferred_element_type=F32)
         + jnp.dot(p_ctx.astype(BF16), vc_ref[...].astype(BF16), preferred_element_type=F32))
    o_ref[...] = (o / l).astype(o_ref.dtype)


def _na_attn_call(proj, cache_k, cache_v, layer, rpb, nb, dec_seq, q_col, hd):
    rows = dec_seq // GRID_W
    assert rows >= NA_K_ROWS and rows % NA_Q_ROWS == 0
    nh = NA_HEADS
    nq = NA_Q_ROWS * GRID_W
    nkb = rows // NA_Q_ROWS
    past = cache_k.shape[3]
    tbl, var_ids = _na_bias_tables(rpb, rows)
    ctx_spec = pl.BlockSpec((None, None, None, past, hd), lambda b, h, k, var: (b, layer, h, 0, 0))
    grid_spec = pltpu.PrefetchScalarGridSpec(
        num_scalar_prefetch=1,
        grid=(nb, nh, nkb),
        in_specs=[pl.BlockSpec((nq, hd), lambda b, h, k, var: (b * nkb + k, q_col + h)),
                  pl.BlockSpec((dec_seq, hd), lambda b, h, k, var: (b, q_col + nh + h)),
                  pl.BlockSpec((dec_seq, hd), lambda b, h, k, var: (b, q_col + 2 * nh + h)),
                  ctx_spec, ctx_spec,
                  pl.BlockSpec((None, None, nq, NA_K_ROWS * GRID_W),
                               lambda b, h, k, var: (h, var[k], 0, 0))],
        out_specs=pl.BlockSpec((nq, hd), lambda b, h, k, var: (b * nkb + k, h)),
    )
    return pl.pallas_call(
        functools.partial(_na_attn_kernel, rows=rows, scale=hd ** -0.5),
        grid_spec=grid_spec,
        out_shape=jax.ShapeDtypeStruct((nb * dec_seq, nh * hd), BF16),
        compiler_params=_cparams("parallel", "parallel", "arbitrary"),
        name="neighbourhood_attention",
    )(var_ids, proj, proj, proj, cache_k, cache_v, tbl)


def _merge_kernel(fl_ref, fc_ref, p_ref, al_ref, ac_ref, c_ref, g0_ref, g1_ref, g2_ref, g3_ref, w_ref,
                  o_ref, *, n_lat_tiles):
    is_lat = pl.program_id(0) < n_lat_tiles
    f = jnp.where(is_lat, fl_ref[...], fc_ref[...])
    a = jnp.where(is_lat, al_ref[...], ac_ref[...])
    acc = None
    for n, (br, gt) in enumerate(((f, g0_ref), (p_ref[...], g1_ref), (a, g2_ref), (c_ref[...], g3_ref))):
        y = gt[...].astype(F32) * jnp.dot(br, w_ref[n], preferred_element_type=F32)
        acc = y if acc is None else acc + y
    o_ref[...] = acc.astype(o_ref.dtype)


def _merge_call(f_lat, f_ctx, o_p, a_lat, a_ctx, o_c, gates, w_branch, tm, tn):
    t, bw = o_p.shape
    d = w_branch.shape[2]
    nj = d // tn
    nl = f_lat.shape[0] // tm
    br_spec = pl.BlockSpec((tm, bw), lambda i, j: (i, 0))
    lat_spec = pl.BlockSpec((tm, bw), lambda i, j: (jnp.minimum(i, nl - 1), 0))
    ctx_spec = pl.BlockSpec((tm, bw), lambda i, j: (jnp.maximum(i - nl, 0), 0))
    gate_specs = [pl.BlockSpec((tm, tn), functools.partial(lambda i, j, n: (i, n * nj + j), n=n))
                  for n in range(N_BRANCH)]
    return pl.pallas_call(
        functools.partial(_merge_kernel, n_lat_tiles=nl),
        grid=(t // tm, nj),
        in_specs=[lat_spec, ctx_spec, br_spec, lat_spec, ctx_spec, br_spec] + gate_specs
                 + [pl.BlockSpec((N_BRANCH, bw, tn), lambda i, j: (0, 0, j))],
        out_specs=pl.BlockSpec((tm, tn), lambda i, j: (i, j)),
        out_shape=jax.ShapeDtypeStruct((t, d), BF16),
        compiler_params=_cparams("parallel", "parallel"),
        name="branch_merge",
    )(f_lat, f_ctx, o_p, a_lat, a_ctx, o_c, gates, gates, gates, gates, w_branch)


def _out_kernel(a_ref, w_ref, x_hbm, mod_ref, g_ref, *rest, tm, gate_row, next_rows):
    if next_rows is None:
        o_ref, xbuf_ref, sem = rest
    else:
        nmod_ref, ng_ref, o_ref, h_ref, xbuf_ref, sem = rest
    k = pl.program_id(1)
    x_copy = pltpu.make_async_copy(x_hbm.at[pl.ds(pl.program_id(0) * tm, tm)], xbuf_ref, sem)

    @pl.when(k == 0)
    def _():
        x_copy.start()
        o_ref[...] = jnp.zeros_like(o_ref)

    o_ref[...] += jnp.dot(a_ref[...], w_ref[...], preferred_element_type=F32)

    @pl.when(k == pl.num_programs(1) - 1)
    def _():
        x_copy.wait()
        xn = xbuf_ref[...] + mod_ref[gate_row:gate_row + 1, :] * _rms(o_ref[...], g_ref[...])
        o_ref[...] = xn
        if next_rows is not None:
            h_ref[...] = _mod_norm(xn, ng_ref[...], nmod_ref, *next_rows).astype(h_ref.dtype)


def _k_tile(kdim, cap):
    lanes = 128
    return max(t for t in range(lanes, cap + 1, lanes) if kdim % t == 0)


def _out_call(a, w, x, mod, g, gate_row, nxt, mod_row, tm):
    t, d = x.shape
    kdim = a.shape[1]
    tk = _k_tile(kdim, 1024)
    row = pl.BlockSpec((tm, d), lambda i, k: (i, 0))
    mod_spec = pl.BlockSpec((None, 6, d), lambda i, k: (mod_row(i * tm), 0, 0))
    vec = pl.BlockSpec((1, d), lambda i, k: (0, 0))
    in_specs = [pl.BlockSpec((tm, tk), lambda i, k: (i, k)),
                pl.BlockSpec((tk, d), lambda i, k: (k, 0)),
                pl.BlockSpec(memory_space=pl.ANY), mod_spec, vec]
    args = [a, w, x, mod, g]
    out_specs, out_shape = row, jax.ShapeDtypeStruct((t, d), F32)
    if nxt is not None:
        in_specs += [mod_spec, vec]
        args += [nxt[0], nxt[1]]
        out_specs = [row, row]
        out_shape = [out_shape, jax.ShapeDtypeStruct((t, d), BF16)]
    return pl.pallas_call(
        functools.partial(_out_kernel, tm=tm, gate_row=gate_row,
                          next_rows=None if nxt is None else nxt[2]),
        grid=(t // tm, kdim // tk),
        in_specs=in_specs,
        out_specs=out_specs,
        out_shape=out_shape,
        scratch_shapes=[pltpu.VMEM((tm, d), F32), pltpu.SemaphoreType.DMA(())],
        compiler_params=_cparams("arbitrary", "arbitrary", vmem_mib=VMEM_LIMIT_WIDE_MIB),
        name="proj_norm_residual",
    )(*args)


def _ffn_up_kernel(h_ref, wg_ref, wu_ref, *rest, n_cast):
    o_ref = rest[n_cast]
    _cast_stream_step(rest[:n_cast], rest[n_cast + 1:])
    h = h_ref[...]
    y = _silu(jnp.dot(h, wg_ref[...], preferred_element_type=F32)) * jnp.dot(
        h, wu_ref[...], preferred_element_type=F32)
    o_ref[...] = y.astype(o_ref.dtype)


def _ffn_up_call(h, wg, wu, tm, tn, cast_srcs):
    t, d = h.shape
    f = wg.shape[1]
    nj = f // tn
    w_spec = pl.BlockSpec((d, tn), lambda i, j: (0, j))
    cast_in, cast_out, cast_shapes = _cast_stream_specs(cast_srcs, (t // tm) * nj, nj)
    return pl.pallas_call(
        functools.partial(_ffn_up_kernel, n_cast=len(cast_srcs)),
        grid=(t // tm, nj),
        in_specs=[pl.BlockSpec((tm, d), lambda i, j: (i, 0)), w_spec, w_spec] + cast_in,
        out_specs=[pl.BlockSpec((tm, tn), lambda i, j: (i, j))] + cast_out,
        out_shape=[jax.ShapeDtypeStruct((t, f), BF16)] + cast_shapes,
        compiler_params=_cparams("arbitrary", "arbitrary"),
        name="ffn_up",
    )(h, wg, wu, *[src[0] for src in cast_srcs])


def _split_bf16(x):
    hi = x.astype(BF16)
    return hi, (x - hi.astype(F32)).astype(BF16)


def _pack_bf16_pairs(h):
    n = h.shape[1] // 2
    hb = h.astype(BF16).astype(F32)
    hi = pltpu.bitcast(hb[:, :n], jnp.int32)
    lo = pltpu.bitcast(hb[:, n:], jnp.int32)
    return hi | lax.shift_right_logical(lo, 16)


def _unpack_bf16_pairs(w):
    hi = pltpu.bitcast(w & jnp.int32(-65536), F32).astype(BF16)
    lo = pltpu.bitcast(w << 16, F32).astype(BF16)
    return hi, lo


def _router_kernel(x_ref, mod_ref, g_ref, w_ref, b_ref, o_ref, hp_ref, *, n_experts):
    h = _mod_norm(x_ref[...], g_ref[...], mod_ref, SCALE2, SHIFT2)
    hp_ref[...] = _pack_bf16_pairs(h)
    h_hi, h_lo = _split_bf16(h)
    w_hi, w_lo = _split_bf16(w_ref[...])
    logits = (jnp.dot(h_hi, w_hi, preferred_element_type=F32)
              + jnp.dot(h_lo, w_hi, preferred_element_type=F32)
              + jnp.dot(h_hi, w_lo, preferred_element_type=F32)) + b_ref[...]
    lane = lax.broadcasted_iota(jnp.int32, logits.shape, 1).astype(F32)
    neg = -jnp.inf
    no_lane = float(logits.shape[1])
    logits = jnp.where(lane < n_experts, logits, neg)
    m1 = jnp.max(logits, axis=-1, keepdims=True)
    i1 = jnp.min(jnp.where(logits == m1, lane, no_lane), axis=-1, keepdims=True)
    rest = jnp.where(lane == i1, neg, logits)
    m2 = jnp.max(rest, axis=-1, keepdims=True)
    i2 = jnp.min(jnp.where(rest == m2, lane, no_lane), axis=-1, keepdims=True)
    e2 = jnp.exp(m2 - m1)
    den = 1.0 + e2
    o_ref[...] = (jnp.where(lane == 0.0, i1, 0.0) + jnp.where(lane == 1.0, i2, 0.0)
                  + jnp.where(lane == 2.0, 1.0 / den, 0.0) + jnp.where(lane == 3.0, e2 / den, 0.0))


def _router_call(x, mod, g, w_router, b_router, mod_row, tm):
    t, d = x.shape
    ne = w_router.shape[1]
    lanes = 128
    w = jnp.zeros((d, lanes), F32).at[:, :ne].set(w_router)
    b = jnp.zeros((1, lanes), F32).at[0, :ne].set(b_router)
    return pl.pallas_call(
        functools.partial(_router_kernel, n_experts=ne),
        grid=(t // tm,),
        in_specs=[pl.BlockSpec((tm, d), lambda i: (i, 0)),
                  pl.BlockSpec((None, 6, d), lambda i: (mod_row(i * tm), 0, 0)),
                  pl.BlockSpec((1, d), lambda i: (0, 0)),
                  pl.BlockSpec((d, lanes), lambda i: (0, 0)),
                  pl.BlockSpec((1, lanes), lambda i: (0, 0))],
        out_specs=[pl.BlockSpec((tm, lanes), lambda i: (i, 0)),
                   pl.BlockSpec((tm, d // 2), lambda i: (i, 0))],
        out_shape=[jax.ShapeDtypeStruct((t, lanes), F32),
                   jax.ShapeDtypeStruct((t, d // 2), jnp.int32)],
        compiler_params=_cparams("parallel"),
        name="router",
    )(x, mod, g, w, b)


def _routing_tables(route, n_experts, tm):
    t = route.shape[0]
    na = TOP_K * t
    p_rows = na + n_experts * tm
    nt = p_rows // tm
    e_flat = route[:, :TOP_K].astype(jnp.int32).reshape(na)
    onehot = (e_flat[:, None] == jnp.arange(n_experts, dtype=jnp.int32)[None, :]).astype(jnp.int32)
    csum = jnp.cumsum(onehot, axis=0)
    rank = jnp.sum((csum - onehot) * onehot, axis=1)
    counts = csum[-1]
    padded = ((counts + tm - 1) // tm) * tm
    ends = jnp.cumsum(padded)
    pos = (ends - padded)[e_flat] + rank
    src = jnp.zeros((p_rows,), jnp.int32).at[pos].set(
        jnp.arange(na, dtype=jnp.int32) // TOP_K, unique_indices=True, mode="promise_in_bounds")
    tile_expert = jnp.minimum(
        jnp.searchsorted(ends, jnp.arange(nt, dtype=jnp.int32) * tm, side="right"), n_experts - 1)
    meta = jnp.concatenate([tile_expert.astype(jnp.int32), (ends[-1:] // tm).astype(jnp.int32)])
    return src, pos.astype(jnp.int32), meta


def _gather_rows_kernel(idx_ref, src_ref, o_ref, buf_ref, sem_ref, *, rows):
    i = pl.program_id(0)
    n = pl.num_programs(0)

    def issue(tile, slot):
        def body(r, carry):
            pltpu.make_async_copy(src_ref.at[pl.ds(idx_ref[tile * rows + r], 1)],
                                  buf_ref.at[slot, pl.ds(r, 1)], sem_ref.at[slot]).start()
            return carry
        lax.fori_loop(0, rows, body, 0, unroll=DMA_ISSUE_UNROLL)

    @pl.when(i == 0)
    def _():
        issue(0, 0)

    @pl.when(i + 1 < n)
    def _():
        issue(i + 1, (i + 1) % 2)

    slot = i % 2
    pltpu.make_async_copy(src_ref.at[pl.ds(0, rows)], buf_ref.at[slot], sem_ref.at[slot]).wait()
    half = buf_ref.shape[2]
    hi, lo = _unpack_bf16_pairs(buf_ref[slot])
    o_ref[:, :half] = hi
    o_ref[:, half:] = lo


def _gather_rows_call(src, idx, rows):
    p_rows = idx.shape[0]
    half = src.shape[1]
    grid_spec = pltpu.PrefetchScalarGridSpec(
        num_scalar_prefetch=1,
        grid=(p_rows // rows,),
        in_specs=[pl.BlockSpec(memory_space=pl.ANY)],
        out_specs=pl.BlockSpec((rows, 2 * half), lambda i, idx: (i, 0)),
        scratch_shapes=[pltpu.VMEM((2, rows, half), src.dtype), pltpu.SemaphoreType.DMA((2,))],
    )
    return pl.pallas_call(
        functools.partial(_gather_rows_kernel, rows=rows),
        grid_spec=grid_spec,
        out_shape=jax.ShapeDtypeStruct((p_rows, 2 * half), BF16),
        compiler_params=_cparams("arbitrary"),
        name="moe_gather",
    )(idx, src)


def _moe_up_kernel(meta_ref, xs_ref, wg_ref, wu_ref, o_ref, wgb_ref, wub_ref, *, n_tiles):
    i = pl.program_id(1)
    used = i < meta_ref[n_tiles]
    fresh = jnp.logical_or(i == 0, meta_ref[i] != meta_ref[jnp.maximum(i - 1, 0)])

    @pl.when(jnp.logical_and(used, fresh))
    def _():
        wgb_ref[...] = wg_ref[...].astype(BF16)
        wub_ref[...] = wu_ref[...].astype(BF16)

    @pl.when(used)
    def _():
        h = xs_ref[...]
        y = _silu(jnp.dot(h, wgb_ref[...], preferred_element_type=F32)) * jnp.dot(
            h, wub_ref[...], preferred_element_type=F32)
        o_ref[...] = y.astype(o_ref.dtype)

    @pl.when(jnp.logical_not(used))
    def _():
        o_ref[...] = jnp.zeros_like(o_ref)


def _moe_up_call(xs, meta, wg, wu, tm, tn):
    p_rows, d = xs.shape
    f = wg.shape[2]
    nt = p_rows // tm
    w_spec = pl.BlockSpec((None, d, tn), lambda j, i, meta: (meta[i], 0, j))
    grid_spec = pltpu.PrefetchScalarGridSpec(
        num_scalar_prefetch=1,
        grid=(f // tn, nt),
        in_specs=[pl.BlockSpec((tm, d), lambda j, i, meta: (i, 0)), w_spec, w_spec],
        out_specs=pl.BlockSpec((None, tm, tn), lambda j, i, meta: (j, i, 0)),
        scratch_shapes=[pltpu.VMEM((d, tn), BF16), pltpu.VMEM((d, tn), BF16)],
    )
    return pl.pallas_call(
        functools.partial(_moe_up_kernel, n_tiles=nt),
        grid_spec=grid_spec,
        out_shape=jax.ShapeDtypeStruct((f // tn, p_rows, tn), BF16),
        compiler_params=_cparams("arbitrary", "arbitrary", vmem_mib=VMEM_LIMIT_WIDE_MIB),
        name="moe_up",
    )(meta, xs, wg, wu)


def _moe_down_kernel(meta_ref, a_ref, w_ref, o_ref, *, n_tiles):
    i = pl.program_id(0)

    @pl.when(pl.program_id(1) == 0)
    def _():
        o_ref[...] = jnp.zeros_like(o_ref)

    @pl.when(i < meta_ref[n_tiles])
    def _():
        nsub, _, tk = a_ref.shape
        acc = o_ref[...]
        for s in range(nsub):
            acc = acc + jnp.dot(a_ref[s], w_ref[s * tk:(s + 1) * tk, :], preferred_element_type=F32)
        o_ref[...] = acc


def _moe_down_call(a, meta, wd, tm, nsub):
    nk, p_rows, tk = a.shape
    d = wd.shape[2]
    nt = p_rows // tm
    grid_spec = pltpu.PrefetchScalarGridSpec(
        num_scalar_prefetch=1,
        grid=(nt, nk // nsub),
        in_specs=[pl.BlockSpec((nsub, tm, tk), lambda i, k, meta: (k, i, 0)),
                  pl.BlockSpec((None, nsub * tk, d), lambda i, k, meta: (meta[i], k, 0))],
        out_specs=pl.BlockSpec((tm, d), lambda i, k, meta: (i, 0)),
    )
    return pl.pallas_call(
        functools.partial(_moe_down_kernel, n_tiles=nt),
        grid_spec=grid_spec,
        out_shape=jax.ShapeDtypeStruct((p_rows, d), F32),
        compiler_params=_cparams("parallel", "arbitrary"),
        name="moe_down",
    )(meta, a, wd)


def _moe_combine_kernel(pos_ref, ys_ref, route_ref, x_ref, mod_ref, g_ref, *rest, rows, n_lat_tiles):
    if n_lat_tiles is None:
        o_ref, buf_ref, sem_ref = rest
    else:
        lat_ref, ctx_ref, buf_ref, sem_ref = rest
    i = pl.program_id(0)
    n = pl.num_programs(0)

    def issue(tile, slot):
        def body(r, carry):
            for s in range(TOP_K):
                pltpu.make_async_copy(ys_ref.at[pl.ds(pos_ref[TOP_K * (tile * rows + r) + s], 1)],
                                      buf_ref.at[slot, s, pl.ds(r, 1)], sem_ref.at[slot]).start()
            return carry
        lax.fori_loop(0, rows, body, 0, unroll=DMA_ISSUE_UNROLL)

    @pl.when(i == 0)
    def _():
        issue(0, 0)

    @pl.when(i + 1 < n)
    def _():
        issue(i + 1, (i + 1) % 2)

    slot = i % 2
    for s in range(TOP_K):
        pltpu.make_async_copy(ys_ref.at[pl.ds(0, rows)], buf_ref.at[slot, s], sem_ref.at[slot]).wait()
    route = route_ref[...]
    y = None
    for s in range(TOP_K):
        term = route[:, TOP_K + s:TOP_K + s + 1] * buf_ref[slot, s]
        y = term if y is None else y + term
    res = x_ref[...] + mod_ref[GATE2:GATE2 + 1, :] * _rms(y, g_ref[...])
    if n_lat_tiles is None:
        o_ref[...] = res
    else:
        @pl.when(i < n_lat_tiles)
        def _():
            lat_ref[...] = res

        @pl.when(i >= n_lat_tiles)
        def _():
            ctx_ref[...] = res


def _moe_combine_call(ys, pos, route, x, mod, g, mod_row, rows, split_rows):
    t, d = x.shape
    row = pl.BlockSpec((rows, d), lambda i, pos: (i, 0))
    if split_rows is None:
        nl = None
        out_specs, out_shape = row, jax.ShapeDtypeStruct((t, d), F32)
    else:
        nl = split_rows // rows
        out_specs = [pl.BlockSpec((rows, d), lambda i, pos: (jnp.minimum(i, nl - 1), 0)),
                     pl.BlockSpec((rows, d), lambda i, pos: (jnp.maximum(i - nl, 0), 0))]
        out_shape = [jax.ShapeDtypeStruct((split_rows, d), F32),
                     jax.ShapeDtypeStruct((t - split_rows, d), F32)]
    grid_spec = pltpu.PrefetchScalarGridSpec(
        num_scalar_prefetch=1,
        grid=(t // rows,),
        in_specs=[pl.BlockSpec(memory_space=pl.ANY),
                  pl.BlockSpec((rows, route.shape[1]), lambda i, pos: (i, 0)),
                  row,
                  pl.BlockSpec((None, 6, d), lambda i, pos: (mod_row(i * rows), 0, 0)),
                  pl.BlockSpec((1, d), lambda i, pos: (0, 0))],
        out_specs=out_specs,
        scratch_shapes=[pltpu.VMEM((2, TOP_K, rows, d), F32), pltpu.SemaphoreType.DMA((2,))],
    )
    return pl.pallas_call(
        functools.partial(_moe_combine_kernel, rows=rows, n_lat_tiles=nl),
        grid_spec=grid_spec,
        out_shape=out_shape,
        compiler_params=_cparams("arbitrary"),
        name="moe_combine",
    )(pos, ys, route, x, mod, g)


def kernel(x_prompt, x_sample, cache_k, cache_v, c, c_ctx, w_ada, b_ada, norm_g, w_in, b_in, pool_w, pool_scale, rpb, conv_w, conv_b, conv_ln_g, conv_ln_b, w_branch, w_out, w_gate_d, w_up_d, w_down_d, w_router, b_router, w_gate_e, w_up_e, w_down_e):
    nbp, seq, d = x_prompt.shape
    nbs, dec_seq, _ = x_sample.shape
    depth = w_ada.shape[0]
    bw = d // N_BRANCH
    hd = bw // NA_HEADS
    ns_rows = nbs * dec_seq
    np_rows = nbp * seq
    assert dec_seq & (dec_seq - 1) == 0 and seq & (seq - 1) == 0
    assert nbs + 1 <= MOD_ROWS

    t_rows = ns_rows + np_rows
    tm_wide = max(tm for tm in (2048, 1024, 512) if dec_seq % tm == 0 and np_rows % tm == 0)
    tm_mid = min(tm_wide, 1024)
    tm_moe = 512
    tile_seq = min(256, seq)
    tn = 512
    assert t_rows % tm_wide == 0 and seq % tile_seq == 0

    def mod_row(row0):
        return jnp.where(row0 < ns_rows, row0 // dec_seq, nbs)

    cvec =jnp.zeros((MOD_ROWS, d), F32).at[:nbs].set(c).at[nbs].set(c_ctx)
    mods = _ada_call(cvec, w_ada, b_ada).reshape(depth, MOD_ROWS, 6, d)
    cc, sc = _channel_dft_mats(bw, FNET_GROUPS)

    pool_col, conv_col, main_cols = 1, 5, 7 * bw
    q_col = 2 * bw // hd

    def gain(l, n):
        return norm_g[l, n][None, :]

    x, h = _entry_call(x_sample.reshape(ns_rows, d), x_prompt.reshape(np_rows, d), mods[0], gain(0, 0),
                       mod_row, 512)
    w_in_cur = w_in[0].astype(BF16)[None]
    new_k, new_v = [], []
    y_split = None
    for l in range(depth):
        mod = mods[l]
        last = l + 1 == depth
        i = l // 2
        if l % 2 == 1:
            mixer_w = [w_down_e[i].reshape(-1, d)]
        else:
            mixer_w = [w_gate_d[i], w_up_d[i], w_down_d[i]]
        mixer_w = [(a, 0, a.shape[0]) for a in mixer_w]
        proj, gates, *mixer_w = _in_call(h, w_in_cur, b_in[l][None, :], 0, main_cols, tm_wide, tn, mixer_w)

        f_lat = _fourier_call(proj, 0, nbs, dec_seq, bw, cc, sc)
        f_ctx = _fourier_call(proj, ns_rows, nbp, seq, bw, cc, sc)
        o_p = _pool_call(proj, pool_w[l].astype(BF16), pool_scale[l][None, :], pool_col, bw,
                         tile_seq, ns_rows, dec_seq, seq)
        o_c = _conv_call(proj, conv_w[l], conv_b[l][None, :], conv_ln_g[l][None, :],
                         conv_ln_b[l][None, :], conv_col, bw, tile_seq, ns_rows, dec_seq, seq)
        a_lat = _na_attn_call(proj, cache_k, cache_v, l, rpb[l], nbs, dec_seq, q_col, hd)
        a_ctx, k_ctx, v_ctx = _ctx_attn_call(proj, ns_rows, nbp, seq, q_col, hd)
        new_k.append(k_ctx)
        new_v.append(v_ctx)

        merged = _merge_call(f_lat, f_ctx, o_p, a_lat, a_ctx, o_c, gates, w_branch[l].astype(BF16),
                             tm_mid, tn)
        nxt = None if last else (mods[l + 1], gain(l + 1, 0), (SCALE1, SHIFT1))
        if l % 2 == 1:
            x = _out_call(merged, w_out[l].astype(BF16), x, mod, gain(l, 1), GATE1, None,
                          mod_row, tm_mid)
            ne = w_router.shape[2]
            route, h_packed = _router_call(x, mod, gain(l, 2), w_router[i], b_router[i], mod_row, 512)
            src, pos, meta = _routing_tables(route, ne, tm_moe)
            xs = _gather_rows_call(h_packed, src, tm_moe)
            hmid = _moe_up_call(xs, meta, w_gate_e[i], w_up_e[i], tm_moe, 2 * tn)
            ys = _moe_down_call(hmid, meta, mixer_w[0].reshape(w_down_e.shape[1:]), tm_moe, 1)
            out = _moe_combine_call(ys, pos, route, x, mod, gain(l, 3), mod_row, 256,
                                    ns_rows if last else None)
            if last:
                y_split = out
            else:
                x = out
                h = _norm_call(x, nxt[0], nxt[1], *nxt[2], mod_row, tm_mid)
                w_in_cur = w_in[l + 1].astype(BF16)[None]
        else:
            wg, wu, wd = mixer_w
            x, h2 = _out_call(merged, w_out[l].astype(BF16), x, mod, gain(l, 1), GATE1,
                              (mod, gain(l, 2), (SCALE2, SHIFT2)), mod_row, tm_mid)
            if last:
                (hmid,) = _ffn_up_call(h2, wg, wu, tm_wide, tn, [])
                x = _out_call(hmid, wd, x, mod, gain(l, 3), GATE2, None, mod_row, tm_mid)
            else:
                hmid, w_in_next = _ffn_up_call(h2, wg, wu, tm_wide, tn,
                                               [(w_in.reshape(depth * d, -1), (l + 1) * d, d)])
                w_in_cur = w_in_next[None]
                x, h = _out_call(hmid, wd, x, mod, gain(l, 3), GATE2, nxt, mod_row, tm_mid)

    if y_split is None:
        y_split = x[:ns_rows], x[ns_rows:]
    y_sample = y_split[0].reshape(nbs, dec_seq, d)
    y_prompt = y_split[1].reshape(nbp, seq, d)
    return y_prompt, y_sample, jnp.stack(new_k, axis=1), jnp.stack(new_v, axis=1)
```

```python
import functools

import numpy as np
import jax
import jax.numpy as jnp
from jax import lax
from jax.experimental import pallas as pl
from jax.experimental.pallas import tpu as pltpu

F32 = jnp.float32
BF16 = jnp.bfloat16

N_BRANCH = 4
FNET_GROUPS = 4
POOL_WINDOWS = (2, 4, 8, 16)
NA_HEADS = 4
NA_WIN_ROWS = 8
NA_WIN_COLS = 16
GRID_W = 64
CONV_K = 31
TOP_K = 2
RMS_EPS = 1e-6
LN_EPS = 1e-5
MASK_VALUE = -1e30
MOD_ROWS = 8
POOL_HALO = 8
CONV_HALO = 16
NA_Q_ROWS = 8
NA_K_ROWS = 16
DMA_ISSUE_UNROLL = 8
VMEM_LIMIT_MIB = 48
VMEM_LIMIT_WIDE_MIB = 56

SHIFT1, SCALE1, GATE1, SHIFT2, SCALE2, GATE2 = range(6)


def _cparams(*sem, vmem_mib=VMEM_LIMIT_MIB):
    return pltpu.CompilerParams(dimension_semantics=sem, vmem_limit_bytes=vmem_mib * 1024 * 1024)


def _sigmoid(x):
    return 0.5 * jnp.tanh(0.5 * x) + 0.5


def _silu(x):
    return x * _sigmoid(x)


def _rms(x, g):
    return x * lax.rsqrt(jnp.mean(x * x, axis=-1, keepdims=True) + RMS_EPS) * g


def _mod_norm(x, g, mod_ref, scale_row, shift_row):
    return (_rms(x, g) * (1.0 + mod_ref[scale_row:scale_row + 1, :])
            + mod_ref[shift_row:shift_row + 1, :])


def _nt_dot(a, b):
    return lax.dot_general(a, b, (((1,), (1,)), ((), ())), preferred_element_type=F32)


def _ada_kernel(c_ref, w_ref, b_ref, o_ref):
    s = _silu(c_ref[...]).astype(BF16)
    o_ref[...] = jnp.dot(s, w_ref[...].astype(BF16), preferred_element_type=F32) + b_ref[...]


def _ada_call(cvec, w_ada, b_ada):
    depth, d, n = w_ada.shape
    tn = 512
    return pl.pallas_call(
        _ada_kernel,
        grid=(depth, n // tn),
        in_specs=[pl.BlockSpec((MOD_ROWS, d), lambda l, j: (0, 0)),
                  pl.BlockSpec((None, d, tn), lambda l, j: (l, 0, j)),
                  pl.BlockSpec((None, 1, tn), lambda l, j: (l, 0, j))],
        out_specs=pl.BlockSpec((None, MOD_ROWS, tn), lambda l, j: (l, 0, j)),
        out_shape=jax.ShapeDtypeStruct((depth, MOD_ROWS, n), F32),
        compiler_params=_cparams("parallel", "parallel"),
        name="adaln",
    )(cvec, w_ada, b_ada.reshape(depth, 1, n))


def _entry_kernel(lat_ref, ctx_ref, mod_ref, g_ref, x_ref, h_ref, *, n_lat_tiles):
    x = jnp.where(pl.program_id(0) < n_lat_tiles, lat_ref[...], ctx_ref[...])
    x_ref[...] = x
    h_ref[...] = _mod_norm(x, g_ref[...], mod_ref, SCALE1, SHIFT1).astype(h_ref.dtype)


def _entry_call(x_lat, x_ctx, mod, g, mod_row, tm):
    d = x_lat.shape[1]
    nl = x_lat.shape[0] // tm
    t = x_lat.shape[0] + x_ctx.shape[0]
    row = pl.BlockSpec((tm, d), lambda i: (i, 0))
    return pl.pallas_call(
        functools.partial(_entry_kernel, n_lat_tiles=nl),
        grid=(t // tm,),
        in_specs=[pl.BlockSpec((tm, d), lambda i: (jnp.minimum(i, nl - 1), 0)),
                  pl.BlockSpec((tm, d), lambda i: (jnp.maximum(i - nl, 0), 0)),
                  pl.BlockSpec((None, 6, d), lambda i: (mod_row(i * tm), 0, 0)),
                  pl.BlockSpec((1, d), lambda i: (0, 0))],
        out_specs=[row, row],
        out_shape=[jax.ShapeDtypeStruct((t, d), F32), jax.ShapeDtypeStruct((t, d), BF16)],
        compiler_params=_cparams("parallel"),
        name="entry_norm",
    )(x_lat, x_ctx, mod, g)


def _norm_kernel(x_ref, mod_ref, g_ref, o_ref, *, scale_row, shift_row):
    o_ref[...] = _mod_norm(x_ref[...], g_ref[...], mod_ref, scale_row, shift_row).astype(o_ref.dtype)


def _norm_call(x, mod, g, scale_row, shift_row, mod_row, tm):
    t, d = x.shape
    return pl.pallas_call(
        functools.partial(_norm_kernel, scale_row=scale_row, shift_row=shift_row),
        grid=(t // tm,),
        in_specs=[pl.BlockSpec((tm, d), lambda i: (i, 0)),
                  pl.BlockSpec((None, 6, d), lambda i: (mod_row(i * tm), 0, 0)),
                  pl.BlockSpec((1, d), lambda i: (0, 0))],
        out_specs=pl.BlockSpec((tm, d), lambda i: (i, 0)),
        out_shape=jax.ShapeDtypeStruct((t, d), BF16),
        compiler_params=_cparams("parallel"),
        name="mod_norm",
    )(x, mod, g)


def _cast_stream_specs(srcs, steps, nj):
    bf16_rows = 16
    in_specs, out_specs, shapes = [], [], []
    for a, row0, rows in srcs:
        cols = a.shape[1]
        chunk = min(c for c in range(bf16_rows, rows + 1, bf16_rows)
                    if rows % c == 0 and row0 % c == 0 and rows // c <= steps)
        walk = functools.partial(lambda i, j, first, last: (first + jnp.minimum(i * nj + j, last), 0),
                                 last=rows // chunk - 1)
        in_specs.append(pl.BlockSpec((chunk, cols), functools.partial(walk, first=row0 // chunk)))
        out_specs.append(pl.BlockSpec((chunk, cols), functools.partial(walk, first=0)))
        shapes.append(jax.ShapeDtypeStruct((rows, cols), BF16))
    return in_specs, out_specs, shapes


def _cast_stream_step(src_refs, dst_refs):
    for s, d in zip(src_refs, dst_refs):
        d[...] = s[...].astype(d.dtype)


def _in_kernel(h_ref, w_ref, b_ref, *rest, n_main, n_cast):
    main_ref, gate_ref = rest[n_cast:n_cast + 2]
    _cast_stream_step(rest[:n_cast], rest[n_cast + 2:])
    j = pl.program_id(1)
    y = jnp.dot(h_ref[...], w_ref[...], preferred_element_type=F32) + b_ref[...]

    @pl.when(j < n_main)
    def _():
        main_ref[...] = y

    @pl.when(j >= n_main)
    def _():
        gate_ref[...] = _sigmoid(y).astype(gate_ref.dtype)


def _in_call(h, w, b, layer, main_cols, tm, tn, cast_srcs):
    t, d = h.shape
    n = w.shape[2]
    n_main = main_cols // tn
    nj = n // tn
    steps = (t // tm) * nj
    in_specs = [pl.BlockSpec((tm, d), lambda i, j: (i, 0)),
                pl.BlockSpec((None, d, tn), lambda i, j: (layer, 0, j)),
                pl.BlockSpec((1, tn), lambda i, j: (0, j))]
    out_specs = [pl.BlockSpec((tm, tn), lambda i, j: (i, jnp.minimum(j, n_main - 1))),
                 pl.BlockSpec((tm, tn), lambda i, j: (i, jnp.maximum(j - n_main, 0)))]
    out_shape = [jax.ShapeDtypeStruct((t, main_cols), F32),
                 jax.ShapeDtypeStruct((t, n - main_cols), BF16)]
    cast_in, cast_out, cast_shapes = _cast_stream_specs(cast_srcs, steps, nj)
    return pl.pallas_call(
        functools.partial(_in_kernel, n_main=n_main, n_cast=len(cast_srcs)),
        grid=(t // tm, nj),
        in_specs=in_specs + cast_in,
        out_specs=out_specs + cast_out,
        out_shape=out_shape + cast_shapes,
        compiler_params=_cparams("arbitrary", "arbitrary"),
        name="in_proj",
    )(h, w, b, *[src[0] for src in cast_srcs])


def _dft_mats(n):
    scale = 1.0 / np.sqrt(n)
    j = jnp.arange(n, dtype=jnp.int32)
    if n <= 1024:
        ang = ((j[:, None] * j[None, :]) % n).astype(F32) * (2.0 * np.pi / n)
        return (jnp.cos(ang) * scale).astype(BF16), (jnp.sin(ang) * scale).astype(BF16)
    base = 64
    hi = n // base
    k1 = jnp.arange(hi, dtype=jnp.int32)
    k0 = jnp.arange(base, dtype=jnp.int32)
    a = ((j[:, None] * k1[None, :]) % hi).astype(F32) * (2.0 * np.pi / hi)
    b = ((j[:, None] * k0[None, :]) % n).astype(F32) * (2.0 * np.pi / n)
    ca, sa = jnp.cos(a)[:, :, None], jnp.sin(a)[:, :, None]
    cb, sb = jnp.cos(b)[:, None, :] * scale, jnp.sin(b)[:, None, :] * scale
    c = (ca * cb - sa * sb).reshape(n, n)
    s = (sa * cb + ca * sb).reshape(n, n)
    return c.astype(BF16), s.astype(BF16)


def _channel_dft_mats(width, groups):
    gw = width // groups
    k = np.arange(gw)
    ang = 2.0 * np.pi * ((k[:, None] * k[None, :]) % gw) / gw
    c = np.zeros((width, width), np.float32)
    s = np.zeros((width, width), np.float32)
    for g in range(groups):
        sl = slice(g * gw, (g + 1) * gw)
        c[sl, sl] = np.cos(ang) / np.sqrt(gw)
        s[sl, sl] = np.sin(ang) / np.sqrt(gw)
    return jnp.asarray(c, BF16), jnp.asarray(s, BF16)


def _fnet1_kernel(u_ref, cc_ref, sc_ref, vc_ref, vs_ref):
    u = u_ref[...].astype(BF16)
    vc_ref[...] = jnp.dot(u, cc_ref[...], preferred_element_type=F32).astype(BF16)
    vs_ref[...] = jnp.dot(u, sc_ref[...], preferred_element_type=F32).astype(BF16)


def _fnet2_kernel(c_ref, s_ref, vc_ref, vs_ref, o_ref, acc_ref):
    k = pl.program_id(2)

    @pl.when(k == 0)
    def _():
        acc_ref[...] = jnp.zeros_like(acc_ref)

    acc_ref[...] += (jnp.dot(c_ref[...], vc_ref[...], preferred_element_type=F32)
                     - jnp.dot(s_ref[...], vs_ref[...], preferred_element_type=F32))

    @pl.when(k == pl.num_programs(2) - 1)
    def _():
        o_ref[...] = acc_ref[...].astype(o_ref.dtype)


def _fourier_call(proj, row_base, nb, seq, bw, cc, sc):
    c_l, s_l = _dft_mats(seq)
    tm1 = min(seq, 512)
    nt1 = seq // tm1
    base1 = row_base // tm1
    vc, vs = pl.pallas_call(
        _fnet1_kernel,
        grid=(nb, nt1),
        in_specs=[pl.BlockSpec((tm1, bw), lambda b, t: (base1 + b * nt1 + t, 0)),
                  pl.BlockSpec((bw, bw), lambda b, t: (0, 0)),
                  pl.BlockSpec((bw, bw), lambda b, t: (0, 0))],
        out_specs=[pl.BlockSpec((tm1, bw), lambda b, t: (t, b)),
                   pl.BlockSpec((tm1, bw), lambda b, t: (t, b))],
        out_shape=[jax.ShapeDtypeStruct((seq, nb * bw), BF16)] * 2,
        compiler_params=_cparams("parallel", "parallel"),
        name="fnet_channels",
    )(proj, cc, sc)
    tm2 = min(seq, 1024)
    tk = min(seq, 2048)
    nt2 = seq // tm2
    return pl.pallas_call(
        _fnet2_kernel,
        grid=(nt2, nb, seq // tk),
        in_specs=[pl.BlockSpec((tm2, tk), lambda i, j, k: (i, k)),
                  pl.BlockSpec((tm2, tk), lambda i, j, k: (i, k)),
                  pl.BlockSpec((tk, bw), lambda i, j, k: (k, j)),
                  pl.BlockSpec((tk, bw), lambda i, j, k: (k, j))],
        out_specs=pl.BlockSpec((tm2, bw), lambda i, j, k: (j * nt2 + i, 0)),
        out_shape=jax.ShapeDtypeStruct((nb * seq, bw), BF16),
        scratch_shapes=[pltpu.VMEM((tm2, bw), F32)],
        compiler_params=_cparams("parallel", "parallel", "arbitrary"),
        name="fnet_positions",
    )(c_l, s_l, vc, vs)


def _seq_position(row0, ns_rows, dec_seq, seq):
    is_lat = row0 < ns_rows
    seq_len = jnp.where(is_lat, dec_seq, seq)
    pos0 = jnp.where(is_lat, row0 & (dec_seq - 1), row0 & (seq - 1))
    return seq_len, pos0


def _pool_kernel(prev_ref, cur_ref, nxt_ref, w_ref, sc_ref, o_ref, ext_ref, *,
                 tm, ns_rows, dec_seq, seq):
    seq_len, pos0 = _seq_position(pl.program_id(0) * tm, ns_rows, dec_seq, seq)
    h = POOL_HALO
    ext_ref[0:h, :] = jnp.where(pos0 == 0, 0.0, prev_ref[...])
    ext_ref[h:h + tm, :] = cur_ref[...]
    ext_ref[h + tm:2 * h + tm, :] = jnp.where(pos0 + tm == seq_len, 0.0, nxt_ref[...])
    t = pos0 + lax.broadcasted_iota(jnp.int32, (tm, 1), 0)
    gw = cur_ref.shape[1] // len(POOL_WINDOWS)
    for gi, win in enumerate(POOL_WINDOWS):
        lo = win // 2
        hi = win - lo
        cols = slice(gi * gw, (gi + 1) * gw)
        s = ext_ref[h - lo:h - lo + tm, cols]
        for j in range(1 - lo, hi):
            s = s + ext_ref[h + j:h + j + tm, cols]
        cnt = (jnp.minimum(t + hi, seq_len) - jnp.maximum(t - lo, 0)).astype(F32)
        dlt = s / cnt - cur_ref[:, cols]
        y = jnp.dot(dlt.astype(BF16), w_ref[gi], preferred_element_type=F32) * sc_ref[:, cols]
        o_ref[:, cols] = y.astype(o_ref.dtype)


def _pool_call(proj, pool_w, pool_scale, col_blk, bw, tm, ns_rows, dec_seq, seq):
    t = proj.shape[0]
    h = POOL_HALO
    r = tm // h
    last = t // h - 1
    gw = bw // len(POOL_WINDOWS)
    return pl.pallas_call(
        functools.partial(_pool_kernel, tm=tm, ns_rows=ns_rows, dec_seq=dec_seq, seq=seq),
        grid=(t // tm,),
        in_specs=[pl.BlockSpec((h, bw), lambda i: (jnp.maximum(i * r - 1, 0), col_blk)),
                  pl.BlockSpec((tm, bw), lambda i: (i, col_blk)),
                  pl.BlockSpec((h, bw), lambda i: (jnp.minimum((i + 1) * r, last), col_blk)),
                  pl.BlockSpec((len(POOL_WINDOWS), gw, gw), lambda i: (0, 0, 0)),
                  pl.BlockSpec((1, bw), lambda i: (0, 0))],
        out_specs=pl.BlockSpec((tm, bw), lambda i: (i, 0)),
        out_shape=jax.ShapeDtypeStruct((t, bw), BF16),
        scratch_shapes=[pltpu.VMEM((tm + 2 * h, bw), F32)],
        compiler_params=_cparams("parallel"),
        name="pool_mix",
    )(proj, proj, proj, pool_w, pool_scale)


def _conv_kernel(ap_ref, ac_ref, an_ref, gp_ref, gc_ref, gn_ref, w_ref, b_ref, lg_ref, lb_ref,
                 o_ref, ext_ref, y_ref, *, tm, ns_rows, dec_seq, seq):
    seq_len, pos0 = _seq_position(pl.program_id(0) * tm, ns_rows, dec_seq, seq)
    h = CONV_HALO
    ext_ref[0:h, :] = jnp.where(pos0 == 0, 0.0, ap_ref[...] * _sigmoid(gp_ref[...]))
    ext_ref[h:h + tm, :] = ac_ref[...] * _sigmoid(gc_ref[...])
    ext_ref[h + tm:2 * h + tm, :] = jnp.where(pos0 + tm == seq_len, 0.0,
                                              an_ref[...] * _sigmoid(gn_ref[...]))
    bw = ac_ref.shape[1]
    lanes = 128
    for c in range(bw // lanes):
        cols = slice(c * lanes, (c + 1) * lanes)
        acc = jnp.zeros((tm, lanes), F32)
        for k in range(CONV_K):
            off = h + k - CONV_K // 2
            acc = acc + ext_ref[off:off + tm, cols] * w_ref[k:k + 1, cols]
        y_ref[:, cols] = acc + b_ref[:, cols]
    y = y_ref[...]
    mu = jnp.mean(y, axis=-1, keepdims=True)
    var = jnp.mean(jnp.square(y - mu), axis=-1, keepdims=True)
    z = (y - mu) * lax.rsqrt(var + LN_EPS) * lg_ref[...] + lb_ref[...]
    o_ref[...] = _silu(z).astype(o_ref.dtype)


def _conv_call(proj, conv_w, conv_b, ln_g, ln_b, col_blk, bw, tm, ns_rows, dec_seq, seq):
    t = proj.shape[0]
    h = CONV_HALO
    r = tm // h
    last = t // h - 1
    prev = lambda i: jnp.maximum(i * r - 1, 0)
    nxt = lambda i: jnp.minimum((i + 1) * r, last)
    vec = pl.BlockSpec((1, bw), lambda i: (0, 0))
    return pl.pallas_call(
        functools.partial(_conv_kernel, tm=tm, ns_rows=ns_rows, dec_seq=dec_seq, seq=seq),
        grid=(t // tm,),
        in_specs=[pl.BlockSpec((h, bw), lambda i: (prev(i), col_blk)),
                  pl.BlockSpec((tm, bw), lambda i: (i, col_blk)),
                  pl.BlockSpec((h, bw), lambda i: (nxt(i), col_blk)),
                  pl.BlockSpec((h, bw), lambda i: (prev(i), col_blk + 1)),
                  pl.BlockSpec((tm, bw), lambda i: (i, col_blk + 1)),
                  pl.BlockSpec((h, bw), lambda i: (nxt(i), col_blk + 1)),
                  pl.BlockSpec((CONV_K, bw), lambda i: (0, 0)),
                  vec, vec, vec],
        out_specs=pl.BlockSpec((tm, bw), lambda i: (i, 0)),
        out_shape=jax.ShapeDtypeStruct((t, bw), BF16),
        scratch_shapes=[pltpu.VMEM((tm + 2 * h, bw), F32), pltpu.VMEM((tm, bw), F32)],
        compiler_params=_cparams("parallel"),
        name="conv_module",
    )(proj, proj, proj, proj, proj, proj, conv_w, conv_b, ln_g, ln_b)


def _ctx_attn_kernel(q_ref, k_ref, v_ref, o_ref, ko_ref, vo_ref, *, scale):
    k = k_ref[...]
    v = v_ref[...]
    s = _nt_dot(q_ref[...].astype(BF16), k.astype(BF16)) * scale
    p = jnp.exp(s - jnp.max(s, axis=-1, keepdims=True))
    l = jnp.sum(p, axis=-1, keepdims=True)
    o = jnp.dot(p.astype(BF16), v.astype(BF16), preferred_element_type=F32)
    o_ref[...] = (o / l).astype(o_ref.dtype)
    ko_ref[...] = k
    vo_ref[...] = v


def _ctx_attn_call(proj, row_base, nb, seq, q_col, hd):
    base = row_base // seq
    nh = NA_HEADS
    kv_spec = pl.BlockSpec((None, None, seq, hd), lambda b, h: (b, h, 0, 0))
    return pl.pallas_call(
        functools.partial(_ctx_attn_kernel, scale=hd ** -0.5),
        grid=(nb, nh),
        in_specs=[pl.BlockSpec((seq, hd), lambda b, h: (base + b, q_col + h)),
                  pl.BlockSpec((seq, hd), lambda b, h: (base + b, q_col + nh + h)),
                  pl.BlockSpec((seq, hd), lambda b, h: (base + b, q_col + 2 * nh + h))],
        out_specs=[pl.BlockSpec((seq, hd), lambda b, h: (b, h)), kv_spec, kv_spec],
        out_shape=[jax.ShapeDtypeStruct((nb * seq, nh * hd), BF16),
                   jax.ShapeDtypeStruct((nb, nh, seq, hd), F32),
                   jax.ShapeDtypeStruct((nb, nh, seq, hd), F32)],
        compiler_params=_cparams("parallel", "parallel"),
        name="context_attention",
    )(proj, proj, proj)


def _na_key_row0(kb, rows):
    lo = kb * NA_Q_ROWS - NA_WIN_ROWS // 2
    if isinstance(kb, (int, np.integer)):
        return int(np.clip(lo, 0, rows - NA_K_ROWS))
    return jnp.clip(lo, 0, rows - NA_K_ROWS)


def _na_bias_tables(rpb, rows):
    nh = rpb.shape[0]
    w = GRID_W
    c = np.arange(w)
    cs = np.clip(c - NA_WIN_COLS // 2, 0, w - NA_WIN_COLS)
    col_ok = (c[None, :] >= cs[:, None]) & (c[None, :] < cs[:, None] + NA_WIN_COLS)
    rel_col = np.clip(c[None, :] - c[:, None] + NA_WIN_COLS - 1, 0, 2 * NA_WIN_COLS - 2)
    planes = jnp.where(col_ok[None, None], rpb[:, :, rel_col], MASK_VALUE)
    planes = jnp.concatenate([planes, jnp.full((nh, 1, w, w), MASK_VALUE, rpb.dtype)], axis=1)
    masked_plane = 2 * NA_WIN_ROWS - 1
    variants, var_ids = [], []
    for kb in range(rows // NA_Q_ROWS):
        r = kb * NA_Q_ROWS + np.arange(NA_Q_ROWS)
        rs = np.clip(r - NA_WIN_ROWS // 2, 0, rows - NA_WIN_ROWS)
        kr = _na_key_row0(kb, rows) + np.arange(NA_K_ROWS)
        ok = (kr[None, :] >= rs[:, None]) & (kr[None, :] < rs[:, None] + NA_WIN_ROWS)
        assert ok.sum() == NA_Q_ROWS * NA_WIN_ROWS
        plane = np.where(ok, kr[None, :] - r[:, None] + NA_WIN_ROWS - 1, masked_plane)
        for vi, known in enumerate(variants):
            if np.array_equal(known, plane):
                var_ids.append(vi)
                break
        else:
            var_ids.append(len(variants))
            variants.append(plane)
    tbl = pl.pallas_call(
        functools.partial(_na_table_kernel, variants=[v.tolist() for v in variants]),
        grid=(nh,),
        in_specs=[pl.BlockSpec((None, masked_plane + 1, w, w), lambda h: (h, 0, 0, 0))],
        out_specs=pl.BlockSpec((None, len(variants), NA_Q_ROWS * w, NA_K_ROWS * w),
                               lambda h: (h, 0, 0, 0)),
        out_shape=jax.ShapeDtypeStruct((nh, len(variants), NA_Q_ROWS * w, NA_K_ROWS * w), rpb.dtype),
        compiler_params=_cparams("parallel"),
        name="na_bias_table",
    )(planes)
    return tbl, jnp.asarray(np.array(var_ids, np.int32))


def _na_table_kernel(planes_ref, o_ref, *, variants):
    w = GRID_W
    for v, plane_of in enumerate(variants):
        for qr in range(NA_Q_ROWS):
            for kr in range(0, NA_K_ROWS, 2):
                pair = jnp.concatenate([planes_ref[plane_of[qr][kr]], planes_ref[plane_of[qr][kr + 1]]], axis=1)
                o_ref[v, qr * w:(qr + 1) * w, kr * w:(kr + 2) * w] = pair


def _na_attn_kernel(var_ref, q_ref, k_ref, v_ref, kc_ref, vc_ref, bias_ref, o_ref, *, rows, scale):
    del var_ref
    nk = NA_K_ROWS * GRID_W
    start = pl.multiple_of(_na_key_row0(pl.program_id(2), rows) * GRID_W, 256)
    kw = k_ref[pl.ds(start, nk), :].astype(BF16)
    vw = v_ref[pl.ds(start, nk), :].astype(BF16)
    q = q_ref[...].astype(BF16)
    s_loc = _nt_dot(q, kw) * scale + bias_ref[...]
    s_ctx = _nt_dot(q, kc_ref[...].astype(BF16)) * scale
    m = jnp.maximum(jnp.max(s_loc, axis=-1, keepdims=True), jnp.max(s_ctx, axis=-1, keepdims=True))
    p_loc = jnp.exp(s_loc - m)
    p_ctx = jnp.exp(s_ctx - m)
    l = jnp.sum(p_loc, axis=-1, keepdims=True) + jnp.sum(p_ctx, axis=-1, keepdims=True)
    o = (jnp.dot(p_loc.astype(BF16), vw, preferred_element_type=F32)
         + jnp.dot(p_ctx.astype(BF16), vc_ref[...].astype(BF16), preferred_element_type=F32))
    o_ref[...] = (o / l).astype(o_ref.dtype)


def _na_attn_call(proj, cache_k, cache_v, layer, rpb, nb, dec_seq, q_col, hd):
    rows = dec_seq // GRID_W
    assert rows >= NA_K_ROWS and rows % NA_Q_ROWS == 0
    nh = NA_HEADS
    nq = NA_Q_ROWS * GRID_W
    nkb = rows // NA_Q_ROWS
    past = cache_k.shape[3]
    tbl, var_ids = _na_bias_tables(rpb, rows)
    ctx_spec = pl.BlockSpec((None, None, None, past, hd), lambda b, h, k, var: (b, layer, h, 0, 0))
    grid_spec = pltpu.PrefetchScalarGridSpec(
        num_scalar_prefetch=1,
        grid=(nb, nh, nkb),
        in_specs=[pl.BlockSpec((nq, hd), lambda b, h, k, var: (b * nkb + k, q_col + h)),
                  pl.BlockSpec((dec_seq, hd), lambda b, h, k, var: (b, q_col + nh + h)),
                  pl.BlockSpec((dec_seq, hd), lambda b, h, k, var: (b, q_col + 2 * nh + h)),
                  ctx_spec, ctx_spec,
                  pl.BlockSpec((None, None, nq, NA_K_ROWS * GRID_W),
                               lambda b, h, k, var: (h, var[k], 0, 0))],
        out_specs=pl.BlockSpec((nq, hd), lambda b, h, k, var: (b * nkb + k, h)),
    )
    return pl.pallas_call(
        functools.partial(_na_attn_kernel, rows=rows, scale=hd ** -0.5),
        grid_spec=grid_spec,
        out_shape=jax.ShapeDtypeStruct((nb * dec_seq, nh * hd), BF16),
        compiler_params=_cparams("parallel", "parallel", "arbitrary"),
        name="neighbourhood_attention",
    )(var_ids, proj, proj, proj, cache_k, cache_v, tbl)


def _merge_kernel(fl_ref, fc_ref, p_ref, al_ref, ac_ref, c_ref, g0_ref, g1_ref, g2_ref, g3_ref, w_ref,
                  o_ref, *, n_lat_tiles):
    is_lat = pl.program_id(0) < n_lat_tiles
    f = jnp.where(is_lat, fl_ref[...], fc_ref[...])
    a = jnp.where(is_lat, al_ref[...], ac_ref[...])
    acc = None
    for n, (br, gt) in enumerate(((f, g0_ref), (p_ref[...], g1_ref), (a, g2_ref), (c_ref[...], g3_ref))):
        y = gt[...].astype(F32) * jnp.dot(br, w_ref[n], preferred_element_type=F32)
        acc = y if acc is None else acc + y
    o_ref[...] = acc.astype(o_ref.dtype)


def _merge_call(f_lat, f_ctx, o_p, a_lat, a_ctx, o_c, gates, w_branch, tm, tn):
    t, bw = o_p.shape
    d = w_branch.shape[2]
    nj = d // tn
    nl = f_lat.shape[0] // tm
    br_spec = pl.BlockSpec((tm, bw), lambda i, j: (i, 0))
    lat_spec = pl.BlockSpec((tm, bw), lambda i, j: (jnp.minimum(i, nl - 1), 0))
    ctx_spec = pl.BlockSpec((tm, bw), lambda i, j: (jnp.maximum(i - nl, 0), 0))
    gate_specs = [pl.BlockSpec((tm, tn), functools.partial(lambda i, j, n: (i, n * nj + j), n=n))
                  for n in range(N_BRANCH)]
    return pl.pallas_call(
        functools.partial(_merge_kernel, n_lat_tiles=nl),
        grid=(t // tm, nj),
        in_specs=[lat_spec, ctx_spec, br_spec, lat_spec, ctx_spec, br_spec] + gate_specs
                 + [pl.BlockSpec((N_BRANCH, bw, tn), lambda i, j: (0, 0, j))],
        out_specs=pl.BlockSpec((tm, tn), lambda i, j: (i, j)),
        out_shape=jax.ShapeDtypeStruct((t, d), BF16),
        compiler_params=_cparams("parallel", "parallel"),
        name="branch_merge",
    )(f_lat, f_ctx, o_p, a_lat, a_ctx, o_c, gates, gates, gates, gates, w_branch)


def _out_kernel(a_ref, w_ref, x_hbm, mod_ref, g_ref, *rest, tm, gate_row, next_rows):
    if next_rows is None:
        o_ref, xbuf_ref, sem = rest
    else:
        nmod_ref, ng_ref, o_ref, h_ref, xbuf_ref, sem = rest
    k = pl.program_id(1)
    x_copy = pltpu.make_async_copy(x_hbm.at[pl.ds(pl.program_id(0) * tm, tm)], xbuf_ref, sem)

    @pl.when(k == 0)
    def _():
        x_copy.start()
        o_ref[...] = jnp.zeros_like(o_ref)

    o_ref[...] += jnp.dot(a_ref[...], w_ref[...], preferred_element_type=F32)

    @pl.when(k == pl.num_programs(1) - 1)
    def _():
        x_copy.wait()
        xn = xbuf_ref[...] + mod_ref[gate_row:gate_row + 1, :] * _rms(o_ref[...], g_ref[...])
        o_ref[...] = xn
        if next_rows is not None:
            h_ref[...] = _mod_norm(xn, ng_ref[...], nmod_ref, *next_rows).astype(h_ref.dtype)


def _k_tile(kdim, cap):
    lanes = 128
    return max(t for t in range(lanes, cap + 1, lanes) if kdim % t == 0)


def _out_call(a, w, x, mod, g, gate_row, nxt, mod_row, tm):
    t, d = x.shape
    kdim = a.shape[1]
    tk = _k_tile(kdim, 1024)
    row = pl.BlockSpec((tm, d), lambda i, k: (i, 0))
    mod_spec = pl.BlockSpec((None, 6, d), lambda i, k: (mod_row(i * tm), 0, 0))
    vec = pl.BlockSpec((1, d), lambda i, k: (0, 0))
    in_specs = [pl.BlockSpec((tm, tk), lambda i, k: (i, k)),
                pl.BlockSpec((tk, d), lambda i, k: (k, 0)),
                pl.BlockSpec(memory_space=pl.ANY), mod_spec, vec]
    args = [a, w, x, mod, g]
    out_specs, out_shape = row, jax.ShapeDtypeStruct((t, d), F32)
    if nxt is not None:
        in_specs += [mod_spec, vec]
        args += [nxt[0], nxt[1]]
        out_specs = [row, row]
        out_shape = [out_shape, jax.ShapeDtypeStruct((t, d), BF16)]
    return pl.pallas_call(
        functools.partial(_out_kernel, tm=tm, gate_row=gate_row,
                          next_rows=None if nxt is None else nxt[2]),
        grid=(t // tm, kdim // tk),
        in_specs=in_specs,
        out_specs=out_specs,
        out_shape=out_shape,
        scratch_shapes=[pltpu.VMEM((tm, d), F32), pltpu.SemaphoreType.DMA(())],
        compiler_params=_cparams("arbitrary", "arbitrary", vmem_mib=VMEM_LIMIT_WIDE_MIB),
        name="proj_norm_residual",
    )(*args)


def _ffn_up_kernel(h_ref, wg_ref, wu_ref, *rest, n_cast):
    o_ref = rest[n_cast]
    _cast_stream_step(rest[:n_cast], rest[n_cast + 1:])
    h = h_ref[...]
    y = _silu(jnp.dot(h, wg_ref[...], preferred_element_type=F32)) * jnp.dot(
        h, wu_ref[...], preferred_element_type=F32)
    o_ref[...] = y.astype(o_ref.dtype)


def _ffn_up_call(h, wg, wu, tm, tn, cast_srcs):
    t, d = h.shape
    f = wg.shape[1]
    nj = f // tn
    w_spec = pl.BlockSpec((d, tn), lambda i, j: (0, j))
    cast_in, cast_out, cast_shapes = _cast_stream_specs(cast_srcs, (t // tm) * nj, nj)
    return pl.pallas_call(
        functools.partial(_ffn_up_kernel, n_cast=len(cast_srcs)),
        grid=(t // tm, nj),
        in_specs=[pl.BlockSpec((tm, d), lambda i, j: (i, 0)), w_spec, w_spec] + cast_in,
        out_specs=[pl.BlockSpec((tm, tn), lambda i, j: (i, j))] + cast_out,
        out_shape=[jax.ShapeDtypeStruct((t, f), BF16)] + cast_shapes,
        compiler_params=_cparams("arbitrary", "arbitrary"),
        name="ffn_up",
    )(h, wg, wu, *[src[0] for src in cast_srcs])


def _split_bf16(x):
    hi = x.astype(BF16)
    return hi, (x - hi.astype(F32)).astype(BF16)


def _pack_bf16_pairs(h):
    n = h.shape[1] // 2
    hb = h.astype(BF16).astype(F32)
    hi = pltpu.bitcast(hb[:, :n], jnp.int32)
    lo = pltpu.bitcast(hb[:, n:], jnp.int32)
    return hi | lax.shift_right_logical(lo, 16)


def _unpack_bf16_pairs(w):
    hi = pltpu.bitcast(w & jnp.int32(-65536), F32).astype(BF16)
    lo = pltpu.bitcast(w << 16, F32).astype(BF16)
    return hi, lo


def _router_kernel(x_ref, mod_ref, g_ref, w_ref, b_ref, o_ref, hp_ref, *, n_experts):
    h = _mod_norm(x_ref[...], g_ref[...], mod_ref, SCALE2, SHIFT2)
    hp_ref[...] = _pack_bf16_pairs(h)
    h_hi, h_lo = _split_bf16(h)
    w_hi, w_lo = _split_bf16(w_ref[...])
    logits = (jnp.dot(h_hi, w_hi, preferred_element_type=F32)
              + jnp.dot(h_lo, w_hi, preferred_element_type=F32)
              + jnp.dot(h_hi, w_lo, preferred_element_type=F32)) + b_ref[...]
    lane = lax.broadcasted_iota(jnp.int32, logits.shape, 1).astype(F32)
    neg = -jnp.inf
    no_lane = float(logits.shape[1])
    logits = jnp.where(lane < n_experts, logits, neg)
    m1 = jnp.max(logits, axis=-1, keepdims=True)
    i1 = jnp.min(jnp.where(logits == m1, lane, no_lane), axis=-1, keepdims=True)
    rest = jnp.where(lane == i1, neg, logits)
    m2 = jnp.max(rest, axis=-1, keepdims=True)
    i2 = jnp.min(jnp.where(rest == m2, lane, no_lane), axis=-1, keepdims=True)
    e2 = jnp.exp(m2 - m1)
    den = 1.0 + e2
    o_ref[...] = (jnp.where(lane == 0.0, i1, 0.0) + jnp.where(lane == 1.0, i2, 0.0)
                  + jnp.where(lane == 2.0, 1.0 / den, 0.0) + jnp.where(lane == 3.0, e2 / den, 0.0))


def _router_call(x, mod, g, w_router, b_router, mod_row, tm):
    t, d = x.shape
    ne = w_router.shape[1]
    lanes = 128
    w = jnp.zeros((d, lanes), F32).at[:, :ne].set(w_router)
    b = jnp.zeros((1, lanes), F32).at[0, :ne].set(b_router)
    return pl.pallas_call(
        functools.partial(_router_kernel, n_experts=ne),
        grid=(t // tm,),
        in_specs=[pl.BlockSpec((tm, d), lambda i: (i, 0)),
                  pl.BlockSpec((None, 6, d), lambda i: (mod_row(i * tm), 0, 0)),
                  pl.BlockSpec((1, d), lambda i: (0, 0)),
                  pl.BlockSpec((d, lanes), lambda i: (0, 0)),
                  pl.BlockSpec((1, lanes), lambda i: (0, 0))],
        out_specs=[pl.BlockSpec((tm, lanes), lambda i: (i, 0)),
                   pl.BlockSpec((tm, d // 2), lambda i: (i, 0))],
        out_shape=[jax.ShapeDtypeStruct((t, lanes), F32),
                   jax.ShapeDtypeStruct((t, d // 2), jnp.int32)],
        compiler_params=_cparams("parallel"),
        name="router",
    )(x, mod, g, w, b)


def _routing_tables(route, n_experts, tm):
    t = route.shape[0]
    na = TOP_K * t
    p_rows = na + n_experts * tm
    nt = p_rows // tm
    e_flat = route[:, :TOP_K].astype(jnp.int32).reshape(na)
    onehot = (e_flat[:, None] == jnp.arange(n_experts, dtype=jnp.int32)[None, :]).astype(jnp.int32)
    csum = jnp.cumsum(onehot, axis=0)
    rank = jnp.sum((csum - onehot) * onehot, axis=1)
    counts = csum[-1]
    padded = ((counts + tm - 1) // tm) * tm
    ends = jnp.cumsum(padded)
    pos = (ends - padded)[e_flat] + rank
    src = jnp.zeros((p_rows,), jnp.int32).at[pos].set(
        jnp.arange(na, dtype=jnp.int32) // TOP_K, unique_indices=True, mode="promise_in_bounds")
    tile_expert = jnp.minimum(
        jnp.searchsorted(ends, jnp.arange(nt, dtype=jnp.int32) * tm, side="right"), n_experts - 1)
    meta = jnp.concatenate([tile_expert.astype(jnp.int32), (ends[-1:] // tm).astype(jnp.int32)])
    return src, pos.astype(jnp.int32), meta


def _gather_rows_kernel(idx_ref, src_ref, o_ref, buf_ref, sem_ref, *, rows):
    i = pl.program_id(0)
    n = pl.num_programs(0)

    def issue(tile, slot):
        def body(r, carry):
            pltpu.make_async_copy(src_ref.at[pl.ds(idx_ref[tile * rows + r], 1)],
                                  buf_ref.at[slot, pl.ds(r, 1)], sem_ref.at[slot]).start()
            return carry
        lax.fori_loop(0, rows, body, 0, unroll=DMA_ISSUE_UNROLL)

    @pl.when(i == 0)
    def _():
        issue(0, 0)

    @pl.when(i + 1 < n)
    def _():
        issue(i + 1, (i + 1) % 2)

    slot = i % 2
    pltpu.make_async_copy(src_ref.at[pl.ds(0, rows)], buf_ref.at[slot], sem_ref.at[slot]).wait()
    half = buf_ref.shape[2]
    hi, lo = _unpack_bf16_pairs(buf_ref[slot])
    o_ref[:, :half] = hi
    o_ref[:, half:] = lo


def _gather_rows_call(src, idx, rows):
    p_rows = idx.shape[0]
    half = src.shape[1]
    grid_spec = pltpu.PrefetchScalarGridSpec(
        num_scalar_prefetch=1,
        grid=(p_rows // rows,),
        in_specs=[pl.BlockSpec(memory_space=pl.ANY)],
        out_specs=pl.BlockSpec((rows, 2 * half), lambda i, idx: (i, 0)),
        scratch_shapes=[pltpu.VMEM((2, rows, half), src.dtype), pltpu.SemaphoreType.DMA((2,))],
    )
    return pl.pallas_call(
        functools.partial(_gather_rows_kernel, rows=rows),
        grid_spec=grid_spec,
        out_shape=jax.ShapeDtypeStruct((p_rows, 2 * half), BF16),
        compiler_params=_cparams("arbitrary"),
        name="moe_gather",
    )(idx, src)


def _moe_up_kernel(meta_ref, xs_ref, wg_ref, wu_ref, o_ref, wgb_ref, wub_ref, *, n_tiles):
    i = pl.program_id(1)
    used = i < meta_ref[n_tiles]
    fresh = jnp.logical_or(i == 0, meta_ref[i] != meta_ref[jnp.maximum(i - 1, 0)])

    @pl.when(jnp.logical_and(used, fresh))
    def _():
        wgb_ref[...] = wg_ref[...].astype(BF16)
        wub_ref[...] = wu_ref[...].astype(BF16)

    @pl.when(used)
    def _():
        h = xs_ref[...]
        y = _silu(jnp.dot(h, wgb_ref[...], preferred_element_type=F32)) * jnp.dot(
            h, wub_ref[...], preferred_element_type=F32)
        o_ref[...] = y.astype(o_ref.dtype)

    @pl.when(jnp.logical_not(used))
    def _():
        o_ref[...] = jnp.zeros_like(o_ref)


def _moe_up_call(xs, meta, wg, wu, tm, tn):
    p_rows, d = xs.shape
    f = wg.shape[2]
    nt = p_rows // tm
    w_spec = pl.BlockSpec((None, d, tn), lambda j, i, meta: (meta[i], 0, j))
    grid_spec = pltpu.PrefetchScalarGridSpec(
        num_scalar_prefetch=1,
        grid=(f // tn, nt),
        in_specs=[pl.BlockSpec((tm, d), lambda j, i, meta: (i, 0)), w_spec, w_spec],
        out_specs=pl.BlockSpec((None, tm, tn), lambda j, i, meta: (j, i, 0)),
        scratch_shapes=[pltpu.VMEM((d, tn), BF16), pltpu.VMEM((d, tn), BF16)],
    )
    return pl.pallas_call(
        functools.partial(_moe_up_kernel, n_tiles=nt),
        grid_spec=grid_spec,
        out_shape=jax.ShapeDtypeStruct((f // tn, p_rows, tn), BF16),
        compiler_params=_cparams("arbitrary", "arbitrary", vmem_mib=VMEM_LIMIT_WIDE_MIB),
        name="moe_up",
    )(meta, xs, wg, wu)


def _moe_down_kernel(meta_ref, a_ref, w_ref, o_ref, *, n_tiles):
    i = pl.program_id(0)

    @pl.when(pl.program_id(1) == 0)
    def _():
        o_ref[...] = jnp.zeros_like(o_ref)

    @pl.when(i < meta_ref[n_tiles])
    def _():
        nsub, _, tk = a_ref.shape
        acc = o_ref[...]
        for s in range(nsub):
            acc = acc + jnp.dot(a_ref[s], w_ref[s * tk:(s + 1) * tk, :], preferred_element_type=F32)
        o_ref[...] = acc


def _moe_down_call(a, meta, wd, tm, nsub):
    nk, p_rows, tk = a.shape
    d = wd.shape[2]
    nt = p_rows // tm
    grid_spec = pltpu.PrefetchScalarGridSpec(
        num_scalar_prefetch=1,
        grid=(nt, nk // nsub),
        in_specs=[pl.BlockSpec((nsub, tm, tk), lambda i, k, meta: (k, i, 0)),
                  pl.BlockSpec((None, nsub * tk, d), lambda i, k, meta: (meta[i], k, 0))],
        out_specs=pl.BlockSpec((tm, d), lambda i, k, meta: (i, 0)),
    )
    return pl.pallas_call(
        functools.partial(_moe_down_kernel, n_tiles=nt),
        grid_spec=grid_spec,
        out_shape=jax.ShapeDtypeStruct((p_rows, d), F32),
        compiler_params=_cparams("parallel", "arbitrary"),
        name="moe_down",
    )(meta, a, wd)


def _moe_combine_kernel(pos_ref, ys_ref, route_ref, x_ref, mod_ref, g_ref, *rest, rows, n_lat_tiles):
    if n_lat_tiles is None:
        o_ref, buf_ref, sem_ref = rest
    else:
        lat_ref, ctx_ref, buf_ref, sem_ref = rest
    i = pl.program_id(0)
    n = pl.num_programs(0)

    def issue(tile, slot):
        def body(r, carry):
            for s in range(TOP_K):
                pltpu.make_async_copy(ys_ref.at[pl.ds(pos_ref[TOP_K * (tile * rows + r) + s], 1)],
                                      buf_ref.at[slot, s, pl.ds(r, 1)], sem_ref.at[slot]).start()
            return carry
        lax.fori_loop(0, rows, body, 0, unroll=DMA_ISSUE_UNROLL)

    @pl.when(i == 0)
    def _():
        issue(0, 0)

    @pl.when(i + 1 < n)
    def _():
        issue(i + 1, (i + 1) % 2)

    slot = i % 2
    for s in range(TOP_K):
        pltpu.make_async_copy(ys_ref.at[pl.ds(0, rows)], buf_ref.at[slot, s], sem_ref.at[slot]).wait()
    route = route_ref[...]
    y = None
    for s in range(TOP_K):
        term = route[:, TOP_K + s:TOP_K + s + 1] * buf_ref[slot, s]
        y = term if y is None else y + term
    res = x_ref[...] + mod_ref[GATE2:GATE2 + 1, :] * _rms(y, g_ref[...])
    if n_lat_tiles is None:
        o_ref[...] = res
    else:
        @pl.when(i < n_lat_tiles)
        def _():
            lat_ref[...] = res

        @pl.when(i >= n_lat_tiles)
        def _():
            ctx_ref[...] = res


def _moe_combine_call(ys, pos, route, x, mod, g, mod_row, rows, split_rows):
    t, d = x.shape
    row = pl.BlockSpec((rows, d), lambda i, pos: (i, 0))
    if split_rows is None:
        nl = None
        out_specs, out_shape = row, jax.ShapeDtypeStruct((t, d), F32)
    else:
        nl = split_rows // rows
        out_specs = [pl.BlockSpec((rows, d), lambda i, pos: (jnp.minimum(i, nl - 1), 0)),
                     pl.BlockSpec((rows, d), lambda i, pos: (jnp.maximum(i - nl, 0), 0))]
        out_shape = [jax.ShapeDtypeStruct((split_rows, d), F32),
                     jax.ShapeDtypeStruct((t - split_rows, d), F32)]
    grid_spec = pltpu.PrefetchScalarGridSpec(
        num_scalar_prefetch=1,
        grid=(t // rows,),
        in_specs=[pl.BlockSpec(memory_space=pl.ANY),
                  pl.BlockSpec((rows, route.shape[1]), lambda i, pos: (i, 0)),
                  row,
                  pl.BlockSpec((None, 6, d), lambda i, pos: (mod_row(i * rows), 0, 0)),
                  pl.BlockSpec((1, d), lambda i, pos: (0, 0))],
        out_specs=out_specs,
        scratch_shapes=[pltpu.VMEM((2, TOP_K, rows, d), F32), pltpu.SemaphoreType.DMA((2,))],
    )
    return pl.pallas_call(
        functools.partial(_moe_combine_kernel, rows=rows, n_lat_tiles=nl),
        grid_spec=grid_spec,
        out_shape=out_shape,
        compiler_params=_cparams("arbitrary"),
        name="moe_combine",
    )(pos, ys, route, x, mod, g)


def kernel(x_prompt, x_sample, cache_k, cache_v, c, c_ctx, w_ada, b_ada, norm_g, w_in, b_in, pool_w, pool_scale, rpb, conv_w, conv_b, conv_ln_g, conv_ln_b, w_branch, w_out, w_gate_d, w_up_d, w_down_d, w_router, b_router, w_gate_e, w_up_e, w_down_e):
    nbp, seq, d = x_prompt.shape
    nbs, dec_seq, _ = x_sample.shape
    depth = w_ada.shape[0]
    bw = d // N_BRANCH
    hd = bw // NA_HEADS
    ns_rows = nbs * dec_seq
    np_rows = nbp * seq
    assert dec_seq & (dec_seq - 1) == 0 and seq & (seq - 1) == 0
    assert nbs + 1 <= MOD_ROWS

    t_rows = ns_rows + np_rows
    tm_wide = max(tm for tm in (2048, 1024, 512) if dec_seq % tm == 0 and np_rows % tm == 0)
    tm_mid = min(tm_wide, 1024)
    tm_moe = 512
    tile_seq = min(256, seq)
    tn = 512
    assert t_rows % tm_wide == 0 and seq % tile_seq == 0

    def mod_row(row0):
        return jnp.where(row0 < ns_rows, row0 // dec_seq, nbs)

    cvec =jnp.zeros((MOD_ROWS, d), F32).at[:nbs].set(c).at[nbs].set(c_ctx)
    mods = _ada_call(cvec, w_ada, b_ada).reshape(depth, MOD_ROWS, 6, d)
    cc, sc = _channel_dft_mats(bw, FNET_GROUPS)

    pool_col, conv_col, main_cols = 1, 5, 7 * bw
    q_col = 2 * bw // hd

    def gain(l, n):
        return norm_g[l, n][None, :]

    x, h = _entry_call(x_sample.reshape(ns_rows, d), x_prompt.reshape(np_rows, d), mods[0], gain(0, 0),
                       mod_row, 512)
    w_in_cur = w_in[0].astype(BF16)[None]
    new_k, new_v = [], []
    y_split = None
    for l in range(depth):
        mod = mods[l]
        last = l + 1 == depth
        i = l // 2
        if l % 2 == 1:
            mixer_w = [w_down_e[i].reshape(-1, d)]
        else:
            mixer_w = [w_gate_d[i], w_up_d[i], w_down_d[i]]
        mixer_w = [(a, 0, a.shape[0]) for a in mixer_w]
        proj, gates, *mixer_w = _in_call(h, w_in_cur, b_in[l][None, :], 0, main_cols, tm_wide, tn, mixer_w)

        f_lat = _fourier_call(proj, 0, nbs, dec_seq, bw, cc, sc)
        f_ctx = _fourier_call(proj, ns_rows, nbp, seq, bw, cc, sc)
        o_p = _pool_call(proj, pool_w[l].astype(BF16), pool_scale[l][None, :], pool_col, bw,
                         tile_seq, ns_rows, dec_seq, seq)
        o_c = _conv_call(proj, conv_w[l], conv_b[l][None, :], conv_ln_g[l][None, :],
                         conv_ln_b[l][None, :], conv_col, bw, tile_seq, ns_rows, dec_seq, seq)
        a_lat = _na_attn_call(proj, cache_k, cache_v, l, rpb[l], nbs, dec_seq, q_col, hd)
        a_ctx, k_ctx, v_ctx = _ctx_attn_call(proj, ns_rows, nbp, seq, q_col, hd)
        new_k.append(k_ctx)
        new_v.append(v_ctx)

        merged = _merge_call(f_lat, f_ctx, o_p, a_lat, a_ctx, o_c, gates, w_branch[l].astype(BF16),
                             tm_mid, tn)
        nxt = None if last else (mods[l + 1], gain(l + 1, 0), (SCALE1, SHIFT1))
        if l % 2 == 1:
            x = _out_call(merged, w_out[l].astype(BF16), x, mod, gain(l, 1), GATE1, None,
                          mod_row, tm_mid)
            ne = w_router.shape[2]
            route, h_packed = _router_call(x, mod, gain(l, 2), w_router[i], b_router[i], mod_row, tm_mid)
            src, pos, meta = _routing_tables(route, ne, tm_moe)
            xs = _gather_rows_call(h_packed, src, tm_moe)
            hmid = _moe_up_call(xs, meta, w_gate_e[i], w_up_e[i], tm_moe, 2 * tn)
            ys = _moe_down_call(hmid, meta, mixer_w[0].reshape(w_down_e.shape[1:]), tm_moe, 1)
            out = _moe_combine_call(ys, pos, route, x, mod, gain(l, 3), mod_row, tm_moe,
                                    ns_rows if last else None)
            if last:
                y_split = out
            else:
                x = out
                h = _norm_call(x, nxt[0], nxt[1], *nxt[2], mod_row, tm_mid)
                w_in_cur = w_in[l + 1].astype(BF16)[None]
        else:
            wg, wu, wd = mixer_w
            x, h2 = _out_call(merged, w_out[l].astype(BF16), x, mod, gain(l, 1), GATE1,
                              (mod, gain(l, 2), (SCALE2, SHIFT2)), mod_row, tm_mid)
            if last:
                (hmid,) = _ffn_up_call(h2, wg, wu, tm_wide, tn, [])
                x = _out_call(hmid, wd, x, mod, gain(l, 3), GATE2, None, mod_row, tm_mid)
            else:
                hmid, w_in_next = _ffn_up_call(h2, wg, wu, tm_wide, tn,
                                               [(w_in.reshape(depth * d, -1), (l + 1) * d, d)])
                w_in_cur = w_in_next[None]
                x, h = _out_call(hmid, wd, x, mod, gain(l, 3), GATE2, nxt, mod_row, tm_mid)

    if y_split is None:
        y_split = x[:ns_rows], x[ns_rows:]
    y_sample = y_split[0].reshape(nbs, dec_seq, d)
    y_prompt = y_split[1].reshape(nbp, seq, d)
    return y_prompt, y_sample, jnp.stack(new_k, axis=1), jnp.stack(new_v, axis=1)
```
